```python
import jax, jax.numpy as jnp
from jax import lax
import numpy as np

D_MODEL = 2048
BATCH = 2
SEQ = 16384
DEPTH = 1

GRID_W = 64
N_META = 16
EPS = 1e-6
NEG_INF = -1e30
NA_HEADS = 8
NA_HEAD_DIM = 128
NA_WIN_ROWS = 8
NA_WIN_COLS = 16
NA_COL_BLOCK = 16
NA_KEY_COLS = NA_COL_BLOCK + NA_WIN_COLS
MLA_HEADS = 8
MLA_Q_RANK = 512
MLA_KV_RANK = 256
MLA_NOPE_DIM = 128
MLA_ROPE_DIM = 64
MLA_V_DIM = 128
MLA_QK_DIM = MLA_NOPE_DIM + MLA_ROPE_DIM
ROPE_THETA = 10000.0
Q_BLOCK = 128
D_FF = 5632
CONV_W = 3

NA_WIDTH = NA_HEADS * NA_HEAD_DIM
MLA_WIDTH = MLA_HEADS * MLA_V_DIM
D_MIX = NA_WIDTH + MLA_WIDTH
IN_COLS = 3 * NA_WIDTH + MLA_Q_RANK + MLA_KV_RANK + MLA_ROPE_DIM
IN_SPLITS = [NA_WIDTH, 2 * NA_WIDTH, 3 * NA_WIDTH, 3 * NA_WIDTH + MLA_Q_RANK, 3 * NA_WIDTH + MLA_Q_RANK + MLA_KV_RANK]

kernel_name = "hybrid_na_mla_convffn_encoder"


def rmsnorm(x, g):
    xf = x.astype(jnp.float32)
    y = xf * lax.rsqrt(jnp.mean(xf * xf, axis=-1, keepdims=True) + EPS)
    return (y * g.astype(jnp.float32)).astype(x.dtype)


def rope_tables(length, dtype):
    inv = ROPE_THETA ** (-jnp.arange(0, MLA_ROPE_DIM, 2, dtype=jnp.float32) / MLA_ROPE_DIM)
    ang = jnp.arange(length, dtype=jnp.float32)[:, None] * inv[None, :]
    return jnp.cos(ang)[:, None, :].astype(dtype), jnp.sin(ang)[:, None, :].astype(dtype)


def apply_rope(x, cos, sin):
    half = x.shape[-1] // 2
    x1, x2 = x[..., :half], x[..., half:]
    return jnp.concatenate([x1 * cos - x2 * sin, x2 * cos + x1 * sin], axis=-1)


def neighbourhood_attention(q, k, v, rpb, meta_bias):
    B, L, H, D = q.shape
    T = L - N_META
    rows = T // GRID_W
    kh = min(NA_WIN_ROWS, rows)
    scale = D ** -0.5
    n_cb = GRID_W // NA_COL_BLOCK
    qm, km, vm = q[:, :N_META], k[:, :N_META], v[:, :N_META]

    s_mm = jnp.einsum('bqhd,bkhd->bhqk', qm, km).astype(jnp.float32) * scale + meta_bias.astype(jnp.float32)[None, :, None, :]
    p_mm = jax.nn.softmax(s_mm, axis=-1).astype(vm.dtype)
    meta_out = jnp.einsum('bhqk,bkhd->bqhd', p_mm, vm)

    qg = q[:, N_META:].reshape(B, rows, GRID_W, H, D)
    kg = k[:, N_META:].reshape(B, rows, GRID_W, H, D)
    vg = v[:, N_META:].reshape(B, rows, GRID_W, H, D)

    qcols = np.arange(GRID_W).reshape(n_cb, NA_COL_BLOCK)
    col_start = np.clip(qcols - NA_WIN_COLS // 2, 0, GRID_W - NA_WIN_COLS)
    kblk_start = np.clip(np.arange(n_cb) * NA_COL_BLOCK - NA_WIN_COLS // 2, 0, GRID_W - NA_KEY_COLS)
    key_cols = kblk_start[:, None] + np.arange(NA_KEY_COLS)[None, :]
    col_valid = (key_cols[:, None, :] >= col_start[:, :, None]) & (key_cols[:, None, :] < col_start[:, :, None] + NA_WIN_COLS)
    col_mask = jnp.asarray(np.where(col_valid, 0.0, NEG_INF), jnp.float32)
    dc_idx = np.clip(key_cols[:, None, :] - qcols[:, :, None] + NA_WIN_COLS - 1, 0, 2 * NA_WIN_COLS - 2)
    rpb32 = rpb.astype(jnp.float32)
    mbias = meta_bias.astype(jnp.float32)[None, None, None]

    def row_block(r):
        rs = jnp.clip(r - kh // 2, 0, rows - kh)
        kb = lax.dynamic_slice_in_dim(kg, rs, kh, axis=1)[:, :, key_cols]
        vb = lax.dynamic_slice_in_dim(vg, rs, kh, axis=1)[:, :, key_cols]
        qr = lax.dynamic_index_in_dim(qg, r, axis=1, keepdims=False).reshape(B, n_cb, NA_COL_BLOCK, H, D)
        s_win = jnp.einsum('bjqhd,bijkhd->bjqhik', qr, kb).astype(jnp.float32) * scale
        dr_idx = rs + jnp.arange(kh) - r + NA_WIN_ROWS - 1
        bias = rpb32[:, dr_idx][:, :, dc_idx].transpose(2, 3, 0, 1, 4)
        s_win = s_win + bias[None] + col_mask[None, :, :, None, None, :]
        s_met = jnp.einsum('bjqhd,bmhd->bjqhm', qr, km).astype(jnp.float32) * scale + mbias
        s = jnp.concatenate([s_met, s_win.reshape(B, n_cb, NA_COL_BLOCK, H, kh * NA_KEY_COLS)], axis=-1)
        p = jax.nn.softmax(s, axis=-1).astype(vb.dtype)
        p_win = p[..., N_META:].reshape(B, n_cb, NA_COL_BLOCK, H, kh, NA_KEY_COLS)
        out = jnp.einsum('bjqhm,bmhd->bjqhd', p[..., :N_META], vm) + jnp.einsum('bjqhik,bijkhd->bjqhd', p_win, vb)
        return out.reshape(B, GRID_W, H, D)

    real = lax.map(row_block, jnp.arange(rows))
    real = real.transpose(1, 0, 2, 3, 4).reshape(B, T, H, D)
    return jnp.concatenate([meta_out, real], axis=1)


def mla_attention(cq, ckv, k_pe, cq_g, ckv_g, w_q_up, w_kv_up, q_g, k_g, cos, sin):
    B, L, _ = cq.shape
    q = (rmsnorm(cq, cq_g) @ w_q_up).reshape(B, L, MLA_HEADS, MLA_QK_DIM)
    kv = (rmsnorm(ckv, ckv_g) @ w_kv_up).reshape(B, L, MLA_HEADS, MLA_NOPE_DIM + MLA_V_DIM)
    k_nope, v = kv[..., :MLA_NOPE_DIM], kv[..., MLA_NOPE_DIM:]
    k = jnp.concatenate([k_nope, jnp.broadcast_to(k_pe[:, :, None, :], (B, L, MLA_HEADS, MLA_ROPE_DIM))], axis=-1)
    q = rmsnorm(q, q_g)
    k = rmsnorm(k, k_g)
    q = jnp.concatenate([q[..., :MLA_NOPE_DIM], apply_rope(q[..., MLA_NOPE_DIM:], cos, sin)], axis=-1)
    k = jnp.concatenate([k[..., :MLA_NOPE_DIM], apply_rope(k[..., MLA_NOPE_DIM:], cos, sin)], axis=-1)
    scale = MLA_QK_DIM ** -0.5

    def attend(qb):
        s = jnp.einsum('bqhd,bkhd->bhqk', qb, k).astype(jnp.float32) * scale
        p = jax.nn.softmax(s, axis=-1).astype(v.dtype)
        return jnp.einsum('bhqk,bkhd->bqhd', p, v)

    out_meta = attend(q[:, :N_META])
    nb = (L - N_META) // Q_BLOCK
    qr = q[:, N_META:].reshape(B, nb, Q_BLOCK, MLA_HEADS, MLA_QK_DIM).transpose(1, 0, 2, 3, 4)
    out_real = lax.map(attend, qr).transpose(1, 0, 2, 3, 4).reshape(B, L - N_META, MLA_HEADS, MLA_V_DIM)
    return jnp.concatenate([out_meta, out_real], axis=1)


def conv_glu(x, w_gate, w_up, conv_w, conv_b, w_down):
    L = x.shape[1]
    g = x @ w_gate
    u = x @ w_up
    pad = CONV_W // 2
    gp = jnp.pad(g, ((0, 0), (pad, pad), (0, 0)))
    gc = conv_b
    for i in range(CONV_W):
        gc = gc + conv_w[i] * gp[:, i:i + L]
    return (jax.nn.silu(gc) * u) @ w_down


def setup_inputs(seed: int = 0) -> dict:
    key = jax.random.key(seed)
    ks = jax.random.split(key, 24)
    f32 = jnp.float32

    def nrm(k, shape, s):
        return jax.random.normal(k, shape, f32) * s

    def gain(k, shape):
        return 1.0 + 0.05 * jax.random.normal(k, shape, f32)

    return {
        "x": nrm(ks[0], (BATCH, SEQ, D_MODEL), 1.0),
        "meta_tokens": nrm(ks[1], (N_META, D_MODEL), 1.0),
        "mix_norm_g": gain(ks[2], (DEPTH, D_MODEL)),
        "w_in": nrm(ks[3], (DEPTH, D_MODEL, IN_COLS), D_MODEL ** -0.5),
        "na_q_g": gain(ks[4], (DEPTH, NA_HEAD_DIM)),
        "na_k_g": gain(ks[5], (DEPTH, NA_HEAD_DIM)),
        "na_rpb": nrm(ks[6], (DEPTH, NA_HEADS, 2 * NA_WIN_ROWS - 1, 2 * NA_WIN_COLS - 1), 0.1),
        "na_meta_bias": nrm(ks[7], (DEPTH, NA_HEADS, N_META), 0.1),
        "mla_cq_g": gain(ks[8], (DEPTH, MLA_Q_RANK)),
        "mla_ckv_g": gain(ks[9], (DEPTH, MLA_KV_RANK)),
        "w_q_up": nrm(ks[10], (DEPTH, MLA_Q_RANK, MLA_HEADS * MLA_QK_DIM), MLA_Q_RANK ** -0.5),
        "w_kv_up": nrm(ks[11], (DEPTH, MLA_KV_RANK, MLA_HEADS * (MLA_NOPE_DIM + MLA_V_DIM)), MLA_KV_RANK ** -0.5),
        "mla_q_g": gain(ks[12], (DEPTH, MLA_QK_DIM)),
        "mla_k_g": gain(ks[13], (DEPTH, MLA_QK_DIM)),
        "na_out_g": gain(ks[14], (DEPTH, NA_WIDTH)),
        "mla_out_g": gain(ks[15], (DEPTH, MLA_WIDTH)),
        "w_out": nrm(ks[16], (DEPTH, D_MIX, D_MODEL), D_MIX ** -0.5),
        "ffn_norm_g": gain(ks[17], (DEPTH, D_MODEL)),
        "w_gate": nrm(ks[18], (DEPTH, D_MODEL, D_FF), D_MODEL ** -0.5),
        "w_up": nrm(ks[19], (DEPTH, D_MODEL, D_FF), D_MODEL ** -0.5),
        "conv_w": nrm(ks[20], (DEPTH, CONV_W, D_FF), CONV_W ** -0.5),
        "conv_b": nrm(ks[21], (DEPTH, D_FF), 0.01),
        "w_down": nrm(ks[22], (DEPTH, D_FF, D_MODEL), D_FF ** -0.5),
    }


def reference(x, meta_tokens, mix_norm_g, w_in, na_q_g, na_k_g, na_rpb, na_meta_bias, mla_cq_g, mla_ckv_g,
              w_q_up, w_kv_up, mla_q_g, mla_k_g, na_out_g, mla_out_g, w_out, ffn_norm_g, w_gate, w_up,
              conv_w, conv_b, w_down):
    B = x.shape[0]
    meta = jnp.broadcast_to(meta_tokens.astype(x.dtype)[None], (B, N_META, D_MODEL))
    h = jnp.concatenate([meta, x], axis=1)
    L = h.shape[1]
    cos, sin = rope_tables(L, x.dtype)
    for l in range(DEPTH):
        hn = rmsnorm(h, mix_norm_g[l])
        proj = hn @ w_in[l]
        q_a, k_a, v_a, cq, ckv, k_pe = jnp.split(proj, IN_SPLITS, axis=-1)
        q_a = rmsnorm(q_a.reshape(B, L, NA_HEADS, NA_HEAD_DIM), na_q_g[l])
        k_a = rmsnorm(k_a.reshape(B, L, NA_HEADS, NA_HEAD_DIM), na_k_g[l])
        v_a = v_a.reshape(B, L, NA_HEADS, NA_HEAD_DIM)
        out_a = neighbourhood_attention(q_a, k_a, v_a, na_rpb[l], na_meta_bias[l]).reshape(B, L, NA_WIDTH)
        out_b = mla_attention(cq, ckv, k_pe, mla_cq_g[l], mla_ckv_g[l], w_q_up[l], w_kv_up[l],
                              mla_q_g[l], mla_k_g[l], cos, sin).reshape(B, L, MLA_WIDTH)
        mix = jnp.concatenate([rmsnorm(out_a, na_out_g[l]), rmsnorm(out_b, mla_out_g[l])], axis=-1)
        h = h + mix @ w_out[l]
        h = h + conv_glu(rmsnorm(h, ffn_norm_g[l]), w_gate[l], w_up[l], conv_w[l], conv_b[l], w_down[l])
    return h[:, N_META:]
```

```python
import functools

import jax
import jax.numpy as jnp
import numpy as np
from jax import lax
from jax.experimental import pallas as pl
from jax.experimental.pallas import tpu as pltpu

F32 = jnp.float32
BF16 = jnp.bfloat16

LANES = 128
BF16_SUBLANES = 16
VMEM_LIMIT_BYTES = 56 * 1024 * 1024

D_MODEL = 2048
GRID_W = 64
N_META = 16
EPS = 1e-6
NEG = -1e30
LOG2E = 1.4426950408889634
HEADS = 8
HEAD_DIM = 128
NA_WIDTH = HEADS * HEAD_DIM
NA_WIN_ROWS = 8
NA_WIN_COLS = 16
Q_RANK = 512
KV_RANK = 256
ROPE_DIM = 64
QK_DIM = HEAD_DIM + ROPE_DIM
QK_PAD = 2 * LANES
ROPE_THETA = 10000.0
D_FF = 5632
META_ROWS = LANES

IN_TM = 512
IN_TN = 1024
UP_TM = 512
ATT_TQ = 512
NA_QROWS = 4
NA_TQ = NA_QROWS * GRID_W
NA_KROWS = 3 * NA_QROWS
OUT_TM = 256
FFN_TM = 512
FFN_TF = 512
HALO = BF16_SUBLANES


def _cparams(*sem):
    return pltpu.CompilerParams(dimension_semantics=sem, vmem_limit_bytes=VMEM_LIMIT_BYTES)


def _rms(v, n=None):
    if n is None:
        ms = jnp.mean(v * v, axis=-1, keepdims=True)
    else:
        ms = jnp.sum(v * v, axis=-1, keepdims=True) * (1.0 / n)
    return v * lax.rsqrt(ms + EPS)


def _dot(a, b):
    return jnp.dot(a, b, preferred_element_type=F32)


def _dot_nt(a, b):
    return lax.dot_general(a, b, (((1,), (1,)), ((), ())), preferred_element_type=F32)


def _inproj_body(x_ref, g_ref, w_ref, qg_ref, kg_ref, cqg_ref, ckvg_ref,
                 qa_ref, ka_ref, va_ref, cq_ref, ckv_ref, kpe_ref, xn_ref, *, q_scale):
    j = pl.program_id(1)

    @pl.when(j == 0)
    def _():
        xn_ref[...] = (_rms(x_ref[...]) * g_ref[...]).astype(BF16)

    y = _dot(xn_ref[...], w_ref[...])

    def head_norm(gain_ref, out_ref, post):
        for h in range(HEADS):
            sl = slice(h * HEAD_DIM, (h + 1) * HEAD_DIM)
            out_ref[:, sl] = (_rms(y[:, sl]) * gain_ref[...] * post).astype(BF16)

    @pl.when(j == 0)
    def _():
        head_norm(qg_ref, qa_ref, q_scale)

    @pl.when(j == 1)
    def _():
        head_norm(kg_ref, ka_ref, 1.0)

    @pl.when(j == 2)
    def _():
        va_ref[...] = y.astype(BF16)

    @pl.when(j == 3)
    def _():
        cq_ref[...] = (_rms(y[:, :Q_RANK]) * cqg_ref[...]).astype(BF16)
        ckv_ref[...] = (_rms(y[:, Q_RANK:Q_RANK + KV_RANK]) * ckvg_ref[...]).astype(BF16)
        kpe_ref[...] = y[:, Q_RANK + KV_RANK:Q_RANK + KV_RANK + LANES]


def _inproj(x, g, w4, qg, kg, cqg, ckvg, tm):
    rows = x.shape[0]
    row = lambda i, j: (i, 0)
    const = lambda i, j: (0, 0)
    outs = (
        jax.ShapeDtypeStruct((rows, NA_WIDTH), BF16),
        jax.ShapeDtypeStruct((rows, NA_WIDTH), BF16),
        jax.ShapeDtypeStruct((rows, NA_WIDTH), BF16),
        jax.ShapeDtypeStruct((rows, Q_RANK), BF16),
        jax.ShapeDtypeStruct((rows, KV_RANK), BF16),
        jax.ShapeDtypeStruct((rows, LANES), F32),
    )
    return pl.pallas_call(
        functools.partial(_inproj_body, q_scale=HEAD_DIM ** -0.5 * LOG2E),
        grid=(rows // tm, 4),
        in_specs=[
            pl.BlockSpec((tm, D_MODEL), row),
            pl.BlockSpec((1, D_MODEL), const),
            pl.BlockSpec((None, D_MODEL, IN_TN), lambda i, j: (j, 0, 0)),
            pl.BlockSpec((1, HEAD_DIM), const),
            pl.BlockSpec((1, HEAD_DIM), const),
            pl.BlockSpec((1, Q_RANK), const),
            pl.BlockSpec((1, KV_RANK), const),
        ],
        out_specs=[
            pl.BlockSpec((tm, NA_WIDTH), row),
            pl.BlockSpec((tm, NA_WIDTH), row),
            pl.BlockSpec((tm, NA_WIDTH), row),
            pl.BlockSpec((tm, Q_RANK), row),
            pl.BlockSpec((tm, KV_RANK), row),
            pl.BlockSpec((tm, LANES), row),
        ],
        out_shape=outs,
        scratch_shapes=[pltpu.VMEM((tm, D_MODEL), BF16)],
        compiler_params=_cparams("parallel", "arbitrary"),
        name="inproj",
    )(x, g, w4, qg, kg, cqg, ckvg)


def _mla_up_body(cq_ref, ckv_ref, kpe_ref, cos_ref, sin_ref, wq_ref, wkv_ref, qg_ref, kg_ref,
                 qt_ref, k_ref, vt_ref, *, q_scale):
    cq = cq_ref[...]
    ckv = ckv_ref[...]
    cosf = cos_ref[...]
    sinf = sin_ref[...]
    qg = qg_ref[...]
    kg = kg_ref[...]

    def rope(r):
        rot = pltpu.roll(r, 32, 1) + pltpu.roll(r, 96, 1)
        return r * cosf + rot * sinf

    kpe = kpe_ref[...]
    kpe_ss = jnp.sum(kpe * kpe, axis=-1, keepdims=True)
    kpe_rot = rope(kpe * kg[:, LANES:])

    for h in range(HEADS):
        q = _dot(cq, wq_ref[h])
        qn = _rms(q, QK_DIM) * qg * q_scale
        qn = jnp.concatenate([qn[:, :LANES], rope(qn[:, LANES:])], axis=-1)
        qt_ref[h] = qn.T.astype(BF16)
        kv = _dot(ckv, wkv_ref[h])
        kn = kv[:, :LANES]
        ms = (jnp.sum(kn * kn, axis=-1, keepdims=True) + kpe_ss) * (1.0 / QK_DIM)
        r = lax.rsqrt(ms + EPS)
        k_ref[h] = jnp.concatenate([kn * r * kg[:, :LANES], kpe_rot * r], axis=-1).astype(BF16)
        vt_ref[h] = kv[:, LANES:].T.astype(BF16)


def _mla_up(cq, ckv, kpe, cosf, sinf, wq, wkv, qg, kg, nb, tm):
    rows = cq.shape[0]
    nt = rows // nb // tm
    row = lambda b, i: (b * nt + i, 0)
    pos = lambda b, i: (i, 0)
    c2 = lambda b, i: (0, 0)
    c3 = lambda b, i: (0, 0, 0)
    outs = (
        jax.ShapeDtypeStruct((nb, HEADS, QK_PAD, nt * tm), BF16),
        jax.ShapeDtypeStruct((nb, HEADS, nt, tm, QK_PAD), BF16),
        jax.ShapeDtypeStruct((nb, HEADS, nt, HEAD_DIM, tm), BF16),
    )
    return pl.pallas_call(
        functools.partial(_mla_up_body, q_scale=QK_DIM ** -0.5 * LOG2E),
        grid=(nb, nt),
        in_specs=[
            pl.BlockSpec((tm, Q_RANK), row),
            pl.BlockSpec((tm, KV_RANK), row),
            pl.BlockSpec((tm, LANES), row),
            pl.BlockSpec((tm, LANES), pos),
            pl.BlockSpec((tm, LANES), pos),
            pl.BlockSpec((HEADS, Q_RANK, QK_PAD), c3),
            pl.BlockSpec((HEADS, KV_RANK, 2 * HEAD_DIM), c3),
            pl.BlockSpec((1, QK_PAD), c2),
            pl.BlockSpec((1, QK_PAD), c2),
        ],
        out_specs=[
            pl.BlockSpec((None, HEADS, QK_PAD, tm), lambda b, i: (b, 0, 0, i)),
            pl.BlockSpec((None, HEADS, None, tm, QK_PAD), lambda b, i: (b, 0, i, 0, 0)),
            pl.BlockSpec((None, HEADS, None, HEAD_DIM, tm), lambda b, i: (b, 0, i, 0, 0)),
        ],
        out_shape=outs,
        compiler_params=_cparams("parallel", "parallel"),
        name="mla_up",
    )(cq, ckv, kpe, cosf, sinf, wq, wkv, qg, kg)


def _mla_attn_body(qt_ref, k_ref, vt_ref, km_ref, vmt_ref, o_ref, acc_ref, *, nk):
    qt = qt_ref[...]

    s = _dot(km_ref[...], qt)
    key = lax.broadcasted_iota(jnp.int32, s.shape, 0)
    s = jnp.where(key < N_META, s, NEG)
    m0 = jnp.max(s, axis=0, keepdims=True)
    p = jnp.exp2(s - m0)
    l0 = jnp.sum(p, axis=0, keepdims=True)
    acc_ref[...] = _dot(vmt_ref[...], p.astype(BF16))

    def chunk(c, carry):
        m_prev, l_prev = carry
        s = _dot(k_ref[c], qt)
        m_new = jnp.maximum(m_prev, jnp.max(s, axis=0, keepdims=True))
        alpha = jnp.exp2(m_prev - m_new)
        p = jnp.exp2(s - m_new)
        l_new = alpha * l_prev + jnp.sum(p, axis=0, keepdims=True)
        acc_ref[...] = alpha * acc_ref[...] + _dot(vt_ref[c], p.astype(BF16))
        return m_new, l_new

    _, l = lax.fori_loop(0, nk, chunk, (m0, l0), unroll=2)
    o_ref[...] = (acc_ref[...] / l).T


def _mla_attn(qt, k, vt, km, vmt, tq, shared_q):
    nb, _, nk, tk, _ = k.shape
    nq = qt.shape[3] // tq
    qb = (lambda b: 0) if shared_q else (lambda b: b)
    return pl.pallas_call(
        functools.partial(_mla_attn_body, nk=nk),
        grid=(nb, HEADS, nq),
        in_specs=[
            pl.BlockSpec((None, None, QK_PAD, tq), lambda b, h, i: (qb(b), h, 0, i)),
            pl.BlockSpec((None, None, nk, tk, QK_PAD), lambda b, h, i: (b, h, 0, 0, 0)),
            pl.BlockSpec((None, None, nk, HEAD_DIM, tk), lambda b, h, i: (b, h, 0, 0, 0)),
            pl.BlockSpec((None, None, None, META_ROWS, QK_PAD), lambda b, h, i: (0, h, 0, 0, 0)),
            pl.BlockSpec((None, None, None, HEAD_DIM, META_ROWS), lambda b, h, i: (0, h, 0, 0, 0)),
        ],
        out_specs=pl.BlockSpec((None, tq, HEAD_DIM), lambda b, h, i: (b, i, h)),
        out_shape=jax.ShapeDtypeStruct((nb, nq * tq, NA_WIDTH), F32),
        scratch_shapes=[pltpu.VMEM((HEAD_DIM, tq), F32)],
        compiler_params=_cparams("parallel", "parallel", "arbitrary"),
        name="mla_attn",
    )(qt, k, vt, km, vmt)


def _na_bias_body(rpb_ref, o_ref, *, rows):
    c = pl.program_id(0)
    h = pl.program_id(1)
    qr = pl.program_id(2)
    n_dr = 2 * NA_WIN_ROWS - 1
    n_dc = 2 * NA_WIN_COLS - 1
    r0 = jnp.where(c == 0, 0, jnp.where(c == 1, 2 * NA_QROWS, rows - NA_QROWS))
    abs_qr = r0 + qr
    rs = jnp.clip(abs_qr - NA_WIN_ROWS // 2, 0, rows - NA_WIN_ROWS)

    shape = (GRID_W, LANES)
    qc = lax.broadcasted_iota(jnp.int32, shape, 0)
    lane = lax.broadcasted_iota(jnp.int32, shape, 1)
    left = lane < GRID_W
    kc = jnp.where(left, lane, lane - GRID_W)
    cs = jnp.clip(qc - NA_WIN_COLS // 2, 0, GRID_W - NA_WIN_COLS)
    col_ok = (kc >= cs) & (kc < cs + NA_WIN_COLS)
    dc = kc - qc + NA_WIN_COLS - 1

    for a in range(NA_KROWS // 2):
        kr0 = r0 - NA_QROWS + 2 * a
        ok0 = ((kr0 >= rs) & (kr0 < rs + NA_WIN_ROWS)).astype(jnp.int32)
        ok1 = ((kr0 + 1 >= rs) & (kr0 + 1 < rs + NA_WIN_ROWS)).astype(jnp.int32)
        dr0 = jnp.clip(kr0 - abs_qr + NA_WIN_ROWS - 1, 0, n_dr - 1)
        dr1 = jnp.clip(kr0 + 1 - abs_qr + NA_WIN_ROWS - 1, 0, n_dr - 1)

        def pick(j, acc):
            v = jnp.where(left, rpb_ref[h, dr0, j], rpb_ref[h, dr1, j])
            return jnp.where(dc == j, v, acc)

        tile = lax.fori_loop(0, n_dc, pick, jnp.zeros(shape, F32))
        ok = col_ok & (jnp.where(left, ok0, ok1) > 0)
        o_ref[:, a * LANES:(a + 1) * LANES] = jnp.where(ok, tile * LOG2E, NEG)


def _na_bias(rpb, rows):
    return pl.pallas_call(
        functools.partial(_na_bias_body, rows=rows),
        grid=(3, HEADS, NA_QROWS),
        in_specs=[pl.BlockSpec(memory_space=pltpu.SMEM)],
        out_specs=pl.BlockSpec((None, None, GRID_W, NA_KROWS * GRID_W), lambda c, h, q: (c, h, q, 0)),
        out_shape=jax.ShapeDtypeStruct((3, HEADS, NA_TQ, NA_KROWS * GRID_W), F32),
        compiler_params=_cparams("parallel", "parallel", "parallel"),
        name="na_bias",
    )(rpb)


def _na_body(q_ref, kp_ref, kc_ref, kn_ref, vp_ref, vc_ref, vn_ref, km_ref, vm_ref,
             bias_ref, mb_ref, g_ref, o_ref, out_ref):
    for h in range(HEADS):
        sl = slice(h * HEAD_DIM, (h + 1) * HEAD_DIM)
        q = q_ref[:, sl]
        s = jnp.concatenate(
            [_dot_nt(q, kp_ref[:, sl]), _dot_nt(q, kc_ref[:, sl]), _dot_nt(q, kn_ref[:, sl])],
            axis=-1) + bias_ref[h]
        sm = _dot_nt(q, km_ref[:, sl]) + mb_ref[h]
        m = jnp.maximum(jnp.max(s, axis=-1, keepdims=True), jnp.max(sm, axis=-1, keepdims=True))
        p = jnp.exp2(s - m)
        pm = jnp.exp2(sm - m)
        l = jnp.sum(pm, axis=-1, keepdims=True) + jnp.sum(p, axis=-1, keepdims=True)
        pb = p.astype(BF16)
        o = (_dot(pm.astype(BF16), vm_ref[:, sl])
             + _dot(pb[:, :NA_TQ], vp_ref[:, sl])
             + _dot(pb[:, NA_TQ:2 * NA_TQ], vc_ref[:, sl])
             + _dot(pb[:, 2 * NA_TQ:], vn_ref[:, sl]))
        out_ref[:, sl] = o / l
    o_ref[...] = (_rms(out_ref[...]) * g_ref[...]).astype(BF16)


def _na_attn(qa, ka, va, km, vm, bias, mb, g, nb):
    rows = qa.shape[0]
    n = rows // nb // NA_TQ
    cur = lambda b, i: (b * n + i, 0)
    prev = lambda b, i: (b * n + jnp.maximum(i - 1, 0), 0)
    nxt = lambda b, i: (b * n + jnp.minimum(i + 1, n - 1), 0)
    blk = lambda f: pl.BlockSpec((NA_TQ, NA_WIDTH), f)
    case = lambda b, i: (jnp.where(i == 0, 0, jnp.where(i == n - 1, 2, 1)), 0, 0, 0)
    return pl.pallas_call(
        _na_body,
        grid=(nb, n),
        in_specs=[
            blk(cur), blk(prev), blk(cur), blk(nxt), blk(prev), blk(cur), blk(nxt),
            pl.BlockSpec((META_ROWS, NA_WIDTH), lambda b, i: (0, 0)),
            pl.BlockSpec((META_ROWS, NA_WIDTH), lambda b, i: (0, 0)),
            pl.BlockSpec((None, HEADS, NA_TQ, NA_KROWS * GRID_W), case),
            pl.BlockSpec((HEADS, 1, META_ROWS), lambda b, i: (0, 0, 0)),
            pl.BlockSpec((1, NA_WIDTH), lambda b, i: (0, 0)),
        ],
        out_specs=blk(cur),
        out_shape=jax.ShapeDtypeStruct((rows, NA_WIDTH), BF16),
        scratch_shapes=[pltpu.VMEM((NA_TQ, NA_WIDTH), F32)],
        compiler_params=_cparams("parallel", "arbitrary"),
        name="na_attn",
    )(qa, ka, ka, ka, va, va, va, km, vm, bias, mb, g)


def _na_meta_body(q_ref, k_ref, v_ref, mb_ref, g_ref, o_ref, out_ref):
    for h in range(HEADS):
        sl = slice(h * HEAD_DIM, (h + 1) * HEAD_DIM)
        s = _dot_nt(q_ref[:, sl], k_ref[:, sl]) + mb_ref[h]
        m = jnp.max(s, axis=-1, keepdims=True)
        p = jnp.exp2(s - m)
        l = jnp.sum(p, axis=-1, keepdims=True)
        out_ref[:, sl] = _dot(p.astype(BF16), v_ref[:, sl]) / l
    o_ref[...] = (_rms(out_ref[...]) * g_ref[...]).astype(BF16)


def _na_meta(qm, km, vm, mb, g):
    full = lambda s: pl.BlockSpec(s, lambda i: (0,) * len(s))
    return pl.pallas_call(
        _na_meta_body,
        grid=(1,),
        in_specs=[full((META_ROWS, NA_WIDTH))] * 3 + [full((HEADS, 1, META_ROWS)), full((1, NA_WIDTH))],
        out_specs=full((META_ROWS, NA_WIDTH)),
        out_shape=jax.ShapeDtypeStruct((META_ROWS, NA_WIDTH), BF16),
        scratch_shapes=[pltpu.VMEM((META_ROWS, NA_WIDTH), F32)],
        compiler_params=_cparams("arbitrary"),
        name="na_meta",
    )(qm, km, vm, mb, g)


def _outproj_body(x_ref, a_ref, b_ref, bg_ref, w_ref, fg_ref, h_ref, xn_ref):
    bn = (_rms(b_ref[...]) * bg_ref[...]).astype(BF16)
    mix = jnp.concatenate([a_ref[...], bn], axis=-1)
    h = x_ref[...] + _dot(mix, w_ref[...])
    h_ref[...] = h
    xn_ref[...] = (_rms(h) * fg_ref[...]).astype(BF16)


def _outproj(x, a, b, bg, w, fg, tm, x_map, a_map, b_map, rows):
    c = lambda i: (0, 0)
    return pl.pallas_call(
        _outproj_body,
        grid=(rows // tm,),
        in_specs=[
            pl.BlockSpec((tm, D_MODEL), x_map),
            pl.BlockSpec((tm, NA_WIDTH), a_map),
            pl.BlockSpec((tm, NA_WIDTH), b_map),
            pl.BlockSpec((1, NA_WIDTH), c),
            pl.BlockSpec((2 * NA_WIDTH, D_MODEL), c),
            pl.BlockSpec((1, D_MODEL), c),
        ],
        out_specs=[pl.BlockSpec((tm, D_MODEL), lambda i: (i, 0)),
                   pl.BlockSpec((tm, D_MODEL), lambda i: (i, 0))],
        out_shape=(jax.ShapeDtypeStruct((rows, D_MODEL), F32),
                   jax.ShapeDtypeStruct((rows, D_MODEL), BF16)),
        compiler_params=_cparams("parallel"),
        name="outproj",
    )(x, a, b, bg, w, fg)


def _ffn_body(xm_ref, xp_ref, xx_ref, xmeta_ref, h_ref, wg_ref, wu_ref, wd_ref, cw_ref, cb_ref,
              o_ref, xe_ref, *, tm, tpb):
    i = pl.program_id(0)
    j = pl.program_id(1)

    @pl.when(j == 0)
    def _():
        first = (i % tpb) == 0
        last = (i % tpb) == tpb - 1
        xe_ref[0:HALO, :] = jnp.where(first, xmeta_ref[...], xp_ref[...])
        xe_ref[HALO:HALO + tm, :] = xm_ref[...]
        xe_ref[HALO + tm:2 * HALO + tm, :] = jnp.where(last, jnp.zeros_like(xx_ref[...]), xx_ref[...])
        o_ref[...] = h_ref[...]

    g = _dot(xe_ref[...], wg_ref[...])
    u = _dot(xe_ref[HALO:HALO + tm, :], wu_ref[...])
    cw = cw_ref[...]
    gc = (cb_ref[...] + cw[0:1] * g[HALO - 1:HALO - 1 + tm]
          + cw[1:2] * g[HALO:HALO + tm] + cw[2:3] * g[HALO + 1:HALO + 1 + tm])
    act = gc * (1.0 / (1.0 + jnp.exp(-gc))) * u
    o_ref[...] += _dot(act.astype(BF16), wd_ref[...])


def _ffn(xn, xn_meta, h1, wg, wu, wd, cw, cb, nb, tm, tf):
    rows = xn.shape[0]
    nt = rows // tm
    tpb = nt // nb
    hb = tm // HALO
    last_hb = rows // HALO - 1
    return pl.pallas_call(
        functools.partial(_ffn_body, tm=tm, tpb=tpb),
        grid=(nt, D_FF // tf),
        in_specs=[
            pl.BlockSpec((tm, D_MODEL), lambda i, j: (i, 0)),
            pl.BlockSpec((HALO, D_MODEL), lambda i, j: (jnp.maximum(i * hb - 1, 0), 0)),
            pl.BlockSpec((HALO, D_MODEL), lambda i, j: (jnp.minimum((i + 1) * hb, last_hb), 0)),
            pl.BlockSpec((HALO, D_MODEL), lambda i, j: ((i // tpb) * (META_ROWS // HALO), 0)),
            pl.BlockSpec((tm, D_MODEL), lambda i, j: (i, 0)),
            pl.BlockSpec((D_MODEL, tf), lambda i, j: (0, j)),
            pl.BlockSpec((D_MODEL, tf), lambda i, j: (0, j)),
            pl.BlockSpec((tf, D_MODEL), lambda i, j: (j, 0)),
            pl.BlockSpec((3, tf), lambda i, j: (0, j)),
            pl.BlockSpec((1, tf), lambda i, j: (0, j)),
        ],
        out_specs=pl.BlockSpec((tm, D_MODEL), lambda i, j: (i, 0)),
        out_shape=jax.ShapeDtypeStruct((rows, D_MODEL), F32),
        scratch_shapes=[pltpu.VMEM((tm + 2 * HALO, D_MODEL), BF16)],
        compiler_params=_cparams("parallel", "arbitrary"),
        name="ffn",
    )(xn, xn, xn, xn_meta, h1, wg, wu, wd, cw, cb)


def _rope_tables(pos):
    inv = ROPE_THETA ** (-jnp.arange(0, ROPE_DIM, 2, dtype=F32) / ROPE_DIM)
    ang = pos.astype(F32)[:, None] * inv[None, :]
    cos, sin = jnp.cos(ang), jnp.sin(ang)
    z = jnp.zeros_like(cos)
    return (jnp.concatenate([cos, cos, z, z], axis=-1),
            jnp.concatenate([-sin, sin, z, z], axis=-1))


def _pad_lanes(v, width):
    return jnp.pad(v, ((0, 0), (0, width - v.shape[-1])))


def kernel(x, meta_tokens, mix_norm_g, w_in, na_q_g, na_k_g, na_rpb, na_meta_bias, mla_cq_g, mla_ckv_g,
           w_q_up, w_kv_up, mla_q_g, mla_k_g, na_out_g, mla_out_g, w_out, ffn_norm_g, w_gate, w_up,
           conv_w, conv_b, w_down):
    nb, seq, d = x.shape
    assert d == D_MODEL and w_in.shape[0] == 1, "one layer of width 2048"
    rows_grid = seq // GRID_W
    assert seq % max(IN_TM, UP_TM, ATT_TQ, FFN_TM) == 0 and rows_grid >= 4 * NA_QROWS

    w_in_p = _pad_lanes(w_in[0], 4 * IN_TN).astype(BF16)
    w4 = w_in_p.reshape(D_MODEL, 4, IN_TN).transpose(1, 0, 2)
    wq = w_q_up[0].reshape(Q_RANK, HEADS, QK_DIM).transpose(1, 0, 2)
    wq = jnp.pad(wq, ((0, 0), (0, 0), (0, QK_PAD - QK_DIM))).astype(BF16)
    wkv = w_kv_up[0].reshape(KV_RANK, HEADS, 2 * HEAD_DIM).transpose(1, 0, 2).astype(BF16)
    wo = w_out[0].astype(BF16)
    wg, wu, wd = w_gate[0].astype(BF16), w_up[0].astype(BF16), w_down[0].astype(BF16)
    row = lambda v: v.reshape(1, -1)
    qg_pad, kg_pad = _pad_lanes(mla_q_g, QK_PAD), _pad_lanes(mla_k_g, QK_PAD)
    mb = jnp.pad(na_meta_bias[0] * LOG2E, ((0, 0), (0, META_ROWS - N_META)),
                 constant_values=NEG).reshape(HEADS, 1, META_ROWS)

    xr = x.reshape(nb * seq, D_MODEL)
    xm = jnp.pad(meta_tokens.astype(x.dtype), ((0, META_ROWS - N_META), (0, 0)))
    cos_r, sin_r = _rope_tables(N_META + jnp.arange(seq))
    cos_m, sin_m = _rope_tables(jnp.arange(META_ROWS))

    inproj = functools.partial(_inproj, g=row(mix_norm_g), w4=w4, qg=row(na_q_g), kg=row(na_k_g),
                               cqg=row(mla_cq_g), ckvg=row(mla_ckv_g))
    qa, ka, va, cq, ckv, kpe = inproj(xr, tm=IN_TM)
    qa_m, ka_m, va_m, cq_m, ckv_m, kpe_m = inproj(xm, tm=META_ROWS)

    up = functools.partial(_mla_up, wq=wq, wkv=wkv, qg=qg_pad, kg=kg_pad)
    qt, kk, vt = up(cq, ckv, kpe, cos_r, sin_r, nb=nb, tm=UP_TM)
    qt_m, kk_m, vt_m = up(cq_m, ckv_m, kpe_m, cos_m, sin_m, nb=1, tm=META_ROWS)

    bias = _na_bias(na_rpb[0] , rows_grid)
    a_n = _na_attn(qa, ka, va, ka_m, va_m, bias, mb, row(na_out_g), nb)
    a_n_m = _na_meta(qa_m, ka_m, va_m, mb, row(na_out_g))

    out_b = _mla_attn(qt, kk, vt, kk_m, vt_m, ATT_TQ, shared_q=False)
    out_b_m = _mla_attn(qt_m, kk, vt, kk_m, vt_m, META_ROWS, shared_q=True)

    op = functools.partial(_outproj, bg=row(mla_out_g), w=wo, fg=row(ffn_norm_g))
    ident = lambda i: (i, 0)
    zero = lambda i: (0, 0)
    h1, xn2 = op(xr, a_n, out_b.reshape(nb * seq, NA_WIDTH), tm=OUT_TM,
                 x_map=ident, a_map=ident, b_map=ident, rows=nb * seq)
    _, xn2_m = op(xm, a_n_m, out_b_m.reshape(nb * META_ROWS, NA_WIDTH), tm=META_ROWS,
                  x_map=zero, a_map=zero, b_map=ident, rows=nb * META_ROWS)

    out = _ffn(xn2, xn2_m, h1, wg, wu, wd, conv_w[0], row(conv_b), nb, FFN_TM, FFN_TF)
    return out.reshape(nb, seq, D_MODEL)
```

```python
import functools

import jax
import jax.numpy as jnp
import numpy as np
from jax import lax
from jax.experimental import pallas as pl
from jax.experimental.pallas import tpu as pltpu

F32 = jnp.float32
BF16 = jnp.bfloat16

LANES = 128
BF16_SUBLANES = 16
VMEM_LIMIT_BYTES = 56 * 1024 * 1024

D_MODEL = 2048
GRID_W = 64
N_META = 16
EPS = 1e-6
NEG = -1e30
LOG2E = 1.4426950408889634
HEADS = 8
HEAD_DIM = 128
NA_WIDTH = HEADS * HEAD_DIM
NA_WIN_ROWS = 8
NA_WIN_COLS = 16
Q_RANK = 512
KV_RANK = 256
ROPE_DIM = 64
QK_DIM = HEAD_DIM + ROPE_DIM
QK_PAD = 2 * LANES
ROPE_THETA = 10000.0
D_FF = 5632
META_ROWS = LANES

IN_TM = 512
IN_TN = 1024
UP_TM = 512
ATT_TQ = 512
ATT_SUB = 256
NA_QROWS = 4
NA_TQ = NA_QROWS * GRID_W
NA_KROWS = 3 * NA_QROWS
OUT_TM = 256
FFN_TM = 512
FFN_TF = 512
HALO = BF16_SUBLANES


def _cparams(*sem):
    return pltpu.CompilerParams(dimension_semantics=sem, vmem_limit_bytes=VMEM_LIMIT_BYTES)


def _rms(v, n=None):
    if n is None:
        ms = jnp.mean(v * v, axis=-1, keepdims=True)
    else:
        ms = jnp.sum(v * v, axis=-1, keepdims=True) * (1.0 / n)
    return v * lax.rsqrt(ms + EPS)


def _dot(a, b):
    return jnp.dot(a, b, preferred_element_type=F32)


def _dot_nt(a, b):
    return lax.dot_general(a, b, (((1,), (1,)), ((), ())), preferred_element_type=F32)


def _inproj_body(x_ref, g_ref, w_ref, qg_ref, kg_ref, cqg_ref, ckvg_ref,
                 qa_ref, ka_ref, va_ref, cq_ref, ckv_ref, kpe_ref, xn_ref, *, q_scale):
    j = pl.program_id(1)

    @pl.when(j == 0)
    def _():
        xn_ref[...] = (_rms(x_ref[...]) * g_ref[...]).astype(BF16)

    y = _dot(xn_ref[...], w_ref[...])

    def head_norm(gain_ref, out_ref, post):
        for h in range(HEADS):
            sl = slice(h * HEAD_DIM, (h + 1) * HEAD_DIM)
            out_ref[:, sl] = (_rms(y[:, sl]) * gain_ref[...] * post).astype(BF16)

    @pl.when(j == 0)
    def _():
        head_norm(qg_ref, qa_ref, q_scale)

    @pl.when(j == 1)
    def _():
        head_norm(kg_ref, ka_ref, 1.0)

    @pl.when(j == 2)
    def _():
        va_ref[...] = y.astype(BF16)

    @pl.when(j == 3)
    def _():
        cq_ref[...] = (_rms(y[:, :Q_RANK]) * cqg_ref[...]).astype(BF16)
        ckv_ref[...] = (_rms(y[:, Q_RANK:Q_RANK + KV_RANK]) * ckvg_ref[...]).astype(BF16)
        kpe_ref[...] = y[:, Q_RANK + KV_RANK:Q_RANK + KV_RANK + LANES]


def _inproj(x, g, w4, qg, kg, cqg, ckvg, tm):
    rows = x.shape[0]
    row = lambda i, j: (i, 0)
    const = lambda i, j: (0, 0)
    outs = (
        jax.ShapeDtypeStruct((rows, NA_WIDTH), BF16),
        jax.ShapeDtypeStruct((rows, NA_WIDTH), BF16),
        jax.ShapeDtypeStruct((rows, NA_WIDTH), BF16),
        jax.ShapeDtypeStruct((rows, Q_RANK), BF16),
        jax.ShapeDtypeStruct((rows, KV_RANK), BF16),
        jax.ShapeDtypeStruct((rows, LANES), F32),
    )
    return pl.pallas_call(
        functools.partial(_inproj_body, q_scale=HEAD_DIM ** -0.5 * LOG2E),
        grid=(rows // tm, 4),
        in_specs=[
            pl.BlockSpec((tm, D_MODEL), row),
            pl.BlockSpec((1, D_MODEL), const),
            pl.BlockSpec((None, D_MODEL, IN_TN), lambda i, j: (j, 0, 0)),
            pl.BlockSpec((1, HEAD_DIM), const),
            pl.BlockSpec((1, HEAD_DIM), const),
            pl.BlockSpec((1, Q_RANK), const),
            pl.BlockSpec((1, KV_RANK), const),
        ],
        out_specs=[
            pl.BlockSpec((tm, NA_WIDTH), row),
            pl.BlockSpec((tm, NA_WIDTH), row),
            pl.BlockSpec((tm, NA_WIDTH), row),
            pl.BlockSpec((tm, Q_RANK), row),
            pl.BlockSpec((tm, KV_RANK), row),
            pl.BlockSpec((tm, LANES), row),
        ],
        out_shape=outs,
        scratch_shapes=[pltpu.VMEM((tm, D_MODEL), BF16)],
        compiler_params=_cparams("parallel", "arbitrary"),
        name="inproj",
    )(x, g, w4, qg, kg, cqg, ckvg)


def _mla_up_body(cq_ref, ckv_ref, kpe_ref, cos_ref, sin_ref, wq_ref, wkv_ref, qg_ref, kg_ref,
                 qt_ref, k_ref, vt_ref, *, q_scale):
    cq = cq_ref[...]
    ckv = ckv_ref[...]
    cosf = cos_ref[...]
    sinf = sin_ref[...]
    qg = qg_ref[...]
    kg = kg_ref[...]

    def rope(r):
        rot = pltpu.roll(r, 32, 1) + pltpu.roll(r, 96, 1)
        return r * cosf + rot * sinf

    kpe = kpe_ref[...]
    kpe_ss = jnp.sum(kpe * kpe, axis=-1, keepdims=True)
    kpe_rot = rope(kpe * kg[:, LANES:])

    for h in range(HEADS):
        q = _dot(cq, wq_ref[h])
        qn = _rms(q, QK_DIM) * qg * q_scale
        qn = jnp.concatenate([qn[:, :LANES], rope(qn[:, LANES:])], axis=-1)
        qt_ref[h] = qn.T.astype(BF16)
        kv = _dot(ckv, wkv_ref[h])
        kn = kv[:, :LANES]
        ms = (jnp.sum(kn * kn, axis=-1, keepdims=True) + kpe_ss) * (1.0 / QK_DIM)
        r = lax.rsqrt(ms + EPS)
        k_ref[h] = jnp.concatenate([kn * r * kg[:, :LANES], kpe_rot * r], axis=-1).astype(BF16)
        vt_ref[h] = kv[:, LANES:].T.astype(BF16)


def _mla_up(cq, ckv, kpe, cosf, sinf, wq, wkv, qg, kg, nb, tm):
    rows = cq.shape[0]
    nt = rows // nb // tm
    row = lambda b, i: (b * nt + i, 0)
    pos = lambda b, i: (i, 0)
    c2 = lambda b, i: (0, 0)
    c3 = lambda b, i: (0, 0, 0)
    outs = (
        jax.ShapeDtypeStruct((nb, HEADS, QK_PAD, nt * tm), BF16),
        jax.ShapeDtypeStruct((nb, HEADS, nt, tm, QK_PAD), BF16),
        jax.ShapeDtypeStruct((nb, HEADS, nt, HEAD_DIM, tm), BF16),
    )
    return pl.pallas_call(
        functools.partial(_mla_up_body, q_scale=QK_DIM ** -0.5 * LOG2E),
        grid=(nb, nt),
        in_specs=[
            pl.BlockSpec((tm, Q_RANK), row),
            pl.BlockSpec((tm, KV_RANK), row),
            pl.BlockSpec((tm, LANES), row),
            pl.BlockSpec((tm, LANES), pos),
            pl.BlockSpec((tm, LANES), pos),
            pl.BlockSpec((HEADS, Q_RANK, QK_PAD), c3),
            pl.BlockSpec((HEADS, KV_RANK, 2 * HEAD_DIM), c3),
            pl.BlockSpec((1, QK_PAD), c2),
            pl.BlockSpec((1, QK_PAD), c2),
        ],
        out_specs=[
            pl.BlockSpec((None, HEADS, QK_PAD, tm), lambda b, i: (b, 0, 0, i)),
            pl.BlockSpec((None, HEADS, None, tm, QK_PAD), lambda b, i: (b, 0, i, 0, 0)),
            pl.BlockSpec((None, HEADS, None, HEAD_DIM, tm), lambda b, i: (b, 0, i, 0, 0)),
        ],
        out_shape=outs,
        compiler_params=_cparams("parallel", "parallel"),
        name="mla_up",
    )(cq, ckv, kpe, cosf, sinf, wq, wkv, qg, kg)


def _mla_attn_body(qt_ref, k_ref, vt_ref, km_ref, vmt_ref, o_ref, acc_ref, s_ref, *, nk, tk):
    qt = qt_ref[...]

    s = _dot(km_ref[...], qt)
    key = lax.broadcasted_iota(jnp.int32, s.shape, 0)
    s = jnp.where(key < N_META, s, NEG)
    m0 = jnp.max(s, axis=0, keepdims=True)
    p = jnp.exp2(s - m0)
    l0 = jnp.sum(p, axis=0, keepdims=True)
    acc_ref[...] = _dot(vmt_ref[...], p.astype(BF16))

    def scores(c, slot):
        cm = None
        for j in range(tk // ATT_SUB):
            rows = pl.ds(j * ATT_SUB, ATT_SUB)
            s = _dot(k_ref[c, rows, :], qt)
            s_ref[slot, rows, :] = s
            m = jnp.max(s, axis=0, keepdims=True)
            cm = m if cm is None else jnp.maximum(cm, m)
        return cm

    def attend(c, slot, cm, m_prev, l_prev):
        m_new = jnp.maximum(m_prev, cm)
        alpha = jnp.exp2(m_prev - m_new)
        p = jnp.exp2(s_ref[slot] - m_new)
        l_new = alpha * l_prev + jnp.sum(p, axis=0, keepdims=True)
        acc_ref[...] = alpha * acc_ref[...] + _dot(vt_ref[c], p.astype(BF16))
        return m_new, l_new

    def pair(i, carry):
        m, l, cm0 = carry
        cm1 = scores(2 * i + 1, 1)
        m, l = attend(2 * i, 0, cm0, m, l)
        cm0 = scores(2 * i + 2, 0)
        m, l = attend(2 * i + 1, 1, cm1, m, l)
        return m, l, cm0

    m, l, cm0 = lax.fori_loop(0, nk // 2 - 1, pair, (m0, l0, scores(0, 0)))
    cm1 = scores(nk - 1, 1)
    m, l = attend(nk - 2, 0, cm0, m, l)
    m, l = attend(nk - 1, 1, cm1, m, l)
    o_ref[...] = (acc_ref[...] / l).T


def _mla_attn(qt, k, vt, km, vmt, tq, shared_q):
    nb, _, nk, tk, _ = k.shape
    nq = qt.shape[3] // tq
    qb = (lambda b: 0) if shared_q else (lambda b: b)
    return pl.pallas_call(
        functools.partial(_mla_attn_body, nk=nk, tk=tk),
        grid=(nb, HEADS, nq),
        in_specs=[
            pl.BlockSpec((None, None, QK_PAD, tq), lambda b, h, i: (qb(b), h, 0, i)),
            pl.BlockSpec((None, None, nk, tk, QK_PAD), lambda b, h, i: (b, h, 0, 0, 0)),
            pl.BlockSpec((None, None, nk, HEAD_DIM, tk), lambda b, h, i: (b, h, 0, 0, 0)),
            pl.BlockSpec((None, None, None, META_ROWS, QK_PAD), lambda b, h, i: (0, h, 0, 0, 0)),
            pl.BlockSpec((None, None, None, HEAD_DIM, META_ROWS), lambda b, h, i: (0, h, 0, 0, 0)),
        ],
        out_specs=pl.BlockSpec((None, tq, HEAD_DIM), lambda b, h, i: (b, i, h)),
        out_shape=jax.ShapeDtypeStruct((nb, nq * tq, NA_WIDTH), F32),
        scratch_shapes=[pltpu.VMEM((HEAD_DIM, tq), F32), pltpu.VMEM((2, tk, tq), F32)],
        compiler_params=_cparams("parallel", "parallel", "arbitrary"),
        name="mla_attn",
    )(qt, k, vt, km, vmt)


def _na_bias_body(rpb_ref, o_ref, *, rows):
    c = pl.program_id(0)
    h = pl.program_id(1)
    qr = pl.program_id(2)
    n_dr = 2 * NA_WIN_ROWS - 1
    n_dc = 2 * NA_WIN_COLS - 1
    r0 = jnp.where(c == 0, 0, jnp.where(c == 1, 2 * NA_QROWS, rows - NA_QROWS))
    abs_qr = r0 + qr
    rs = jnp.clip(abs_qr - NA_WIN_ROWS // 2, 0, rows - NA_WIN_ROWS)

    shape = (GRID_W, LANES)
    qc = lax.broadcasted_iota(jnp.int32, shape, 0)
    lane = lax.broadcasted_iota(jnp.int32, shape, 1)
    left = lane < GRID_W
    kc = jnp.where(left, lane, lane - GRID_W)
    cs = jnp.clip(qc - NA_WIN_COLS // 2, 0, GRID_W - NA_WIN_COLS)
    col_ok = (kc >= cs) & (kc < cs + NA_WIN_COLS)
    dc = kc - qc + NA_WIN_COLS - 1

    for a in range(NA_KROWS // 2):
        kr0 = r0 - NA_QROWS + 2 * a
        ok0 = ((kr0 >= rs) & (kr0 < rs + NA_WIN_ROWS)).astype(jnp.int32)
        ok1 = ((kr0 + 1 >= rs) & (kr0 + 1 < rs + NA_WIN_ROWS)).astype(jnp.int32)
        dr0 = jnp.clip(kr0 - abs_qr + NA_WIN_ROWS - 1, 0, n_dr - 1)
        dr1 = jnp.clip(kr0 + 1 - abs_qr + NA_WIN_ROWS - 1, 0, n_dr - 1)

        def pick(j, acc):
            v = jnp.where(left, rpb_ref[h, dr0, j], rpb_ref[h, dr1, j])
            return jnp.where(dc == j, v, acc)

        tile = lax.fori_loop(0, n_dc, pick, jnp.zeros(shape, F32))
        ok = col_ok & (jnp.where(left, ok0, ok1) > 0)
        o_ref[:, a * LANES:(a + 1) * LANES] = jnp.where(ok, tile * LOG2E, NEG)


def _na_bias(rpb, rows):
    return pl.pallas_call(
        functools.partial(_na_bias_body, rows=rows),
        grid=(3, HEADS, NA_QROWS),
        in_specs=[pl.BlockSpec(memory_space=pltpu.SMEM)],
        out_specs=pl.BlockSpec((None, None, GRID_W, NA_KROWS * GRID_W), lambda c, h, q: (c, h, q, 0)),
        out_shape=jax.ShapeDtypeStruct((3, HEADS, NA_TQ, NA_KROWS * GRID_W), F32),
        compiler_params=_cparams("parallel", "parallel", "parallel"),
        name="na_bias",
    )(rpb)


def _na_body(q_ref, kp_ref, kc_ref, kn_ref, vp_ref, vc_ref, vn_ref, km_ref, vm_ref,
             bias_ref, mb_ref, g_ref, o_ref, out_ref):
    for h in range(HEADS):
        sl = slice(h * HEAD_DIM, (h + 1) * HEAD_DIM)
        q = q_ref[:, sl]
        s = jnp.concatenate(
            [_dot_nt(q, kp_ref[:, sl]), _dot_nt(q, kc_ref[:, sl]), _dot_nt(q, kn_ref[:, sl])],
            axis=-1) + bias_ref[h]
        sm = _dot_nt(q, km_ref[:, sl]) + mb_ref[h]
        m = jnp.maximum(jnp.max(s, axis=-1, keepdims=True), jnp.max(sm, axis=-1, keepdims=True))
        p = jnp.exp2(s - m)
        pm = jnp.exp2(sm - m)
        l = jnp.sum(pm, axis=-1, keepdims=True) + jnp.sum(p, axis=-1, keepdims=True)
        pb = p.astype(BF16)
        o = (_dot(pm.astype(BF16), vm_ref[:, sl])
             + _dot(pb[:, :NA_TQ], vp_ref[:, sl])
             + _dot(pb[:, NA_TQ:2 * NA_TQ], vc_ref[:, sl])
             + _dot(pb[:, 2 * NA_TQ:], vn_ref[:, sl]))
        out_ref[:, sl] = o / l
    o_ref[...] = (_rms(out_ref[...]) * g_ref[...]).astype(BF16)


def _na_attn(qa, ka, va, km, vm, bias, mb, g, nb):
    rows = qa.shape[0]
    n = rows // nb // NA_TQ
    cur = lambda b, i: (b * n + i, 0)
    prev = lambda b, i: (b * n + jnp.maximum(i - 1, 0), 0)
    nxt = lambda b, i: (b * n + jnp.minimum(i + 1, n - 1), 0)
    blk = lambda f: pl.BlockSpec((NA_TQ, NA_WIDTH), f)
    case = lambda b, i: (jnp.where(i == 0, 0, jnp.where(i == n - 1, 2, 1)), 0, 0, 0)
    return pl.pallas_call(
        _na_body,
        grid=(nb, n),
        in_specs=[
            blk(cur), blk(prev), blk(cur), blk(nxt), blk(prev), blk(cur), blk(nxt),
            pl.BlockSpec((META_ROWS, NA_WIDTH), lambda b, i: (0, 0)),
            pl.BlockSpec((META_ROWS, NA_WIDTH), lambda b, i: (0, 0)),
            pl.BlockSpec((None, HEADS, NA_TQ, NA_KROWS * GRID_W), case),
            pl.BlockSpec((HEADS, 1, META_ROWS), lambda b, i: (0, 0, 0)),
            pl.BlockSpec((1, NA_WIDTH), lambda b, i: (0, 0)),
        ],
        out_specs=blk(cur),
        out_shape=jax.ShapeDtypeStruct((rows, NA_WIDTH), BF16),
        scratch_shapes=[pltpu.VMEM((NA_TQ, NA_WIDTH), F32)],
        compiler_params=_cparams("parallel", "arbitrary"),
        name="na_attn",
    )(qa, ka, ka, ka, va, va, va, km, vm, bias, mb, g)


def _na_meta_body(q_ref, k_ref, v_ref, mb_ref, g_ref, o_ref, out_ref):
    for h in range(HEADS):
        sl = slice(h * HEAD_DIM, (h + 1) * HEAD_DIM)
        s = _dot_nt(q_ref[:, sl], k_ref[:, sl]) + mb_ref[h]
        m = jnp.max(s, axis=-1, keepdims=True)
        p = jnp.exp2(s - m)
        l = jnp.sum(p, axis=-1, keepdims=True)
        out_ref[:, sl] = _dot(p.astype(BF16), v_ref[:, sl]) / l
    o_ref[...] = (_rms(out_ref[...]) * g_ref[...]).astype(BF16)


def _na_meta(qm, km, vm, mb, g):
    full = lambda s: pl.BlockSpec(s, lambda i: (0,) * len(s))
    return pl.pallas_call(
        _na_meta_body,
        grid=(1,),
        in_specs=[full((META_ROWS, NA_WIDTH))] * 3 + [full((HEADS, 1, META_ROWS)), full((1, NA_WIDTH))],
        out_specs=full((META_ROWS, NA_WIDTH)),
        out_shape=jax.ShapeDtypeStruct((META_ROWS, NA_WIDTH), BF16),
        scratch_shapes=[pltpu.VMEM((META_ROWS, NA_WIDTH), F32)],
        compiler_params=_cparams("arbitrary"),
        name="na_meta",
    )(qm, km, vm, mb, g)


def _outproj_body(x_ref, a_ref, b_ref, bg_ref, w_ref, fg_ref, h_ref, xn_ref):
    bn = (_rms(b_ref[...]) * bg_ref[...]).astype(BF16)
    mix = jnp.concatenate([a_ref[...], bn], axis=-1)
    h = x_ref[...] + _dot(mix, w_ref[...])
    h_ref[...] = h
    xn_ref[...] = (_rms(h) * fg_ref[...]).astype(BF16)


def _outproj(x, a, b, bg, w, fg, tm, x_map, a_map, b_map, rows):
    c = lambda i: (0, 0)
    return pl.pallas_call(
        _outproj_body,
        grid=(rows // tm,),
        in_specs=[
            pl.BlockSpec((tm, D_MODEL), x_map),
            pl.BlockSpec((tm, NA_WIDTH), a_map),
            pl.BlockSpec((tm, NA_WIDTH), b_map),
            pl.BlockSpec((1, NA_WIDTH), c),
            pl.BlockSpec((2 * NA_WIDTH, D_MODEL), c),
            pl.BlockSpec((1, D_MODEL), c),
        ],
        out_specs=[pl.BlockSpec((tm, D_MODEL), lambda i: (i, 0)),
                   pl.BlockSpec((tm, D_MODEL), lambda i: (i, 0))],
        out_shape=(jax.ShapeDtypeStruct((rows, D_MODEL), F32),
                   jax.ShapeDtypeStruct((rows, D_MODEL), BF16)),
        compiler_params=_cparams("parallel"),
        name="outproj",
    )(x, a, b, bg, w, fg)


def _ffn_body(xm_ref, xp_ref, xx_ref, xmeta_ref, h_ref, wg_ref, wu_ref, wd_ref, cw_ref, cb_ref,
              o_ref, xe_ref, *, tm, tpb):
    i = pl.program_id(0)
    j = pl.program_id(1)

    @pl.when(j == 0)
    def _():
        first = (i % tpb) == 0
        last = (i % tpb) == tpb - 1
        xe_ref[0:HALO, :] = jnp.where(first, xmeta_ref[...], xp_ref[...])
        xe_ref[HALO:HALO + tm, :] = xm_ref[...]
        xe_ref[HALO + tm:2 * HALO + tm, :] = jnp.where(last, jnp.zeros_like(xx_ref[...]), xx_ref[...])
        o_ref[...] = h_ref[...]

    g = _dot(xe_ref[...], wg_ref[...])
    u = _dot(xe_ref[HALO:HALO + tm, :], wu_ref[...])
    cw = cw_ref[...]
    gc = (cb_ref[...] + cw[0:1] * g[HALO - 1:HALO - 1 + tm]
          + cw[1:2] * g[HALO:HALO + tm] + cw[2:3] * g[HALO + 1:HALO + 1 + tm])
    act = gc * (1.0 / (1.0 + jnp.exp(-gc))) * u
    o_ref[...] += _dot(act.astype(BF16), wd_ref[...])


def _ffn(xn, xn_meta, h1, wg, wu, wd, cw, cb, nb, tm, tf):
    rows = xn.shape[0]
    nt = rows // tm
    tpb = nt // nb
    hb = tm // HALO
    last_hb = rows // HALO - 1
    return pl.pallas_call(
        functools.partial(_ffn_body, tm=tm, tpb=tpb),
        grid=(nt, D_FF // tf),
        in_specs=[
            pl.BlockSpec((tm, D_MODEL), lambda i, j: (i, 0)),
            pl.BlockSpec((HALO, D_MODEL), lambda i, j: (jnp.maximum(i * hb - 1, 0), 0)),
            pl.BlockSpec((HALO, D_MODEL), lambda i, j: (jnp.minimum((i + 1) * hb, last_hb), 0)),
            pl.BlockSpec((HALO, D_MODEL), lambda i, j: ((i // tpb) * (META_ROWS // HALO), 0)),
            pl.BlockSpec((tm, D_MODEL), lambda i, j: (i, 0)),
            pl.BlockSpec((D_MODEL, tf), lambda i, j: (0, j)),
            pl.BlockSpec((D_MODEL, tf), lambda i, j: (0, j)),
            pl.BlockSpec((tf, D_MODEL), lambda i, j: (j, 0)),
            pl.BlockSpec((3, tf), lambda i, j: (0, j)),
            pl.BlockSpec((1, tf), lambda i, j: (0, j)),
        ],
        out_specs=pl.BlockSpec((tm, D_MODEL), lambda i, j: (i, 0)),
        out_shape=jax.ShapeDtypeStruct((rows, D_MODEL), F32),
        scratch_shapes=[pltpu.VMEM((tm + 2 * HALO, D_MODEL), BF16)],
        compiler_params=_cparams("parallel", "arbitrary"),
        name="ffn",
    )(xn, xn, xn, xn_meta, h1, wg, wu, wd, cw, cb)


def _rope_tables(pos):
    inv = ROPE_THETA ** (-jnp.arange(0, ROPE_DIM, 2, dtype=F32) / ROPE_DIM)
    ang = pos.astype(F32)[:, None] * inv[None, :]
    cos, sin = jnp.cos(ang), jnp.sin(ang)
    z = jnp.zeros_like(cos)
    return (jnp.concatenate([cos, cos, z, z], axis=-1),
            jnp.concatenate([-sin, sin, z, z], axis=-1))


def _pad_lanes(v, width):
    return jnp.pad(v, ((0, 0), (0, width - v.shape[-1])))


def kernel(x, meta_tokens, mix_norm_g, w_in, na_q_g, na_k_g, na_rpb, na_meta_bias, mla_cq_g, mla_ckv_g,
           w_q_up, w_kv_up, mla_q_g, mla_k_g, na_out_g, mla_out_g, w_out, ffn_norm_g, w_gate, w_up,
           conv_w, conv_b, w_down):
    nb, seq, d = x.shape
    assert d == D_MODEL and w_in.shape[0] == 1, "one layer of width 2048"
    rows_grid = seq // GRID_W
    assert seq % max(IN_TM, UP_TM, ATT_TQ, FFN_TM) == 0 and rows_grid >= 4 * NA_QROWS

    w_in_p = _pad_lanes(w_in[0], 4 * IN_TN).astype(BF16)
    w4 = w_in_p.reshape(D_MODEL, 4, IN_TN).transpose(1, 0, 2)
    wq = w_q_up[0].reshape(Q_RANK, HEADS, QK_DIM).transpose(1, 0, 2)
    wq = jnp.pad(wq, ((0, 0), (0, 0), (0, QK_PAD - QK_DIM))).astype(BF16)
    wkv = w_kv_up[0].reshape(KV_RANK, HEADS, 2 * HEAD_DIM).transpose(1, 0, 2).astype(BF16)
    wo = w_out[0].astype(BF16)
    wg, wu, wd = w_gate[0].astype(BF16), w_up[0].astype(BF16), w_down[0].astype(BF16)
    row = lambda v: v.reshape(1, -1)
    qg_pad, kg_pad = _pad_lanes(mla_q_g, QK_PAD), _pad_lanes(mla_k_g, QK_PAD)
    mb = jnp.pad(na_meta_bias[0] * LOG2E, ((0, 0), (0, META_ROWS - N_META)),
                 constant_values=NEG).reshape(HEADS, 1, META_ROWS)

    xr = x.reshape(nb * seq, D_MODEL)
    xm = jnp.pad(meta_tokens.astype(x.dtype), ((0, META_ROWS - N_META), (0, 0)))
    cos_r, sin_r = _rope_tables(N_META + jnp.arange(seq))
    cos_m, sin_m = _rope_tables(jnp.arange(META_ROWS))

    inproj = functools.partial(_inproj, g=row(mix_norm_g), w4=w4, qg=row(na_q_g), kg=row(na_k_g),
                               cqg=row(mla_cq_g), ckvg=row(mla_ckv_g))
    qa, ka, va, cq, ckv, kpe = inproj(xr, tm=IN_TM)
    qa_m, ka_m, va_m, cq_m, ckv_m, kpe_m = inproj(xm, tm=META_ROWS)

    up = functools.partial(_mla_up, wq=wq, wkv=wkv, qg=qg_pad, kg=kg_pad)
    qt, kk, vt = up(cq, ckv, kpe, cos_r, sin_r, nb=nb, tm=UP_TM)
    qt_m, kk_m, vt_m = up(cq_m, ckv_m, kpe_m, cos_m, sin_m, nb=1, tm=META_ROWS)

    bias = _na_bias(na_rpb[0] , rows_grid)
    a_n = _na_attn(qa, ka, va, ka_m, va_m, bias, mb, row(na_out_g), nb)
    a_n_m = _na_meta(qa_m, ka_m, va_m, mb, row(na_out_g))

    out_b = _mla_attn(qt, kk, vt, kk_m, vt_m, ATT_TQ, shared_q=False)
    out_b_m = _mla_attn(qt_m, kk, vt, kk_m, vt_m, META_ROWS, shared_q=True)

    op = functools.partial(_outproj, bg=row(mla_out_g), w=wo, fg=row(ffn_norm_g))
    ident = lambda i: (i, 0)
    zero = lambda i: (0, 0)
    h1, xn2 = op(xr, a_n, out_b.reshape(nb * seq, NA_WIDTH), tm=OUT_TM,
                 x_map=ident, a_map=ident, b_map=ident, rows=nb * seq)
    _, xn2_m = op(xm, a_n_m, out_b_m.reshape(nb * META_ROWS, NA_WIDTH), tm=META_ROWS,
                  x_map=zero, a_map=zero, b_map=ident, rows=nb * META_ROWS)

    out = _ffn(xn2, xn2_m, h1, wg, wu, wd, conv_w[0], row(conv_b), nb, FFN_TM, FFN_TF)
    return out.reshape(nb, seq, D_MODEL)
```

```python
import functools

import jax
import jax.numpy as jnp
import numpy as np
from jax import lax
from jax.experimental import pallas as pl
from jax.experimental.pallas import tpu as pltpu

F32 = jnp.float32
BF16 = jnp.bfloat16

LANES = 128
BF16_SUBLANES = 16
VMEM_LIMIT_BYTES = 56 * 1024 * 1024

D_MODEL = 2048
GRID_W = 64
N_META = 16
EPS = 1e-6
NEG = -1e30
LOG2E = 1.4426950408889634
HEADS = 8
HEAD_DIM = 128
NA_WIDTH = HEADS * HEAD_DIM
NA_WIN_ROWS = 8
NA_WIN_COLS = 16
Q_RANK = 512
KV_RANK = 256
ROPE_DIM = 64
QK_DIM = HEAD_DIM + ROPE_DIM
QK_PAD = 2 * LANES
ROPE_THETA = 10000.0
D_FF = 5632
META_ROWS = LANES

IN_TM = 512
IN_TN = 1024
UP_TM = 512
ATT_TQ = 1024
ATT_SUB = 256
ATT_GROUP = 4
NA_QROWS = 4
NA_TQ = NA_QROWS * GRID_W
NA_KROWS = 3 * NA_QROWS
OUT_TM = 256
FFN_TM = 512
FFN_TF = 512
HALO = BF16_SUBLANES


def _cparams(*sem):
    return pltpu.CompilerParams(dimension_semantics=sem, vmem_limit_bytes=VMEM_LIMIT_BYTES)


def _rms(v, n=None):
    if n is None:
        ms = jnp.mean(v * v, axis=-1, keepdims=True)
    else:
        ms = jnp.sum(v * v, axis=-1, keepdims=True) * (1.0 / n)
    return v * lax.rsqrt(ms + EPS)


def _dot(a, b):
    return jnp.dot(a, b, preferred_element_type=F32)


def _dot_nt(a, b):
    return lax.dot_general(a, b, (((1,), (1,)), ((), ())), preferred_element_type=F32)


def _inproj_body(x_ref, g_ref, w_ref, qg_ref, kg_ref, cqg_ref, ckvg_ref,
                 qa_ref, ka_ref, va_ref, cq_ref, ckv_ref, kpe_ref, xn_ref, *, q_scale):
    j = pl.program_id(1)

    @pl.when(j == 0)
    def _():
        xn_ref[...] = (_rms(x_ref[...]) * g_ref[...]).astype(BF16)

    y = _dot(xn_ref[...], w_ref[...])

    def head_norm(gain_ref, out_ref, post):
        for h in range(HEADS):
            sl = slice(h * HEAD_DIM, (h + 1) * HEAD_DIM)
            out_ref[:, sl] = (_rms(y[:, sl]) * gain_ref[...] * post).astype(BF16)

    @pl.when(j == 0)
    def _():
        head_norm(qg_ref, qa_ref, q_scale)

    @pl.when(j == 1)
    def _():
        head_norm(kg_ref, ka_ref, 1.0)

    @pl.when(j == 2)
    def _():
        va_ref[...] = y.astype(BF16)

    @pl.when(j == 3)
    def _():
        cq_ref[...] = (_rms(y[:, :Q_RANK]) * cqg_ref[...]).astype(BF16)
        ckv_ref[...] = (_rms(y[:, Q_RANK:Q_RANK + KV_RANK]) * ckvg_ref[...]).astype(BF16)
        kpe_ref[...] = y[:, Q_RANK + KV_RANK:Q_RANK + KV_RANK + LANES]


def _inproj(x, g, w4, qg, kg, cqg, ckvg, tm):
    rows = x.shape[0]
    row = lambda i, j: (i, 0)
    const = lambda i, j: (0, 0)
    outs = (
        jax.ShapeDtypeStruct((rows, NA_WIDTH), BF16),
        jax.ShapeDtypeStruct((rows, NA_WIDTH), BF16),
        jax.ShapeDtypeStruct((rows, NA_WIDTH), BF16),
        jax.ShapeDtypeStruct((rows, Q_RANK), BF16),
        jax.ShapeDtypeStruct((rows, KV_RANK), BF16),
        jax.ShapeDtypeStruct((rows, LANES), F32),
    )
    return pl.pallas_call(
        functools.partial(_inproj_body, q_scale=HEAD_DIM ** -0.5 * LOG2E),
        grid=(rows // tm, 4),
        in_specs=[
            pl.BlockSpec((tm, D_MODEL), row),
            pl.BlockSpec((1, D_MODEL), const),
            pl.BlockSpec((None, D_MODEL, IN_TN), lambda i, j: (j, 0, 0)),
            pl.BlockSpec((1, HEAD_DIM), const),
            pl.BlockSpec((1, HEAD_DIM), const),
            pl.BlockSpec((1, Q_RANK), const),
            pl.BlockSpec((1, KV_RANK), const),
        ],
        out_specs=[
            pl.BlockSpec((tm, NA_WIDTH), row),
            pl.BlockSpec((tm, NA_WIDTH), row),
            pl.BlockSpec((tm, NA_WIDTH), row),
            pl.BlockSpec((tm, Q_RANK), row),
            pl.BlockSpec((tm, KV_RANK), row),
            pl.BlockSpec((tm, LANES), row),
        ],
        out_shape=outs,
        scratch_shapes=[pltpu.VMEM((tm, D_MODEL), BF16)],
        compiler_params=_cparams("parallel", "arbitrary"),
        name="inproj",
    )(x, g, w4, qg, kg, cqg, ckvg)


def _mla_up_body(cq_ref, ckv_ref, kpe_ref, cos_ref, sin_ref, wq_ref, wkv_ref, qg_ref, kg_ref,
                 qt_ref, k_ref, vt_ref, *, q_scale):
    cq = cq_ref[...]
    ckv = ckv_ref[...]
    cosf = cos_ref[...]
    sinf = sin_ref[...]
    qg = qg_ref[...]
    kg = kg_ref[...]

    def rope(r):
        rot = pltpu.roll(r, 32, 1) + pltpu.roll(r, 96, 1)
        return r * cosf + rot * sinf

    kpe = kpe_ref[...]
    kpe_ss = jnp.sum(kpe * kpe, axis=-1, keepdims=True)
    kpe_rot = rope(kpe * kg[:, LANES:])

    for h in range(HEADS):
        q = _dot(cq, wq_ref[h])
        qn = _rms(q, QK_DIM) * qg * q_scale
        qn = jnp.concatenate([qn[:, :LANES], rope(qn[:, LANES:])], axis=-1)
        qt_ref[h] = qn.T.astype(BF16)
        kv = _dot(ckv, wkv_ref[h])
        kn = kv[:, :LANES]
        ms = (jnp.sum(kn * kn, axis=-1, keepdims=True) + kpe_ss) * (1.0 / QK_DIM)
        r = lax.rsqrt(ms + EPS)
        k_ref[h] = jnp.concatenate([kn * r * kg[:, :LANES], kpe_rot * r], axis=-1).astype(BF16)
        vt_ref[h] = kv[:, LANES:].T.astype(BF16)


def _mla_up(cq, ckv, kpe, cosf, sinf, wq, wkv, qg, kg, nb, tm):
    rows = cq.shape[0]
    nt = rows // nb // tm
    row = lambda b, i: (b * nt + i, 0)
    pos = lambda b, i: (i, 0)
    c2 = lambda b, i: (0, 0)
    c3 = lambda b, i: (0, 0, 0)
    outs = (
        jax.ShapeDtypeStruct((nb, HEADS, QK_PAD, nt * tm), BF16),
        jax.ShapeDtypeStruct((nb, HEADS, nt, tm, QK_PAD), BF16),
        jax.ShapeDtypeStruct((nb, HEADS, nt, HEAD_DIM, tm), BF16),
    )
    return pl.pallas_call(
        functools.partial(_mla_up_body, q_scale=QK_DIM ** -0.5 * LOG2E),
        grid=(nb, nt),
        in_specs=[
            pl.BlockSpec((tm, Q_RANK), row),
            pl.BlockSpec((tm, KV_RANK), row),
            pl.BlockSpec((tm, LANES), row),
            pl.BlockSpec((tm, LANES), pos),
            pl.BlockSpec((tm, LANES), pos),
            pl.BlockSpec((HEADS, Q_RANK, QK_PAD), c3),
            pl.BlockSpec((HEADS, KV_RANK, 2 * HEAD_DIM), c3),
            pl.BlockSpec((1, QK_PAD), c2),
            pl.BlockSpec((1, QK_PAD), c2),
        ],
        out_specs=[
            pl.BlockSpec((None, HEADS, QK_PAD, tm), lambda b, i: (b, 0, 0, i)),
            pl.BlockSpec((None, HEADS, None, tm, QK_PAD), lambda b, i: (b, 0, i, 0, 0)),
            pl.BlockSpec((None, HEADS, None, HEAD_DIM, tm), lambda b, i: (b, 0, i, 0, 0)),
        ],
        out_shape=outs,
        compiler_params=_cparams("parallel", "parallel"),
        name="mla_up",
    )(cq, ckv, kpe, cosf, sinf, wq, wkv, qg, kg)


def _mla_attn_body(qt_ref, qtn_ref, k_ref, vt_ref, km_ref, vmt_ref, o_ref, acc_ref, s_ref, gm_ref,
                   *, nk, tk, carry_over):
    qt = qt_ref[...]
    ng = nk // ATT_GROUP

    def scores(g, half, q=None):
        q = qt if q is None else q
        gm = None
        for u in range(ATT_GROUP):
            for j in range(tk // ATT_SUB):
                rows = pl.ds(j * ATT_SUB, ATT_SUB)
                s = _dot(k_ref[g * ATT_GROUP + u, rows, :], q)
                s_ref[half * ATT_GROUP + u, rows, :] = s
                m = jnp.max(s, axis=0, keepdims=True)
                gm = m if gm is None else jnp.maximum(gm, m)
        return gm

    def attend(g, half, gm, m_prev, l_prev):
        m_new = jnp.maximum(m_prev, gm)
        alpha = jnp.exp2(m_prev - m_new)
        l_new = alpha * l_prev
        pv = None
        for u in range(ATT_GROUP):
            p = jnp.exp2(s_ref[half * ATT_GROUP + u] - m_new)
            l_new = l_new + jnp.sum(p, axis=0, keepdims=True)
            d = _dot(vt_ref[g * ATT_GROUP + u], p.astype(BF16))
            pv = d if pv is None else pv + d
        acc_ref[...] = alpha * acc_ref[...] + pv
        return m_new, l_new

    def two_stages(i, carry):
        m, l, gm0 = carry
        gm1 = scores(2 * i + 1, 1)
        m, l = attend(2 * i, 0, gm0, m, l)
        gm0 = scores(2 * i + 2, 0)
        m, l = attend(2 * i + 1, 1, gm1, m, l)
        return m, l, gm0

    if carry_over:
        @pl.when(pl.program_id(2) == 0)
        def _():
            gm_ref[...] = scores(0, 0)
        gm_first = gm_ref[...]
    else:
        gm_first = scores(0, 0)

    s = _dot(km_ref[...], qt)
    key = lax.broadcasted_iota(jnp.int32, s.shape, 0)
    s = jnp.where(key < N_META, s, NEG)
    m0 = jnp.max(s, axis=0, keepdims=True)
    p = jnp.exp2(s - m0)
    l0 = jnp.sum(p, axis=0, keepdims=True)
    acc_ref[...] = _dot(vmt_ref[...], p.astype(BF16))

    m, l, gm0 = lax.fori_loop(0, ng // 2 - 1, two_stages, (m0, l0, gm_first))
    gm1 = scores(ng - 1, 1)
    m, l = attend(ng - 2, 0, gm0, m, l)
    if carry_over:
        gm_ref[...] = scores(0, 0, qtn_ref[...])
    m, l = attend(ng - 1, 1, gm1, m, l)
    o_ref[...] = (acc_ref[...] / l).T


def _mla_attn(qt, k, vt, km, vmt, tq, shared_q):
    nb, _, nk, tk, _ = k.shape
    nq = qt.shape[3] // tq
    qb = (lambda b: 0) if shared_q else (lambda b: b)
    return pl.pallas_call(
        functools.partial(_mla_attn_body, nk=nk, tk=tk, carry_over=nq > 1),
        grid=(nb, HEADS, nq),
        in_specs=[
            pl.BlockSpec((None, None, QK_PAD, tq), lambda b, h, i: (qb(b), h, 0, i)),
            pl.BlockSpec((None, None, QK_PAD, tq), lambda b, h, i: (qb(b), h, 0, jnp.minimum(i + 1, nq - 1))),
            pl.BlockSpec((None, None, nk, tk, QK_PAD), lambda b, h, i: (b, h, 0, 0, 0),
                         pipeline_mode=pl.Buffered(1)),
            pl.BlockSpec((None, None, nk, HEAD_DIM, tk), lambda b, h, i: (b, h, 0, 0, 0),
                         pipeline_mode=pl.Buffered(1)),
            pl.BlockSpec((None, None, None, META_ROWS, QK_PAD), lambda b, h, i: (0, h, 0, 0, 0)),
            pl.BlockSpec((None, None, None, HEAD_DIM, META_ROWS), lambda b, h, i: (0, h, 0, 0, 0)),
        ],
        out_specs=pl.BlockSpec((None, tq, HEAD_DIM), lambda b, h, i: (b, i, h)),
        out_shape=jax.ShapeDtypeStruct((nb, nq * tq, NA_WIDTH), F32),
        scratch_shapes=[pltpu.VMEM((HEAD_DIM, tq), F32), pltpu.VMEM((2 * ATT_GROUP, tk, tq), F32),
                        pltpu.VMEM((1, tq), F32)],
        compiler_params=_cparams("parallel", "parallel", "arbitrary"),
        name="mla_attn",
    )(qt, qt, k, vt, km, vmt)


def _na_bias_body(rpb_ref, o_ref, t_ref, *, rows):
    h = pl.program_id(0)
    n_dr = 2 * NA_WIN_ROWS - 1
    n_dc = 2 * NA_WIN_COLS - 1
    shape = (GRID_W, LANES)
    qc = lax.broadcasted_iota(jnp.int32, shape, 0)
    lane = lax.broadcasted_iota(jnp.int32, shape, 1)
    left = lane < GRID_W
    kc = jnp.where(left, lane, lane - GRID_W)
    cs = jnp.clip(qc - NA_WIN_COLS // 2, 0, GRID_W - NA_WIN_COLS)
    col_ok = (kc >= cs) & (kc < cs + NA_WIN_COLS)
    dc = kc - qc + NA_WIN_COLS - 1

    def build(dr, carry):
        def pick(j, acc):
            return jnp.where(dc == j, rpb_ref[h, dr, j], acc)

        tile = lax.fori_loop(0, n_dc, pick, jnp.zeros(shape, F32))
        t_ref[dr] = jnp.where(col_ok, tile * LOG2E, NEG)
        return carry

    lax.fori_loop(0, n_dr, build, 0)

    neg = jnp.full(shape, NEG, F32)
    for c, r0 in enumerate((0, 2 * NA_QROWS, rows - NA_QROWS)):
        for qr in range(NA_QROWS):
            abs_qr = r0 + qr
            rs = min(max(abs_qr - NA_WIN_ROWS // 2, 0), rows - NA_WIN_ROWS)
            for a in range(NA_KROWS // 2):
                halves = []
                for kr in (r0 - NA_QROWS + 2 * a, r0 - NA_QROWS + 2 * a + 1):
                    in_window = rs <= kr < rs + NA_WIN_ROWS
                    halves.append(t_ref[kr - abs_qr + NA_WIN_ROWS - 1] if in_window else neg)
                o_ref[c, qr * GRID_W:(qr + 1) * GRID_W, a * LANES:(a + 1) * LANES] = (
                    jnp.where(left, halves[0], halves[1]))


def _na_bias(rpb, rows):
    return pl.pallas_call(
        functools.partial(_na_bias_body, rows=rows),
        grid=(HEADS,),
        in_specs=[pl.BlockSpec(memory_space=pltpu.SMEM)],
        out_specs=pl.BlockSpec((3, None, NA_TQ, NA_KROWS * GRID_W), lambda h: (0, h, 0, 0)),
        out_shape=jax.ShapeDtypeStruct((3, HEADS, NA_TQ, NA_KROWS * GRID_W), F32),
        scratch_shapes=[pltpu.VMEM((2 * NA_WIN_ROWS - 1, GRID_W, LANES), F32)],
        compiler_params=_cparams("parallel"),
        name="na_bias",
    )(rpb)


def _na_body(q_ref, kp_ref, kc_ref, kn_ref, vp_ref, vc_ref, vn_ref, km_ref, vm_ref,
             bias_ref, mb_ref, g_ref, o_ref, out_ref):
    for h in range(HEADS):
        sl = slice(h * HEAD_DIM, (h + 1) * HEAD_DIM)
        q = q_ref[:, sl]
        s = jnp.concatenate(
            [_dot_nt(q, kp_ref[:, sl]), _dot_nt(q, kc_ref[:, sl]), _dot_nt(q, kn_ref[:, sl])],
            axis=-1) + bias_ref[h]
        sm = _dot_nt(q, km_ref[:, sl]) + mb_ref[h]
        m = jnp.maximum(jnp.max(s, axis=-1, keepdims=True), jnp.max(sm, axis=-1, keepdims=True))
        p = jnp.exp2(s - m)
        pm = jnp.exp2(sm - m)
        l = jnp.sum(pm, axis=-1, keepdims=True) + jnp.sum(p, axis=-1, keepdims=True)
        pb = p.astype(BF16)
        o = (_dot(pm.astype(BF16), vm_ref[:, sl])
             + _dot(pb[:, :NA_TQ], vp_ref[:, sl])
             + _dot(pb[:, NA_TQ:2 * NA_TQ], vc_ref[:, sl])
             + _dot(pb[:, 2 * NA_TQ:], vn_ref[:, sl]))
        out_ref[:, sl] = o / l
    o_ref[...] = (_rms(out_ref[...]) * g_ref[...]).astype(BF16)


def _na_attn(qa, ka, va, km, vm, bias, mb, g, nb):
    rows = qa.shape[0]
    n = rows // nb // NA_TQ
    cur = lambda b, i: (b * n + i, 0)
    prev = lambda b, i: (b * n + jnp.maximum(i - 1, 0), 0)
    nxt = lambda b, i: (b * n + jnp.minimum(i + 1, n - 1), 0)
    blk = lambda f: pl.BlockSpec((NA_TQ, NA_WIDTH), f)
    case = lambda b, i: (jnp.where(i == 0, 0, jnp.where(i == n - 1, 2, 1)), 0, 0, 0)
    return pl.pallas_call(
        _na_body,
        grid=(nb, n),
        in_specs=[
            blk(cur), blk(prev), blk(cur), blk(nxt), blk(prev), blk(cur), blk(nxt),
            pl.BlockSpec((META_ROWS, NA_WIDTH), lambda b, i: (0, 0)),
            pl.BlockSpec((META_ROWS, NA_WIDTH), lambda b, i: (0, 0)),
            pl.BlockSpec((None, HEADS, NA_TQ, NA_KROWS * GRID_W), case),
            pl.BlockSpec((HEADS, 1, META_ROWS), lambda b, i: (0, 0, 0)),
            pl.BlockSpec((1, NA_WIDTH), lambda b, i: (0, 0)),
        ],
        out_specs=blk(cur),
        out_shape=jax.ShapeDtypeStruct((rows, NA_WIDTH), BF16),
        scratch_shapes=[pltpu.VMEM((NA_TQ, NA_WIDTH), F32)],
        compiler_params=_cparams("parallel", "arbitrary"),
        name="na_attn",
    )(qa, ka, ka, ka, va, va, va, km, vm, bias, mb, g)


def _na_meta_body(q_ref, k_ref, v_ref, mb_ref, g_ref, o_ref, out_ref):
    for h in range(HEADS):
        sl = slice(h * HEAD_DIM, (h + 1) * HEAD_DIM)
        s = _dot_nt(q_ref[:, sl], k_ref[:, sl]) + mb_ref[h]
        m = jnp.max(s, axis=-1, keepdims=True)
        p = jnp.exp2(s - m)
        l = jnp.sum(p, axis=-1, keepdims=True)
        out_ref[:, sl] = _dot(p.astype(BF16), v_ref[:, sl]) / l
    o_ref[...] = (_rms(out_ref[...]) * g_ref[...]).astype(BF16)


def _na_meta(qm, km, vm, mb, g):
    full = lambda s: pl.BlockSpec(s, lambda i: (0,) * len(s))
    return pl.pallas_call(
        _na_meta_body,
        grid=(1,),
        in_specs=[full((META_ROWS, NA_WIDTH))] * 3 + [full((HEADS, 1, META_ROWS)), full((1, NA_WIDTH))],
        out_specs=full((META_ROWS, NA_WIDTH)),
        out_shape=jax.ShapeDtypeStruct((META_ROWS, NA_WIDTH), BF16),
        scratch_shapes=[pltpu.VMEM((META_ROWS, NA_WIDTH), F32)],
        compiler_params=_cparams("arbitrary"),
        name="na_meta",
    )(qm, km, vm, mb, g)


def _outproj_body(x_ref, a_ref, b_ref, bg_ref, w_ref, fg_ref, h_ref, xn_ref):
    bn = (_rms(b_ref[...]) * bg_ref[...]).astype(BF16)
    mix = jnp.concatenate([a_ref[...], bn], axis=-1)
    h = x_ref[...] + _dot(mix, w_ref[...])
    h_ref[...] = h
    xn_ref[...] = (_rms(h) * fg_ref[...]).astype(BF16)


def _outproj(x, a, b, bg, w, fg, tm, x_map, a_map, b_map, rows):
    c = lambda i: (0, 0)
    return pl.pallas_call(
        _outproj_body,
        grid=(rows // tm,),
        in_specs=[
            pl.BlockSpec((tm, D_MODEL), x_map),
            pl.BlockSpec((tm, NA_WIDTH), a_map),
            pl.BlockSpec((tm, NA_WIDTH), b_map),
            pl.BlockSpec((1, NA_WIDTH), c),
            pl.BlockSpec((2 * NA_WIDTH, D_MODEL), c),
            pl.BlockSpec((1, D_MODEL), c),
        ],
        out_specs=[pl.BlockSpec((tm, D_MODEL), lambda i: (i, 0)),
                   pl.BlockSpec((tm, D_MODEL), lambda i: (i, 0))],
        out_shape=(jax.ShapeDtypeStruct((rows, D_MODEL), F32),
                   jax.ShapeDtypeStruct((rows, D_MODEL), BF16)),
        compiler_params=_cparams("parallel"),
        name="outproj",
    )(x, a, b, bg, w, fg)


def _ffn_body(xm_ref, xp_ref, xx_ref, xmeta_ref, h_ref, wg_ref, wu_ref, wd_ref, cw_ref, cb_ref,
              o_ref, xe_ref, *, tm, tpb):
    i = pl.program_id(0)
    j = pl.program_id(1)

    @pl.when(j == 0)
    def _():
        first = (i % tpb) == 0
        last = (i % tpb) == tpb - 1
        xe_ref[0:HALO, :] = jnp.where(first, xmeta_ref[...], xp_ref[...])
        xe_ref[HALO:HALO + tm, :] = xm_ref[...]
        xe_ref[HALO + tm:2 * HALO + tm, :] = jnp.where(last, jnp.zeros_like(xx_ref[...]), xx_ref[...])
        o_ref[...] = h_ref[...]

    g = _dot(xe_ref[...], wg_ref[...])
    u = _dot(xe_ref[HALO:HALO + tm, :], wu_ref[...])
    cw = cw_ref[...]
    gc = (cb_ref[...] + cw[0:1] * g[HALO - 1:HALO - 1 + tm]
          + cw[1:2] * g[HALO:HALO + tm] + cw[2:3] * g[HALO + 1:HALO + 1 + tm])
    act = gc * (1.0 / (1.0 + jnp.exp(-gc))) * u
    o_ref[...] += _dot(act.astype(BF16), wd_ref[...])


def _ffn(xn, xn_meta, h1, wg, wu, wd, cw, cb, nb, tm, tf):
    rows = xn.shape[0]
    nt = rows // tm
    tpb = nt // nb
    hb = tm // HALO
    last_hb = rows // HALO - 1
    return pl.pallas_call(
        functools.partial(_ffn_body, tm=tm, tpb=tpb),
        grid=(nt, D_FF // tf),
        in_specs=[
            pl.BlockSpec((tm, D_MODEL), lambda i, j: (i, 0)),
            pl.BlockSpec((HALO, D_MODEL), lambda i, j: (jnp.maximum(i * hb - 1, 0), 0)),
            pl.BlockSpec((HALO, D_MODEL), lambda i, j: (jnp.minimum((i + 1) * hb, last_hb), 0)),
            pl.BlockSpec((HALO, D_MODEL), lambda i, j: ((i // tpb) * (META_ROWS // HALO), 0)),
            pl.BlockSpec((tm, D_MODEL), lambda i, j: (i, 0)),
            pl.BlockSpec((D_MODEL, tf), lambda i, j: (0, j)),
            pl.BlockSpec((D_MODEL, tf), lambda i, j: (0, j)),
            pl.BlockSpec((tf, D_MODEL), lambda i, j: (j, 0)),
            pl.BlockSpec((3, tf), lambda i, j: (0, j)),
            pl.BlockSpec((1, tf), lambda i, j: (0, j)),
        ],
        out_specs=pl.BlockSpec((tm, D_MODEL), lambda i, j: (i, 0)),
        out_shape=jax.ShapeDtypeStruct((rows, D_MODEL), F32),
        scratch_shapes=[pltpu.VMEM((tm + 2 * HALO, D_MODEL), BF16)],
        compiler_params=_cparams("parallel", "arbitrary"),
        name="ffn",
    )(xn, xn, xn, xn_meta, h1, wg, wu, wd, cw, cb)


def _rope_tables(pos):
    inv = ROPE_THETA ** (-jnp.arange(0, ROPE_DIM, 2, dtype=F32) / ROPE_DIM)
    ang = pos.astype(F32)[:, None] * inv[None, :]
    cos, sin = jnp.cos(ang), jnp.sin(ang)
    z = jnp.zeros_like(cos)
    return (jnp.concatenate([cos, cos, z, z], axis=-1),
            jnp.concatenate([-sin, sin, z, z], axis=-1))


def _pad_lanes(v, width):
    return jnp.pad(v, ((0, 0), (0, width - v.shape[-1])))


def kernel(x, meta_tokens, mix_norm_g, w_in, na_q_g, na_k_g, na_rpb, na_meta_bias, mla_cq_g, mla_ckv_g,
           w_q_up, w_kv_up, mla_q_g, mla_k_g, na_out_g, mla_out_g, w_out, ffn_norm_g, w_gate, w_up,
           conv_w, conv_b, w_down):
    nb, seq, d = x.shape
    assert d == D_MODEL and w_in.shape[0] == 1, "one layer of width 2048"
    rows_grid = seq // GRID_W
    assert seq % max(IN_TM, UP_TM, ATT_TQ, FFN_TM) == 0 and rows_grid >= 4 * NA_QROWS

    w_in_p = _pad_lanes(w_in[0], 4 * IN_TN).astype(BF16)
    w4 = w_in_p.reshape(D_MODEL, 4, IN_TN).transpose(1, 0, 2)
    wq = w_q_up[0].reshape(Q_RANK, HEADS, QK_DIM).transpose(1, 0, 2)
    wq = jnp.pad(wq, ((0, 0), (0, 0), (0, QK_PAD - QK_DIM))).astype(BF16)
    wkv = w_kv_up[0].reshape(KV_RANK, HEADS, 2 * HEAD_DIM).transpose(1, 0, 2).astype(BF16)
    wo = w_out[0].astype(BF16)
    wg, wu, wd = w_gate[0].astype(BF16), w_up[0].astype(BF16), w_down[0].astype(BF16)
    row = lambda v: v.reshape(1, -1)
    qg_pad, kg_pad = _pad_lanes(mla_q_g, QK_PAD), _pad_lanes(mla_k_g, QK_PAD)
    mb = jnp.pad(na_meta_bias[0] * LOG2E, ((0, 0), (0, META_ROWS - N_META)),
                 constant_values=NEG).reshape(HEADS, 1, META_ROWS)

    xr = x.reshape(nb * seq, D_MODEL)
    xm = jnp.pad(meta_tokens.astype(x.dtype), ((0, META_ROWS - N_META), (0, 0)))
    cos_r, sin_r = _rope_tables(N_META + jnp.arange(seq))
    cos_m, sin_m = _rope_tables(jnp.arange(META_ROWS))

    inproj = functools.partial(_inproj, g=row(mix_norm_g), w4=w4, qg=row(na_q_g), kg=row(na_k_g),
                               cqg=row(mla_cq_g), ckvg=row(mla_ckv_g))
    qa, ka, va, cq, ckv, kpe = inproj(xr, tm=IN_TM)
    qa_m, ka_m, va_m, cq_m, ckv_m, kpe_m = inproj(xm, tm=META_ROWS)

    up = functools.partial(_mla_up, wq=wq, wkv=wkv, qg=qg_pad, kg=kg_pad)
    qt, kk, vt = up(cq, ckv, kpe, cos_r, sin_r, nb=nb, tm=UP_TM)
    qt_m, kk_m, vt_m = up(cq_m, ckv_m, kpe_m, cos_m, sin_m, nb=1, tm=META_ROWS)

    bias = _na_bias(na_rpb[0] , rows_grid)
    a_n = _na_attn(qa, ka, va, ka_m, va_m, bias, mb, row(na_out_g), nb)
    a_n_m = _na_meta(qa_m, ka_m, va_m, mb, row(na_out_g))

    out_b = _mla_attn(qt, kk, vt, kk_m, vt_m, ATT_TQ, shared_q=False)
    out_b_m = _mla_attn(qt_m, kk, vt, kk_m, vt_m, META_ROWS, shared_q=True)

    op = functools.partial(_outproj, bg=row(mla_out_g), w=wo, fg=row(ffn_norm_g))
    ident = lambda i: (i, 0)
    zero = lambda i: (0, 0)
    h1, xn2 = op(xr, a_n, out_b.reshape(nb * seq, NA_WIDTH), tm=OUT_TM,
                 x_map=ident, a_map=ident, b_map=ident, rows=nb * seq)
    _, xn2_m = op(xm, a_n_m, out_b_m.reshape(nb * META_ROWS, NA_WIDTH), tm=META_ROWS,
                  x_map=zero, a_map=zero, b_map=ident, rows=nb * META_ROWS)

    out = _ffn(xn2, xn2_m, h1, wg, wu, wd, conv_w[0], row(conv_b), nb, FFN_TM, FFN_TF)
    return out.reshape(nb, seq, D_MODEL)
```

```python
import functools

import jax
import jax.numpy as jnp
import numpy as np
from jax import lax
from jax.experimental import pallas as pl
from jax.experimental.pallas import tpu as pltpu

F32 = jnp.float32
BF16 = jnp.bfloat16

LANES = 128
BF16_SUBLANES = 16
VMEM_LIMIT_BYTES = 56 * 1024 * 1024

D_MODEL = 2048
GRID_W = 64
N_META = 16
EPS = 1e-6
NEG = -1e30
LOG2E = 1.4426950408889634
HEADS = 8
HEAD_DIM = 128
NA_WIDTH = HEADS * HEAD_DIM
NA_WIN_ROWS = 8
NA_WIN_COLS = 16
Q_RANK = 512
KV_RANK = 256
ROPE_DIM = 64
QK_DIM = HEAD_DIM + ROPE_DIM
QK_PAD = 2 * LANES
ROPE_THETA = 10000.0
D_FF = 5632
META_ROWS = LANES

IN_TM = 512
IN_TN = 1024
UP_TM = 512
ATT_TQ = 1024
ATT_SUB = 256
ATT_GROUP = 4
NA_QROWS = 4
NA_TQ = NA_QROWS * GRID_W
NA_KROWS = 3 * NA_QROWS
OUT_TM = 256
FFN_TM = 512
FFN_TF = 512
HALO = BF16_SUBLANES


def _cparams(*sem):
    return pltpu.CompilerParams(dimension_semantics=sem, vmem_limit_bytes=VMEM_LIMIT_BYTES)


def _rms(v, n=None):
    if n is None:
        ms = jnp.mean(v * v, axis=-1, keepdims=True)
    else:
        ms = jnp.sum(v * v, axis=-1, keepdims=True) * (1.0 / n)
    return v * lax.rsqrt(ms + EPS)


def _dot(a, b):
    return jnp.dot(a, b, preferred_element_type=F32)


def _dot_nt(a, b):
    return lax.dot_general(a, b, (((1,), (1,)), ((), ())), preferred_element_type=F32)


def _inproj_body(x_ref, g_ref, w_ref, qg_ref, kg_ref, cqg_ref, ckvg_ref,
                 qa_ref, ka_ref, va_ref, cq_ref, ckv_ref, kpe_ref, xn_ref, *, q_scale):
    j = pl.program_id(1)

    @pl.when(j == 0)
    def _():
        xn_ref[...] = (_rms(x_ref[...]) * g_ref[...]).astype(BF16)

    def project(lo, hi):
        return _dot(xn_ref[...], w_ref[:, lo:hi])

    def head_norm(gain_ref, out_ref, post):
        for t in range(HEADS // 2):
            y = project(2 * t * HEAD_DIM, 2 * (t + 1) * HEAD_DIM)
            for u in range(2):
                sl = slice((2 * t + u) * HEAD_DIM, (2 * t + u + 1) * HEAD_DIM)
                yh = y[:, u * HEAD_DIM:(u + 1) * HEAD_DIM]
                out_ref[:, sl] = (_rms(yh) * gain_ref[...] * post).astype(BF16)

    @pl.when(j == 0)
    def _():
        head_norm(qg_ref, qa_ref, q_scale)

    @pl.when(j == 1)
    def _():
        head_norm(kg_ref, ka_ref, 1.0)

    @pl.when(j == 2)
    def _():
        va_ref[...] = project(0, IN_TN).astype(BF16)

    @pl.when(j == 3)
    def _():
        cq_ref[...] = (_rms(project(0, Q_RANK)) * cqg_ref[...]).astype(BF16)
        y = project(Q_RANK, IN_TN)
        ckv_ref[...] = (_rms(y[:, :KV_RANK]) * ckvg_ref[...]).astype(BF16)
        kpe_ref[...] = y[:, KV_RANK:KV_RANK + LANES]


def _inproj(x, g, w4, qg, kg, cqg, ckvg, tm):
    rows = x.shape[0]
    row = lambda i, j: (i, 0)
    const = lambda i, j: (0, 0)
    outs = (
        jax.ShapeDtypeStruct((rows, NA_WIDTH), BF16),
        jax.ShapeDtypeStruct((rows, NA_WIDTH), BF16),
        jax.ShapeDtypeStruct((rows, NA_WIDTH), BF16),
        jax.ShapeDtypeStruct((rows, Q_RANK), BF16),
        jax.ShapeDtypeStruct((rows, KV_RANK), BF16),
        jax.ShapeDtypeStruct((rows, LANES), F32),
    )
    return pl.pallas_call(
        functools.partial(_inproj_body, q_scale=HEAD_DIM ** -0.5 * LOG2E),
        grid=(rows // tm, 4),
        in_specs=[
            pl.BlockSpec((tm, D_MODEL), row),
            pl.BlockSpec((1, D_MODEL), const),
            pl.BlockSpec((None, D_MODEL, IN_TN), lambda i, j: (j, 0, 0)),
            pl.BlockSpec((1, HEAD_DIM), const),
            pl.BlockSpec((1, HEAD_DIM), const),
            pl.BlockSpec((1, Q_RANK), const),
            pl.BlockSpec((1, KV_RANK), const),
        ],
        out_specs=[
            pl.BlockSpec((tm, NA_WIDTH), row),
            pl.BlockSpec((tm, NA_WIDTH), row),
            pl.BlockSpec((tm, NA_WIDTH), row),
            pl.BlockSpec((tm, Q_RANK), row),
            pl.BlockSpec((tm, KV_RANK), row),
            pl.BlockSpec((tm, LANES), row),
        ],
        out_shape=outs,
        scratch_shapes=[pltpu.VMEM((tm, D_MODEL), BF16)],
        compiler_params=_cparams("parallel", "arbitrary"),
        name="inproj",
    )(x, g, w4, qg, kg, cqg, ckvg)


def _mla_up_body(cq_ref, ckv_ref, kpe_ref, cos_ref, sin_ref, cost_ref, sint_ref,
                 wqt_ref, wk_ref, wvt_ref, qgt_ref, kg_ref, qt_ref, k_ref, vt_ref, *, q_scale):
    half = ROPE_DIM // 2
    ckv = ckv_ref[...]
    cqt = cq_ref[...].astype(F32).T.astype(BF16)
    ckvt = ckv.astype(F32).T.astype(BF16)
    qgt = qgt_ref[...]
    cost = cost_ref[...]
    sint = sint_ref[...]
    cosf = cos_ref[...]
    sinf = sin_ref[...]
    kg = kg_ref[...]

    kpe = kpe_ref[...]
    kpe_ss = jnp.sum(kpe * kpe, axis=-1, keepdims=True)
    r = kpe * kg[:, LANES:]
    kpe_rot = r * cosf + (pltpu.roll(r, half, 1) + pltpu.roll(r, LANES - half, 1)) * sinf

    for h in range(HEADS):
        q = _dot(wqt_ref[h], cqt)
        ms = jnp.sum(q * q, axis=0, keepdims=True) * (1.0 / QK_DIM)
        qn = q * (lax.rsqrt(ms + EPS) * q_scale) * qgt
        x1 = qn[LANES:LANES + half]
        x2 = qn[LANES + half:LANES + 2 * half]
        qt_ref[h, 0:LANES, :] = qn[:LANES].astype(BF16)
        qt_ref[h, LANES:LANES + half, :] = (x1 * cost - x2 * sint).astype(BF16)
        qt_ref[h, LANES + half:LANES + 2 * half, :] = (x2 * cost + x1 * sint).astype(BF16)
        qt_ref[h, LANES + 2 * half:, :] = jnp.zeros((QK_PAD - QK_DIM, q.shape[1]), BF16)
        vt_ref[h] = _dot(wvt_ref[h], ckvt).astype(BF16)

    for t in range(HEADS // 2):
        kn2 = _dot(ckv, wk_ref[t])
        for u in range(2):
            kn = kn2[:, u * LANES:(u + 1) * LANES]
            ms = (jnp.sum(kn * kn, axis=-1, keepdims=True) + kpe_ss) * (1.0 / QK_DIM)
            rn = lax.rsqrt(ms + EPS)
            k_ref[2 * t + u] = jnp.concatenate([kn * rn * kg[:, :LANES], kpe_rot * rn], axis=-1).astype(BF16)


def _mla_up(cq, ckv, kpe, cosf, sinf, cost, sint, wqt, wk, wvt, qg, kg, nb, tm):
    rows = cq.shape[0]
    nt = rows // nb // tm
    row = lambda b, i: (b * nt + i, 0)
    pos = lambda b, i: (i, 0)
    post = lambda b, i: (0, i)
    c2 = lambda b, i: (0, 0)
    c3 = lambda b, i: (0, 0, 0)
    qgt = jnp.broadcast_to(qg.reshape(QK_PAD, 1), (QK_PAD, tm))
    outs = (
        jax.ShapeDtypeStruct((nb, HEADS, QK_PAD, nt * tm), BF16),
        jax.ShapeDtypeStruct((nb, HEADS, nt, tm, QK_PAD), BF16),
        jax.ShapeDtypeStruct((nb, HEADS, nt, HEAD_DIM, tm), BF16),
    )
    return pl.pallas_call(
        functools.partial(_mla_up_body, q_scale=QK_DIM ** -0.5 * LOG2E),
        grid=(nb, nt),
        in_specs=[
            pl.BlockSpec((tm, Q_RANK), row),
            pl.BlockSpec((tm, KV_RANK), row),
            pl.BlockSpec((tm, LANES), row),
            pl.BlockSpec((tm, LANES), pos),
            pl.BlockSpec((tm, LANES), pos),
            pl.BlockSpec((ROPE_DIM // 2, tm), post),
            pl.BlockSpec((ROPE_DIM // 2, tm), post),
            pl.BlockSpec((HEADS, QK_PAD, Q_RANK), c3),
            pl.BlockSpec((HEADS // 2, KV_RANK, 2 * HEAD_DIM), c3),
            pl.BlockSpec((HEADS, HEAD_DIM, KV_RANK), c3),
            pl.BlockSpec((QK_PAD, tm), c2),
            pl.BlockSpec((1, QK_PAD), c2),
        ],
        out_specs=[
            pl.BlockSpec((None, HEADS, QK_PAD, tm), lambda b, i: (b, 0, 0, i)),
            pl.BlockSpec((None, HEADS, None, tm, QK_PAD), lambda b, i: (b, 0, i, 0, 0)),
            pl.BlockSpec((None, HEADS, None, HEAD_DIM, tm), lambda b, i: (b, 0, i, 0, 0)),
        ],
        out_shape=outs,
        compiler_params=_cparams("parallel", "parallel"),
        name="mla_up",
    )(cq, ckv, kpe, cosf, sinf, cost, sint, wqt, wk, wvt, qgt, kg)


def _mla_attn_body(qt_ref, qtn_ref, k_ref, vt_ref, km_ref, vmt_ref, o_ref, acc_ref, s_ref, gm_ref,
                   *, nk, tk, carry_over):
    qt = qt_ref[...]
    ng = nk // ATT_GROUP

    def scores(g, half, q=None):
        q = qt if q is None else q
        gm = None
        for u in range(ATT_GROUP):
            for j in range(tk // ATT_SUB):
                rows = pl.ds(j * ATT_SUB, ATT_SUB)
                s = _dot(k_ref[g * ATT_GROUP + u, rows, :], q)
                s_ref[half * ATT_GROUP + u, rows, :] = s
                m = jnp.max(s, axis=0, keepdims=True)
                gm = m if gm is None else jnp.maximum(gm, m)
        return gm

    def attend(g, half, gm, m_prev, l_prev):
        m_new = jnp.maximum(m_prev, gm)
        alpha = jnp.exp2(m_prev - m_new)
        l_new = alpha * l_prev
        pv = None
        for u in range(ATT_GROUP):
            p = jnp.exp2(s_ref[half * ATT_GROUP + u] - m_new)
            l_new = l_new + jnp.sum(p, axis=0, keepdims=True)
            d = _dot(vt_ref[g * ATT_GROUP + u], p.astype(BF16))
            pv = d if pv is None else pv + d
        acc_ref[...] = alpha * acc_ref[...] + pv
        return m_new, l_new

    def two_stages(i, carry):
        m, l, gm0 = carry
        gm1 = scores(2 * i + 1, 1)
        m, l = attend(2 * i, 0, gm0, m, l)
        gm0 = scores(2 * i + 2, 0)
        m, l = attend(2 * i + 1, 1, gm1, m, l)
        return m, l, gm0

    if carry_over:
        @pl.when(pl.program_id(2) == 0)
        def _():
            gm_ref[...] = scores(0, 0)
        gm_first = gm_ref[...]
    else:
        gm_first = scores(0, 0)

    s = _dot(km_ref[...], qt)
    key = lax.broadcasted_iota(jnp.int32, s.shape, 0)
    s = jnp.where(key < N_META, s, NEG)
    m0 = jnp.max(s, axis=0, keepdims=True)
    p = jnp.exp2(s - m0)
    l0 = jnp.sum(p, axis=0, keepdims=True)
    acc_ref[...] = _dot(vmt_ref[...], p.astype(BF16))

    m, l, gm0 = lax.fori_loop(0, ng // 2 - 1, two_stages, (m0, l0, gm_first))
    gm1 = scores(ng - 1, 1)
    m, l = attend(ng - 2, 0, gm0, m, l)
    if carry_over:
        gm_ref[...] = scores(0, 0, qtn_ref[...])
    m, l = attend(ng - 1, 1, gm1, m, l)
    o_ref[...] = (acc_ref[...] / l).T


def _mla_attn(qt, k, vt, km, vmt, tq, shared_q):
    nb, _, nk, tk, _ = k.shape
    nq = qt.shape[3] // tq
    qb = (lambda b: 0) if shared_q else (lambda b: b)
    return pl.pallas_call(
        functools.partial(_mla_attn_body, nk=nk, tk=tk, carry_over=nq > 1),
        grid=(nb, HEADS, nq),
        in_specs=[
            pl.BlockSpec((None, None, QK_PAD, tq), lambda b, h, i: (qb(b), h, 0, i)),
            pl.BlockSpec((None, None, QK_PAD, tq), lambda b, h, i: (qb(b), h, 0, jnp.minimum(i + 1, nq - 1))),
            pl.BlockSpec((None, None, nk, tk, QK_PAD), lambda b, h, i: (b, h, 0, 0, 0),
                         pipeline_mode=pl.Buffered(1)),
            pl.BlockSpec((None, None, nk, HEAD_DIM, tk), lambda b, h, i: (b, h, 0, 0, 0),
                         pipeline_mode=pl.Buffered(1)),
            pl.BlockSpec((None, None, None, META_ROWS, QK_PAD), lambda b, h, i: (0, h, 0, 0, 0)),
            pl.BlockSpec((None, None, None, HEAD_DIM, META_ROWS), lambda b, h, i: (0, h, 0, 0, 0)),
        ],
        out_specs=pl.BlockSpec((None, tq, HEAD_DIM), lambda b, h, i: (b, i, h)),
        out_shape=jax.ShapeDtypeStruct((nb, nq * tq, NA_WIDTH), F32),
        scratch_shapes=[pltpu.VMEM((HEAD_DIM, tq), F32), pltpu.VMEM((2 * ATT_GROUP, tk, tq), F32),
                        pltpu.VMEM((1, tq), F32)],
        compiler_params=_cparams("parallel", "parallel", "arbitrary"),
        name="mla_attn",
    )(qt, qt, k, vt, km, vmt)


def _na_bias_body(rpb_ref, o_ref, t_ref, *, rows):
    h = pl.program_id(0)
    n_dr = 2 * NA_WIN_ROWS - 1
    n_dc = 2 * NA_WIN_COLS - 1
    shape = (GRID_W, LANES)
    qc = lax.broadcasted_iota(jnp.int32, shape, 0)
    lane = lax.broadcasted_iota(jnp.int32, shape, 1)
    left = lane < GRID_W
    kc = jnp.where(left, lane, lane - GRID_W)
    cs = jnp.clip(qc - NA_WIN_COLS // 2, 0, GRID_W - NA_WIN_COLS)
    col_ok = (kc >= cs) & (kc < cs + NA_WIN_COLS)
    dc = kc - qc + NA_WIN_COLS - 1

    def build(dr, carry):
        def pick(j, acc):
            return jnp.where(dc == j, rpb_ref[h, dr, j], acc)

        tile = lax.fori_loop(0, n_dc, pick, jnp.zeros(shape, F32))
        t_ref[dr] = jnp.where(col_ok, tile * LOG2E, NEG)
        return carry

    lax.fori_loop(0, n_dr, build, 0)

    neg = jnp.full(shape, NEG, F32)
    for c, r0 in enumerate((0, 2 * NA_QROWS, rows - NA_QROWS)):
        for qr in range(NA_QROWS):
            abs_qr = r0 + qr
            rs = min(max(abs_qr - NA_WIN_ROWS // 2, 0), rows - NA_WIN_ROWS)
            for a in range(NA_KROWS // 2):
                halves = []
                for kr in (r0 - NA_QROWS + 2 * a, r0 - NA_QROWS + 2 * a + 1):
                    in_window = rs <= kr < rs + NA_WIN_ROWS
                    halves.append(t_ref[kr - abs_qr + NA_WIN_ROWS - 1] if in_window else neg)
                o_ref[c, qr * GRID_W:(qr + 1) * GRID_W, a * LANES:(a + 1) * LANES] = (
                    jnp.where(left, halves[0], halves[1]))


def _na_bias(rpb, rows):
    return pl.pallas_call(
        functools.partial(_na_bias_body, rows=rows),
        grid=(HEADS,),
        in_specs=[pl.BlockSpec(memory_space=pltpu.SMEM)],
        out_specs=pl.BlockSpec((3, None, NA_TQ, NA_KROWS * GRID_W), lambda h: (0, h, 0, 0)),
        out_shape=jax.ShapeDtypeStruct((3, HEADS, NA_TQ, NA_KROWS * GRID_W), F32),
        scratch_shapes=[pltpu.VMEM((2 * NA_WIN_ROWS - 1, GRID_W, LANES), F32)],
        compiler_params=_cparams("parallel"),
        name="na_bias",
    )(rpb)


def _na_body(q_ref, kp_ref, kc_ref, kn_ref, vp_ref, vc_ref, vn_ref, km_ref, vm_ref,
             bias_ref, mb_ref, g_ref, o_ref, out_ref):
    for h in range(HEADS):
        sl = slice(h * HEAD_DIM, (h + 1) * HEAD_DIM)
        q = q_ref[:, sl]
        s = jnp.concatenate(
            [_dot_nt(q, kp_ref[:, sl]), _dot_nt(q, kc_ref[:, sl]), _dot_nt(q, kn_ref[:, sl])],
            axis=-1) + bias_ref[h]
        sm = _dot_nt(q, km_ref[:, sl]) + mb_ref[h]
        m = jnp.maximum(jnp.max(s, axis=-1, keepdims=True), jnp.max(sm, axis=-1, keepdims=True))
        p = jnp.exp2(s - m)
        pm = jnp.exp2(sm - m)
        l = jnp.sum(pm, axis=-1, keepdims=True) + jnp.sum(p, axis=-1, keepdims=True)
        pb = p.astype(BF16)
        o = (_dot(pm.astype(BF16), vm_ref[:, sl])
             + _dot(pb[:, :NA_TQ], vp_ref[:, sl])
             + _dot(pb[:, NA_TQ:2 * NA_TQ], vc_ref[:, sl])
             + _dot(pb[:, 2 * NA_TQ:], vn_ref[:, sl]))
        out_ref[:, sl] = o / l
    o_ref[...] = (_rms(out_ref[...]) * g_ref[...]).astype(BF16)


def _na_attn(qa, ka, va, km, vm, bias, mb, g, nb):
    rows = qa.shape[0]
    n = rows // nb // NA_TQ
    cur = lambda b, i: (b * n + i, 0)
    prev = lambda b, i: (b * n + jnp.maximum(i - 1, 0), 0)
    nxt = lambda b, i: (b * n + jnp.minimum(i + 1, n - 1), 0)
    blk = lambda f: pl.BlockSpec((NA_TQ, NA_WIDTH), f)
    case = lambda b, i: (jnp.where(i == 0, 0, jnp.where(i == n - 1, 2, 1)), 0, 0, 0)
    return pl.pallas_call(
        _na_body,
        grid=(nb, n),
        in_specs=[
            blk(cur), blk(prev), blk(cur), blk(nxt), blk(prev), blk(cur), blk(nxt),
            pl.BlockSpec((META_ROWS, NA_WIDTH), lambda b, i: (0, 0)),
            pl.BlockSpec((META_ROWS, NA_WIDTH), lambda b, i: (0, 0)),
            pl.BlockSpec((None, HEADS, NA_TQ, NA_KROWS * GRID_W), case),
            pl.BlockSpec((HEADS, 1, META_ROWS), lambda b, i: (0, 0, 0)),
            pl.BlockSpec((1, NA_WIDTH), lambda b, i: (0, 0)),
        ],
        out_specs=blk(cur),
        out_shape=jax.ShapeDtypeStruct((rows, NA_WIDTH), BF16),
        scratch_shapes=[pltpu.VMEM((NA_TQ, NA_WIDTH), F32)],
        compiler_params=_cparams("parallel", "arbitrary"),
        name="na_attn",
    )(qa, ka, ka, ka, va, va, va, km, vm, bias, mb, g)


def _na_meta_body(q_ref, k_ref, v_ref, mb_ref, g_ref, o_ref, out_ref):
    for h in range(HEADS):
        sl = slice(h * HEAD_DIM, (h + 1) * HEAD_DIM)
        s = _dot_nt(q_ref[:, sl], k_ref[:, sl]) + mb_ref[h]
        m = jnp.max(s, axis=-1, keepdims=True)
        p = jnp.exp2(s - m)
        l = jnp.sum(p, axis=-1, keepdims=True)
        out_ref[:, sl] = _dot(p.astype(BF16), v_ref[:, sl]) / l
    o_ref[...] = (_rms(out_ref[...]) * g_ref[...]).astype(BF16)


def _na_meta(qm, km, vm, mb, g):
    full = lambda s: pl.BlockSpec(s, lambda i: (0,) * len(s))
    return pl.pallas_call(
        _na_meta_body,
        grid=(1,),
        in_specs=[full((META_ROWS, NA_WIDTH))] * 3 + [full((HEADS, 1, META_ROWS)), full((1, NA_WIDTH))],
        out_specs=full((META_ROWS, NA_WIDTH)),
        out_shape=jax.ShapeDtypeStruct((META_ROWS, NA_WIDTH), BF16),
        scratch_shapes=[pltpu.VMEM((META_ROWS, NA_WIDTH), F32)],
        compiler_params=_cparams("arbitrary"),
        name="na_meta",
    )(qm, km, vm, mb, g)


def _outproj_body(x_ref, a_ref, b_ref, bg_ref, w_ref, fg_ref, h_ref, xn_ref):
    bn = (_rms(b_ref[...]) * bg_ref[...]).astype(BF16)
    mix = jnp.concatenate([a_ref[...], bn], axis=-1)
    h = x_ref[...] + _dot(mix, w_ref[...])
    h_ref[...] = h
    xn_ref[...] = (_rms(h) * fg_ref[...]).astype(BF16)


def _outproj(x, a, b, bg, w, fg, tm, x_map, a_map, b_map, rows):
    c = lambda i: (0, 0)
    return pl.pallas_call(
        _outproj_body,
        grid=(rows // tm,),
        in_specs=[
            pl.BlockSpec((tm, D_MODEL), x_map),
            pl.BlockSpec((tm, NA_WIDTH), a_map),
            pl.BlockSpec((tm, NA_WIDTH), b_map),
            pl.BlockSpec((1, NA_WIDTH), c),
            pl.BlockSpec((2 * NA_WIDTH, D_MODEL), c),
            pl.BlockSpec((1, D_MODEL), c),
        ],
        out_specs=[pl.BlockSpec((tm, D_MODEL), lambda i: (i, 0)),
                   pl.BlockSpec((tm, D_MODEL), lambda i: (i, 0))],
        out_shape=(jax.ShapeDtypeStruct((rows, D_MODEL), F32),
                   jax.ShapeDtypeStruct((rows, D_MODEL), BF16)),
        compiler_params=_cparams("parallel"),
        name="outproj",
    )(x, a, b, bg, w, fg)


def _ffn_body(xm_ref, xp_ref, xx_ref, xmeta_ref, h_ref, wg_ref, wu_ref, wd_ref, cw_ref, cb_ref,
              o_ref, xe_ref, *, tm, tpb):
    i = pl.program_id(0)
    j = pl.program_id(1)

    @pl.when(j == 0)
    def _():
        first = (i % tpb) == 0
        last = (i % tpb) == tpb - 1
        xe_ref[0:HALO, :] = jnp.where(first, xmeta_ref[...], xp_ref[...])
        xe_ref[HALO:HALO + tm, :] = xm_ref[...]
        xe_ref[HALO + tm:2 * HALO + tm, :] = jnp.where(last, jnp.zeros_like(xx_ref[...]), xx_ref[...])
        o_ref[...] = h_ref[...]

    g = _dot(xe_ref[...], wg_ref[...])
    u = _dot(xe_ref[HALO:HALO + tm, :], wu_ref[...])
    cw = cw_ref[...]
    gc = (cb_ref[...] + cw[0:1] * g[HALO - 1:HALO - 1 + tm]
          + cw[1:2] * g[HALO:HALO + tm] + cw[2:3] * g[HALO + 1:HALO + 1 + tm])
    act = gc * (1.0 / (1.0 + jnp.exp(-gc))) * u
    o_ref[...] += _dot(act.astype(BF16), wd_ref[...])


def _ffn(xn, xn_meta, h1, wg, wu, wd, cw, cb, nb, tm, tf):
    rows = xn.shape[0]
    nt = rows // tm
    tpb = nt // nb
    hb = tm // HALO
    last_hb = rows // HALO - 1
    return pl.pallas_call(
        functools.partial(_ffn_body, tm=tm, tpb=tpb),
        grid=(nt, D_FF // tf),
        in_specs=[
            pl.BlockSpec((tm, D_MODEL), lambda i, j: (i, 0)),
            pl.BlockSpec((HALO, D_MODEL), lambda i, j: (jnp.maximum(i * hb - 1, 0), 0)),
            pl.BlockSpec((HALO, D_MODEL), lambda i, j: (jnp.minimum((i + 1) * hb, last_hb), 0)),
            pl.BlockSpec((HALO, D_MODEL), lambda i, j: ((i // tpb) * (META_ROWS // HALO), 0)),
            pl.BlockSpec((tm, D_MODEL), lambda i, j: (i, 0)),
            pl.BlockSpec((D_MODEL, tf), lambda i, j: (0, j)),
            pl.BlockSpec((D_MODEL, tf), lambda i, j: (0, j)),
            pl.BlockSpec((tf, D_MODEL), lambda i, j: (j, 0)),
            pl.BlockSpec((3, tf), lambda i, j: (0, j)),
            pl.BlockSpec((1, tf), lambda i, j: (0, j)),
        ],
        out_specs=pl.BlockSpec((tm, D_MODEL), lambda i, j: (i, 0)),
        out_shape=jax.ShapeDtypeStruct((rows, D_MODEL), F32),
        scratch_shapes=[pltpu.VMEM((tm + 2 * HALO, D_MODEL), BF16)],
        compiler_params=_cparams("parallel", "arbitrary"),
        name="ffn",
    )(xn, xn, xn, xn_meta, h1, wg, wu, wd, cw, cb)


def _rope_tables(pos):
    inv = ROPE_THETA ** (-jnp.arange(0, ROPE_DIM, 2, dtype=F32) / ROPE_DIM)
    ang = pos.astype(F32)[:, None] * inv[None, :]
    cos, sin = jnp.cos(ang), jnp.sin(ang)
    z = jnp.zeros_like(cos)
    return (jnp.concatenate([cos, cos, z, z], axis=-1),
            jnp.concatenate([-sin, sin, z, z], axis=-1), cos.T, sin.T)


def _pad_lanes(v, width):
    return jnp.pad(v, ((0, 0), (0, width - v.shape[-1])))


def kernel(x, meta_tokens, mix_norm_g, w_in, na_q_g, na_k_g, na_rpb, na_meta_bias, mla_cq_g, mla_ckv_g,
           w_q_up, w_kv_up, mla_q_g, mla_k_g, na_out_g, mla_out_g, w_out, ffn_norm_g, w_gate, w_up,
           conv_w, conv_b, w_down):
    nb, seq, d = x.shape
    assert d == D_MODEL and w_in.shape[0] == 1, "one layer of width 2048"
    rows_grid = seq // GRID_W
    assert seq % max(IN_TM, UP_TM, ATT_TQ, FFN_TM) == 0 and rows_grid >= 4 * NA_QROWS

    w_in_p = _pad_lanes(w_in[0], 4 * IN_TN).astype(BF16)
    w4 = w_in_p.reshape(D_MODEL, 4, IN_TN).transpose(1, 0, 2)
    wqt = w_q_up[0].reshape(Q_RANK, HEADS, QK_DIM).transpose(1, 2, 0)
    wqt = jnp.pad(wqt, ((0, 0), (0, QK_PAD - QK_DIM), (0, 0))).astype(BF16)
    wkv = w_kv_up[0].reshape(KV_RANK, HEADS, 2 * HEAD_DIM)
    wk = wkv[:, :, :HEAD_DIM].reshape(KV_RANK, HEADS // 2, 2 * HEAD_DIM).transpose(1, 0, 2).astype(BF16)
    wvt = wkv[:, :, HEAD_DIM:].transpose(1, 2, 0).astype(BF16)
    wo = w_out[0].astype(BF16)
    wg, wu, wd = w_gate[0].astype(BF16), w_up[0].astype(BF16), w_down[0].astype(BF16)
    row = lambda v: v.reshape(1, -1)
    qg_pad, kg_pad = _pad_lanes(mla_q_g, QK_PAD), _pad_lanes(mla_k_g, QK_PAD)
    mb = jnp.pad(na_meta_bias[0] * LOG2E, ((0, 0), (0, META_ROWS - N_META)),
                 constant_values=NEG).reshape(HEADS, 1, META_ROWS)

    xr = x.reshape(nb * seq, D_MODEL)
    xm = jnp.pad(meta_tokens.astype(x.dtype), ((0, META_ROWS - N_META), (0, 0)))
    rope_r = _rope_tables(N_META + jnp.arange(seq))
    rope_m = _rope_tables(jnp.arange(META_ROWS))

    inproj = functools.partial(_inproj, g=row(mix_norm_g), w4=w4, qg=row(na_q_g), kg=row(na_k_g),
                               cqg=row(mla_cq_g), ckvg=row(mla_ckv_g))
    qa, ka, va, cq, ckv, kpe = inproj(xr, tm=IN_TM)
    qa_m, ka_m, va_m, cq_m, ckv_m, kpe_m = inproj(xm, tm=META_ROWS)

    up = functools.partial(_mla_up, wqt=wqt, wk=wk, wvt=wvt, qg=qg_pad, kg=kg_pad)
    qt, kk, vt = up(cq, ckv, kpe, *rope_r, nb=nb, tm=UP_TM)
    qt_m, kk_m, vt_m = up(cq_m, ckv_m, kpe_m, *rope_m, nb=1, tm=META_ROWS)

    bias = _na_bias(na_rpb[0] , rows_grid)
    a_n = _na_attn(qa, ka, va, ka_m, va_m, bias, mb, row(na_out_g), nb)
    a_n_m = _na_meta(qa_m, ka_m, va_m, mb, row(na_out_g))

    out_b = _mla_attn(qt, kk, vt, kk_m, vt_m, ATT_TQ, shared_q=False)
    out_b_m = _mla_attn(qt_m, kk, vt, kk_m, vt_m, META_ROWS, shared_q=True)

    op = functools.partial(_outproj, bg=row(mla_out_g), w=wo, fg=row(ffn_norm_g))
    ident = lambda i: (i, 0)
    zero = lambda i: (0, 0)
    h1, xn2 = op(xr, a_n, out_b.reshape(nb * seq, NA_WIDTH), tm=OUT_TM,
                 x_map=ident, a_map=ident, b_map=ident, rows=nb * seq)
    _, xn2_m = op(xm, a_n_m, out_b_m.reshape(nb * META_ROWS, NA_WIDTH), tm=META_ROWS,
                  x_map=zero, a_map=zero, b_map=ident, rows=nb * META_ROWS)

    out = _ffn(xn2, xn2_m, h1, wg, wu, wd, conv_w[0], row(conv_b), nb, FFN_TM, FFN_TF)
    return out.reshape(nb, seq, D_MODEL)
```

```python
import functools

import jax
import jax.numpy as jnp
import numpy as np
from jax import lax
from jax.experimental import pallas as pl
from jax.experimental.pallas import tpu as pltpu

F32 = jnp.float32
BF16 = jnp.bfloat16

LANES = 128
BF16_SUBLANES = 16
VMEM_LIMIT_BYTES = 56 * 1024 * 1024

D_MODEL = 2048
GRID_W = 64
N_META = 16
EPS = 1e-6
NEG = -1e30
LOG2E = 1.4426950408889634
HEADS = 8
HEAD_DIM = 128
NA_WIDTH = HEADS * HEAD_DIM
NA_WIN_ROWS = 8
NA_WIN_COLS = 16
Q_RANK = 512
KV_RANK = 256
ROPE_DIM = 64
QK_DIM = HEAD_DIM + ROPE_DIM
QK_PAD = 2 * LANES
ROPE_THETA = 10000.0
D_FF = 5632
META_ROWS = LANES

IN_TM = 1024
IN_TN = 1024
UP_TM = 512
ATT_TQ = 1024
ATT_SUB = 512
ATT_GROUP = 4
NA_QROWS = 4
NA_TQ = NA_QROWS * GRID_W
NA_KROWS = 3 * NA_QROWS
OUT_TM = 256
FFN_TM = 512
FFN_TF = 512
HALO = BF16_SUBLANES


def _cparams(*sem):
    return pltpu.CompilerParams(dimension_semantics=sem, vmem_limit_bytes=VMEM_LIMIT_BYTES)


def _rms(v, n=None):
    if n is None:
        ms = jnp.mean(v * v, axis=-1, keepdims=True)
    else:
        ms = jnp.sum(v * v, axis=-1, keepdims=True) * (1.0 / n)
    return v * lax.rsqrt(ms + EPS)


def _dot(a, b):
    return jnp.dot(a, b, preferred_element_type=F32)


def _dot_nt(a, b):
    return lax.dot_general(a, b, (((1,), (1,)), ((), ())), preferred_element_type=F32)


def _inproj_body(x_ref, g_ref, w_ref, qg_ref, kg_ref, cqg_ref, ckvg_ref,
                 qa_ref, ka_ref, va_ref, cq_ref, ckv_ref, kpe_ref, xn_ref, *, q_scale):
    j = pl.program_id(1)

    @pl.when(j == 0)
    def _():
        xn_ref[...] = (_rms(x_ref[...]) * g_ref[...]).astype(BF16)

    def project(lo, hi):
        return _dot(xn_ref[...], w_ref[:, lo:hi])

    def head_norm(gain_ref, out_ref, post):
        for t in range(HEADS // 2):
            y = project(2 * t * HEAD_DIM, 2 * (t + 1) * HEAD_DIM)
            for u in range(2):
                sl = slice((2 * t + u) * HEAD_DIM, (2 * t + u + 1) * HEAD_DIM)
                yh = y[:, u * HEAD_DIM:(u + 1) * HEAD_DIM]
                out_ref[:, sl] = (_rms(yh) * gain_ref[...] * post).astype(BF16)

    @pl.when(j == 0)
    def _():
        head_norm(qg_ref, qa_ref, q_scale)

    @pl.when(j == 1)
    def _():
        head_norm(kg_ref, ka_ref, 1.0)

    @pl.when(j == 2)
    def _():
        va_ref[...] = project(0, IN_TN).astype(BF16)

    @pl.when(j == 3)
    def _():
        cq_ref[...] = (_rms(project(0, Q_RANK)) * cqg_ref[...]).astype(BF16)
        y = project(Q_RANK, IN_TN)
        ckv_ref[...] = (_rms(y[:, :KV_RANK]) * ckvg_ref[...]).astype(BF16)
        kpe_ref[...] = y[:, KV_RANK:KV_RANK + LANES]


def _inproj(x, g, w4, qg, kg, cqg, ckvg, tm):
    rows = x.shape[0]
    row = lambda i, j: (i, 0)
    const = lambda i, j: (0, 0)
    outs = (
        jax.ShapeDtypeStruct((rows, NA_WIDTH), BF16),
        jax.ShapeDtypeStruct((rows, NA_WIDTH), BF16),
        jax.ShapeDtypeStruct((rows, NA_WIDTH), BF16),
        jax.ShapeDtypeStruct((rows, Q_RANK), BF16),
        jax.ShapeDtypeStruct((rows, KV_RANK), BF16),
        jax.ShapeDtypeStruct((rows, LANES), F32),
    )
    return pl.pallas_call(
        functools.partial(_inproj_body, q_scale=HEAD_DIM ** -0.5 * LOG2E),
        grid=(rows // tm, 4),
        in_specs=[
            pl.BlockSpec((tm, D_MODEL), row),
            pl.BlockSpec((1, D_MODEL), const),
            pl.BlockSpec((None, D_MODEL, IN_TN), lambda i, j: (j, 0, 0)),
            pl.BlockSpec((1, HEAD_DIM), const),
            pl.BlockSpec((1, HEAD_DIM), const),
            pl.BlockSpec((1, Q_RANK), const),
            pl.BlockSpec((1, KV_RANK), const),
        ],
        out_specs=[
            pl.BlockSpec((tm, NA_WIDTH), row),
            pl.BlockSpec((tm, NA_WIDTH), row),
            pl.BlockSpec((tm, NA_WIDTH), row),
            pl.BlockSpec((tm, Q_RANK), row),
            pl.BlockSpec((tm, KV_RANK), row),
            pl.BlockSpec((tm, LANES), row),
        ],
        out_shape=outs,
        scratch_shapes=[pltpu.VMEM((tm, D_MODEL), BF16)],
        compiler_params=_cparams("parallel", "arbitrary"),
        name="inproj",
    )(x, g, w4, qg, kg, cqg, ckvg)


def _mla_up_body(cq_ref, ckv_ref, kpe_ref, cos_ref, sin_ref, cost_ref, sint_ref,
                 wqt_ref, wk_ref, wvt_ref, qgt_ref, kg_ref, qt_ref, k_ref, vt_ref, *, q_scale):
    half = ROPE_DIM // 2
    ckv = ckv_ref[...]
    cqt = cq_ref[...].astype(F32).T.astype(BF16)
    ckvt = ckv.astype(F32).T.astype(BF16)
    qgt = qgt_ref[...]
    cost = cost_ref[...]
    sint = sint_ref[...]
    cosf = cos_ref[...]
    sinf = sin_ref[...]
    kg = kg_ref[...]

    kpe = kpe_ref[...]
    kpe_ss = jnp.sum(kpe * kpe, axis=-1, keepdims=True)
    r = kpe * kg[:, LANES:]
    kpe_rot = r * cosf + (pltpu.roll(r, half, 1) + pltpu.roll(r, LANES - half, 1)) * sinf

    for h in range(HEADS):
        q = _dot(wqt_ref[h], cqt)
        ms = jnp.sum(q * q, axis=0, keepdims=True) * (1.0 / QK_DIM)
        qn = q * (lax.rsqrt(ms + EPS) * q_scale) * qgt
        x1 = qn[LANES:LANES + half]
        x2 = qn[LANES + half:LANES + 2 * half]
        qt_ref[h, 0:LANES, :] = qn[:LANES].astype(BF16)
        qt_ref[h, LANES:LANES + half, :] = (x1 * cost - x2 * sint).astype(BF16)
        qt_ref[h, LANES + half:LANES + 2 * half, :] = (x2 * cost + x1 * sint).astype(BF16)
        qt_ref[h, LANES + 2 * half:, :] = jnp.zeros((QK_PAD - QK_DIM, q.shape[1]), BF16)
        vt_ref[h] = _dot(wvt_ref[h], ckvt).astype(BF16)

    for t in range(HEADS // 2):
        kn2 = _dot(ckv, wk_ref[t])
        for u in range(2):
            kn = kn2[:, u * LANES:(u + 1) * LANES]
            ms = (jnp.sum(kn * kn, axis=-1, keepdims=True) + kpe_ss) * (1.0 / QK_DIM)
            rn = lax.rsqrt(ms + EPS)
            k_ref[2 * t + u] = jnp.concatenate([kn * rn * kg[:, :LANES], kpe_rot * rn], axis=-1).astype(BF16)


def _mla_up(cq, ckv, kpe, cosf, sinf, cost, sint, wqt, wk, wvt, qg, kg, nb, tm):
    rows = cq.shape[0]
    nt = rows // nb // tm
    row = lambda b, i: (b * nt + i, 0)
    pos = lambda b, i: (i, 0)
    post = lambda b, i: (0, i)
    c2 = lambda b, i: (0, 0)
    c3 = lambda b, i: (0, 0, 0)
    qgt = jnp.broadcast_to(qg.reshape(QK_PAD, 1), (QK_PAD, tm))
    outs = (
        jax.ShapeDtypeStruct((nb, HEADS, QK_PAD, nt * tm), BF16),
        jax.ShapeDtypeStruct((nb, HEADS, nt, tm, QK_PAD), BF16),
        jax.ShapeDtypeStruct((nb, HEADS, nt, HEAD_DIM, tm), BF16),
    )
    return pl.pallas_call(
        functools.partial(_mla_up_body, q_scale=QK_DIM ** -0.5 * LOG2E),
        grid=(nb, nt),
        in_specs=[
            pl.BlockSpec((tm, Q_RANK), row),
            pl.BlockSpec((tm, KV_RANK), row),
            pl.BlockSpec((tm, LANES), row),
            pl.BlockSpec((tm, LANES), pos),
            pl.BlockSpec((tm, LANES), pos),
            pl.BlockSpec((ROPE_DIM // 2, tm), post),
            pl.BlockSpec((ROPE_DIM // 2, tm), post),
            pl.BlockSpec((HEADS, QK_PAD, Q_RANK), c3),
            pl.BlockSpec((HEADS // 2, KV_RANK, 2 * HEAD_DIM), c3),
            pl.BlockSpec((HEADS, HEAD_DIM, KV_RANK), c3),
            pl.BlockSpec((QK_PAD, tm), c2),
            pl.BlockSpec((1, QK_PAD), c2),
        ],
        out_specs=[
            pl.BlockSpec((None, HEADS, QK_PAD, tm), lambda b, i: (b, 0, 0, i)),
            pl.BlockSpec((None, HEADS, None, tm, QK_PAD), lambda b, i: (b, 0, i, 0, 0)),
            pl.BlockSpec((None, HEADS, None, HEAD_DIM, tm), lambda b, i: (b, 0, i, 0, 0)),
        ],
        out_shape=outs,
        compiler_params=_cparams("parallel", "parallel"),
        name="mla_up",
    )(cq, ckv, kpe, cosf, sinf, cost, sint, wqt, wk, wvt, qgt, kg)


def _mla_attn_body(qt_ref, qtn_ref, k_ref, vt_ref, km_ref, vmt_ref, o_ref, acc_ref, s_ref, gm_ref,
                   *, nk, tk, carry_over):
    qt = qt_ref[...]
    ng = nk // ATT_GROUP

    def scores(g, half, q=None):
        q = qt if q is None else q
        gm = None
        for u in range(ATT_GROUP):
            for j in range(tk // ATT_SUB):
                rows = pl.ds(j * ATT_SUB, ATT_SUB)
                s = _dot(k_ref[g * ATT_GROUP + u, rows, :], q)
                s_ref[half * ATT_GROUP + u, rows, :] = s
                m = jnp.max(s, axis=0, keepdims=True)
                gm = m if gm is None else jnp.maximum(gm, m)
        return gm

    def attend(g, half, gm, m_prev, l_prev):
        m_new = jnp.maximum(m_prev, gm)
        alpha = jnp.exp2(m_prev - m_new)
        l_new = alpha * l_prev
        pv = None
        for u in range(ATT_GROUP):
            p = jnp.exp2(s_ref[half * ATT_GROUP + u] - m_new)
            l_new = l_new + jnp.sum(p, axis=0, keepdims=True)
            d = _dot(vt_ref[g * ATT_GROUP + u], p.astype(BF16))
            pv = d if pv is None else pv + d
        acc_ref[...] = alpha * acc_ref[...] + pv
        return m_new, l_new

    def two_stages(i, carry):
        m, l, gm0 = carry
        gm1 = scores(2 * i + 1, 1)
        m, l = attend(2 * i, 0, gm0, m, l)
        gm0 = scores(2 * i + 2, 0)
        m, l = attend(2 * i + 1, 1, gm1, m, l)
        return m, l, gm0

    if carry_over:
        @pl.when(pl.program_id(2) == 0)
        def _():
            gm_ref[...] = scores(0, 0)
        gm_first = gm_ref[...]
    else:
        gm_first = scores(0, 0)

    s = _dot(km_ref[...], qt)
    key = lax.broadcasted_iota(jnp.int32, s.shape, 0)
    s = jnp.where(key < N_META, s, NEG)
    m0 = jnp.max(s, axis=0, keepdims=True)
    p = jnp.exp2(s - m0)
    l0 = jnp.sum(p, axis=0, keepdims=True)
    acc_ref[...] = _dot(vmt_ref[...], p.astype(BF16))

    m, l, gm0 = lax.fori_loop(0, ng // 2 - 1, two_stages, (m0, l0, gm_first))
    gm1 = scores(ng - 1, 1)
    m, l = attend(ng - 2, 0, gm0, m, l)
    if carry_over:
        gm_ref[...] = scores(0, 0, qtn_ref[...])
    m, l = attend(ng - 1, 1, gm1, m, l)
    o_ref[...] = (acc_ref[...] / l).T


def _mla_attn(qt, k, vt, km, vmt, tq, shared_q):
    nb, _, nk, tk, _ = k.shape
    nq = qt.shape[3] // tq
    qb = (lambda b: 0) if shared_q else (lambda b: b)
    return pl.pallas_call(
        functools.partial(_mla_attn_body, nk=nk, tk=tk, carry_over=nq > 1),
        grid=(nb, HEADS, nq),
        in_specs=[
            pl.BlockSpec((None, None, QK_PAD, tq), lambda b, h, i: (qb(b), h, 0, i)),
            pl.BlockSpec((None, None, QK_PAD, tq), lambda b, h, i: (qb(b), h, 0, jnp.minimum(i + 1, nq - 1))),
            pl.BlockSpec((None, None, nk, tk, QK_PAD), lambda b, h, i: (b, h, 0, 0, 0),
                         pipeline_mode=pl.Buffered(1)),
            pl.BlockSpec((None, None, nk, HEAD_DIM, tk), lambda b, h, i: (b, h, 0, 0, 0),
                         pipeline_mode=pl.Buffered(1)),
            pl.BlockSpec((None, None, None, META_ROWS, QK_PAD), lambda b, h, i: (0, h, 0, 0, 0)),
            pl.BlockSpec((None, None, None, HEAD_DIM, META_ROWS), lambda b, h, i: (0, h, 0, 0, 0)),
        ],
        out_specs=pl.BlockSpec((None, tq, HEAD_DIM), lambda b, h, i: (b, i, h)),
        out_shape=jax.ShapeDtypeStruct((nb, nq * tq, NA_WIDTH), F32),
        scratch_shapes=[pltpu.VMEM((HEAD_DIM, tq), F32), pltpu.VMEM((2 * ATT_GROUP, tk, tq), F32),
                        pltpu.VMEM((1, tq), F32)],
        compiler_params=_cparams("parallel", "parallel", "arbitrary"),
        name="mla_attn",
    )(qt, qt, k, vt, km, vmt)


def _na_bias_body(rpb_ref, o_ref, t_ref, *, rows):
    h = pl.program_id(0)
    n_dr = 2 * NA_WIN_ROWS - 1
    n_dc = 2 * NA_WIN_COLS - 1
    shape = (GRID_W, LANES)
    qc = lax.broadcasted_iota(jnp.int32, shape, 0)
    lane = lax.broadcasted_iota(jnp.int32, shape, 1)
    left = lane < GRID_W
    kc = jnp.where(left, lane, lane - GRID_W)
    cs = jnp.clip(qc - NA_WIN_COLS // 2, 0, GRID_W - NA_WIN_COLS)
    col_ok = (kc >= cs) & (kc < cs + NA_WIN_COLS)
    dc = kc - qc + NA_WIN_COLS - 1

    def build(dr, carry):
        def pick(j, acc):
            return jnp.where(dc == j, rpb_ref[h, dr, j], acc)

        tile = lax.fori_loop(0, n_dc, pick, jnp.zeros(shape, F32))
        t_ref[dr] = jnp.where(col_ok, tile * LOG2E, NEG)
        return carry

    lax.fori_loop(0, n_dr, build, 0)

    neg = jnp.full(shape, NEG, F32)
    for c, r0 in enumerate((0, 2 * NA_QROWS, rows - NA_QROWS)):
        for qr in range(NA_QROWS):
            abs_qr = r0 + qr
            rs = min(max(abs_qr - NA_WIN_ROWS // 2, 0), rows - NA_WIN_ROWS)
            for a in range(NA_KROWS // 2):
                halves = []
                for kr in (r0 - NA_QROWS + 2 * a, r0 - NA_QROWS + 2 * a + 1):
                    in_window = rs <= kr < rs + NA_WIN_ROWS
                    halves.append(t_ref[kr - abs_qr + NA_WIN_ROWS - 1] if in_window else neg)
                o_ref[c, qr * GRID_W:(qr + 1) * GRID_W, a * LANES:(a + 1) * LANES] = (
                    jnp.where(left, halves[0], halves[1]))


def _na_bias(rpb, rows):
    return pl.pallas_call(
        functools.partial(_na_bias_body, rows=rows),
        grid=(HEADS,),
        in_specs=[pl.BlockSpec(memory_space=pltpu.SMEM)],
        out_specs=pl.BlockSpec((3, None, NA_TQ, NA_KROWS * GRID_W), lambda h: (0, h, 0, 0)),
        out_shape=jax.ShapeDtypeStruct((3, HEADS, NA_TQ, NA_KROWS * GRID_W), F32),
        scratch_shapes=[pltpu.VMEM((2 * NA_WIN_ROWS - 1, GRID_W, LANES), F32)],
        compiler_params=_cparams("parallel"),
        name="na_bias",
    )(rpb)


def _na_body(q_ref, kp_ref, kc_ref, kn_ref, vp_ref, vc_ref, vn_ref, km_ref, vm_ref,
             bias_ref, mb_ref, g_ref, o_ref, out_ref):
    for h in range(HEADS):
        sl = slice(h * HEAD_DIM, (h + 1) * HEAD_DIM)
        q = q_ref[:, sl]
        s = jnp.concatenate(
            [_dot_nt(q, kp_ref[:, sl]), _dot_nt(q, kc_ref[:, sl]), _dot_nt(q, kn_ref[:, sl])],
            axis=-1) + bias_ref[h]
        sm = _dot_nt(q, km_ref[:, sl]) + mb_ref[h]
        m = jnp.maximum(jnp.max(s, axis=-1, keepdims=True), jnp.max(sm, axis=-1, keepdims=True))
        p = jnp.exp2(s - m)
        pm = jnp.exp2(sm - m)
        l = jnp.sum(pm, axis=-1, keepdims=True) + jnp.sum(p, axis=-1, keepdims=True)
        pb = p.astype(BF16)
        o = (_dot(pm.astype(BF16), vm_ref[:, sl])
             + _dot(pb[:, :NA_TQ], vp_ref[:, sl])
             + _dot(pb[:, NA_TQ:2 * NA_TQ], vc_ref[:, sl])
             + _dot(pb[:, 2 * NA_TQ:], vn_ref[:, sl]))
        out_ref[:, sl] = o / l
    o_ref[...] = (_rms(out_ref[...]) * g_ref[...]).astype(BF16)


def _na_attn(qa, ka, va, km, vm, bias, mb, g, nb):
    rows = qa.shape[0]
    n = rows // nb // NA_TQ
    cur = lambda b, i: (b * n + i, 0)
    prev = lambda b, i: (b * n + jnp.maximum(i - 1, 0), 0)
    nxt = lambda b, i: (b * n + jnp.minimum(i + 1, n - 1), 0)
    blk = lambda f: pl.BlockSpec((NA_TQ, NA_WIDTH), f)
    case = lambda b, i: (jnp.where(i == 0, 0, jnp.where(i == n - 1, 2, 1)), 0, 0, 0)
    return pl.pallas_call(
        _na_body,
        grid=(nb, n),
        in_specs=[
            blk(cur), blk(prev), blk(cur), blk(nxt), blk(prev), blk(cur), blk(nxt),
            pl.BlockSpec((META_ROWS, NA_WIDTH), lambda b, i: (0, 0)),
            pl.BlockSpec((META_ROWS, NA_WIDTH), lambda b, i: (0, 0)),
            pl.BlockSpec((None, HEADS, NA_TQ, NA_KROWS * GRID_W), case),
            pl.BlockSpec((HEADS, 1, META_ROWS), lambda b, i: (0, 0, 0)),
            pl.BlockSpec((1, NA_WIDTH), lambda b, i: (0, 0)),
        ],
        out_specs=blk(cur),
        out_shape=jax.ShapeDtypeStruct((rows, NA_WIDTH), BF16),
        scratch_shapes=[pltpu.VMEM((NA_TQ, NA_WIDTH), F32)],
        compiler_params=_cparams("parallel", "arbitrary"),
        name="na_attn",
    )(qa, ka, ka, ka, va, va, va, km, vm, bias, mb, g)


def _na_meta_body(q_ref, k_ref, v_ref, mb_ref, g_ref, o_ref, out_ref):
    for h in range(HEADS):
        sl = slice(h * HEAD_DIM, (h + 1) * HEAD_DIM)
        s = _dot_nt(q_ref[:, sl], k_ref[:, sl]) + mb_ref[h]
        m = jnp.max(s, axis=-1, keepdims=True)
        p = jnp.exp2(s - m)
        l = jnp.sum(p, axis=-1, keepdims=True)
        out_ref[:, sl] = _dot(p.astype(BF16), v_ref[:, sl]) / l
    o_ref[...] = (_rms(out_ref[...]) * g_ref[...]).astype(BF16)


def _na_meta(qm, km, vm, mb, g):
    full = lambda s: pl.BlockSpec(s, lambda i: (0,) * len(s))
    return pl.pallas_call(
        _na_meta_body,
        grid=(1,),
        in_specs=[full((META_ROWS, NA_WIDTH))] * 3 + [full((HEADS, 1, META_ROWS)), full((1, NA_WIDTH))],
        out_specs=full((META_ROWS, NA_WIDTH)),
        out_shape=jax.ShapeDtypeStruct((META_ROWS, NA_WIDTH), BF16),
        scratch_shapes=[pltpu.VMEM((META_ROWS, NA_WIDTH), F32)],
        compiler_params=_cparams("arbitrary"),
        name="na_meta",
    )(qm, km, vm, mb, g)


def _outproj_body(x_ref, a_ref, b_ref, bg_ref, w_ref, fg_ref, h_ref, xn_ref):
    bn = (_rms(b_ref[...]) * bg_ref[...]).astype(BF16)
    mix = jnp.concatenate([a_ref[...], bn], axis=-1)
    h = x_ref[...] + _dot(mix, w_ref[...])
    h_ref[...] = h
    xn_ref[...] = (_rms(h) * fg_ref[...]).astype(BF16)


def _outproj(x, a, b, bg, w, fg, tm, x_map, a_map, b_map, rows):
    c = lambda i: (0, 0)
    return pl.pallas_call(
        _outproj_body,
        grid=(rows // tm,),
        in_specs=[
            pl.BlockSpec((tm, D_MODEL), x_map),
            pl.BlockSpec((tm, NA_WIDTH), a_map),
            pl.BlockSpec((tm, NA_WIDTH), b_map),
            pl.BlockSpec((1, NA_WIDTH), c),
            pl.BlockSpec((2 * NA_WIDTH, D_MODEL), c),
            pl.BlockSpec((1, D_MODEL), c),
        ],
        out_specs=[pl.BlockSpec((tm, D_MODEL), lambda i: (i, 0)),
                   pl.BlockSpec((tm, D_MODEL), lambda i: (i, 0))],
        out_shape=(jax.ShapeDtypeStruct((rows, D_MODEL), F32),
                   jax.ShapeDtypeStruct((rows, D_MODEL), BF16)),
        compiler_params=_cparams("parallel"),
        name="outproj",
    )(x, a, b, bg, w, fg)


def _ffn_body(xm_ref, xp_ref, xx_ref, xmeta_ref, h_ref, wg_ref, wu_ref, wd_ref, cw_ref, cb_ref,
              o_ref, xe_ref, *, tm, tpb):
    i = pl.program_id(0)
    j = pl.program_id(1)

    @pl.when(j == 0)
    def _():
        first = (i % tpb) == 0
        last = (i % tpb) == tpb - 1
        xe_ref[0:HALO, :] = jnp.where(first, xmeta_ref[...], xp_ref[...])
        xe_ref[HALO:HALO + tm, :] = xm_ref[...]
        xe_ref[HALO + tm:2 * HALO + tm, :] = jnp.where(last, jnp.zeros_like(xx_ref[...]), xx_ref[...])
        o_ref[...] = h_ref[...]

    g = _dot(xe_ref[...], wg_ref[...])
    u = _dot(xe_ref[HALO:HALO + tm, :], wu_ref[...])
    cw = cw_ref[...]
    gc = (cb_ref[...] + cw[0:1] * g[HALO - 1:HALO - 1 + tm]
          + cw[1:2] * g[HALO:HALO + tm] + cw[2:3] * g[HALO + 1:HALO + 1 + tm])
    act = gc * (1.0 / (1.0 + jnp.exp(-gc))) * u
    o_ref[...] += _dot(act.astype(BF16), wd_ref[...])


def _ffn(xn, xn_meta, h1, wg, wu, wd, cw, cb, nb, tm, tf):
    rows = xn.shape[0]
    nt = rows // tm
    tpb = nt // nb
    hb = tm // HALO
    last_hb = rows // HALO - 1
    return pl.pallas_call(
        functools.partial(_ffn_body, tm=tm, tpb=tpb),
        grid=(nt, D_FF // tf),
        in_specs=[
            pl.BlockSpec((tm, D_MODEL), lambda i, j: (i, 0)),
            pl.BlockSpec((HALO, D_MODEL), lambda i, j: (jnp.maximum(i * hb - 1, 0), 0)),
            pl.BlockSpec((HALO, D_MODEL), lambda i, j: (jnp.minimum((i + 1) * hb, last_hb), 0)),
            pl.BlockSpec((HALO, D_MODEL), lambda i, j: ((i // tpb) * (META_ROWS // HALO), 0)),
            pl.BlockSpec((tm, D_MODEL), lambda i, j: (i, 0)),
            pl.BlockSpec((D_MODEL, tf), lambda i, j: (0, j)),
            pl.BlockSpec((D_MODEL, tf), lambda i, j: (0, j)),
            pl.BlockSpec((tf, D_MODEL), lambda i, j: (j, 0)),
            pl.BlockSpec((3, tf), lambda i, j: (0, j)),
            pl.BlockSpec((1, tf), lambda i, j: (0, j)),
        ],
        out_specs=pl.BlockSpec((tm, D_MODEL), lambda i, j: (i, 0)),
        out_shape=jax.ShapeDtypeStruct((rows, D_MODEL), F32),
        scratch_shapes=[pltpu.VMEM((tm + 2 * HALO, D_MODEL), BF16)],
        compiler_params=_cparams("parallel", "arbitrary"),
        name="ffn",
    )(xn, xn, xn, xn_meta, h1, wg, wu, wd, cw, cb)


def _rope_tables(pos):
    inv = ROPE_THETA ** (-jnp.arange(0, ROPE_DIM, 2, dtype=F32) / ROPE_DIM)
    ang = pos.astype(F32)[:, None] * inv[None, :]
    cos, sin = jnp.cos(ang), jnp.sin(ang)
    z = jnp.zeros_like(cos)
    return (jnp.concatenate([cos, cos, z, z], axis=-1),
            jnp.concatenate([-sin, sin, z, z], axis=-1), cos.T, sin.T)


def _pad_lanes(v, width):
    return jnp.pad(v, ((0, 0), (0, width - v.shape[-1])))


def kernel(x, meta_tokens, mix_norm_g, w_in, na_q_g, na_k_g, na_rpb, na_meta_bias, mla_cq_g, mla_ckv_g,
           w_q_up, w_kv_up, mla_q_g, mla_k_g, na_out_g, mla_out_g, w_out, ffn_norm_g, w_gate, w_up,
           conv_w, conv_b, w_down):
    nb, seq, d = x.shape
    assert d == D_MODEL and w_in.shape[0] == 1, "one layer of width 2048"
    rows_grid = seq // GRID_W
    assert seq % max(IN_TM, UP_TM, ATT_TQ, FFN_TM) == 0 and rows_grid >= 4 * NA_QROWS

    w_in_p = _pad_lanes(w_in[0], 4 * IN_TN).astype(BF16)
    w4 = w_in_p.reshape(D_MODEL, 4, IN_TN).transpose(1, 0, 2)
    wqt = w_q_up[0].reshape(Q_RANK, HEADS, QK_DIM).transpose(1, 2, 0)
    wqt = jnp.pad(wqt, ((0, 0), (0, QK_PAD - QK_DIM), (0, 0))).astype(BF16)
    wkv = w_kv_up[0].reshape(KV_RANK, HEADS, 2 * HEAD_DIM)
    wk = wkv[:, :, :HEAD_DIM].reshape(KV_RANK, HEADS // 2, 2 * HEAD_DIM).transpose(1, 0, 2).astype(BF16)
    wvt = wkv[:, :, HEAD_DIM:].transpose(1, 2, 0).astype(BF16)
    wo = w_out[0].astype(BF16)
    wg, wu, wd = w_gate[0].astype(BF16), w_up[0].astype(BF16), w_down[0].astype(BF16)
    row = lambda v: v.reshape(1, -1)
    qg_pad, kg_pad = _pad_lanes(mla_q_g, QK_PAD), _pad_lanes(mla_k_g, QK_PAD)
    mb = jnp.pad(na_meta_bias[0] * LOG2E, ((0, 0), (0, META_ROWS - N_META)),
                 constant_values=NEG).reshape(HEADS, 1, META_ROWS)

    xr = x.reshape(nb * seq, D_MODEL)
    xm = jnp.pad(meta_tokens.astype(x.dtype), ((0, META_ROWS - N_META), (0, 0)))
    rope_r = _rope_tables(N_META + jnp.arange(seq))
    rope_m = _rope_tables(jnp.arange(META_ROWS))

    inproj = functools.partial(_inproj, g=row(mix_norm_g), w4=w4, qg=row(na_q_g), kg=row(na_k_g),
                               cqg=row(mla_cq_g), ckvg=row(mla_ckv_g))
    qa, ka, va, cq, ckv, kpe = inproj(xr, tm=IN_TM)
    qa_m, ka_m, va_m, cq_m, ckv_m, kpe_m = inproj(xm, tm=META_ROWS)

    up = functools.partial(_mla_up, wqt=wqt, wk=wk, wvt=wvt, qg=qg_pad, kg=kg_pad)
    qt, kk, vt = up(cq, ckv, kpe, *rope_r, nb=nb, tm=UP_TM)
    qt_m, kk_m, vt_m = up(cq_m, ckv_m, kpe_m, *rope_m, nb=1, tm=META_ROWS)

    bias = _na_bias(na_rpb[0] , rows_grid)
    a_n = _na_attn(qa, ka, va, ka_m, va_m, bias, mb, row(na_out_g), nb)
    a_n_m = _na_meta(qa_m, ka_m, va_m, mb, row(na_out_g))

    out_b = _mla_attn(qt, kk, vt, kk_m, vt_m, ATT_TQ, shared_q=False)
    out_b_m = _mla_attn(qt_m, kk, vt, kk_m, vt_m, META_ROWS, shared_q=True)

    op = functools.partial(_outproj, bg=row(mla_out_g), w=wo, fg=row(ffn_norm_g))
    ident = lambda i: (i, 0)
    zero = lambda i: (0, 0)
    h1, xn2 = op(xr, a_n, out_b.reshape(nb * seq, NA_WIDTH), tm=OUT_TM,
                 x_map=ident, a_map=ident, b_map=ident, rows=nb * seq)
    _, xn2_m = op(xm, a_n_m, out_b_m.reshape(nb * META_ROWS, NA_WIDTH), tm=META_ROWS,
                  x_map=zero, a_map=zero, b_map=ident, rows=nb * META_ROWS)

    out = _ffn(xn2, xn2_m, h1, wg, wu, wd, conv_w[0], row(conv_b), nb, FFN_TM, FFN_TF)
    return out.reshape(nb, seq, D_MODEL)
```

```python
import functools

import jax
import jax.numpy as jnp
import numpy as np
from jax import lax
from jax.experimental import pallas as pl
from jax.experimental.pallas import tpu as pltpu

F32 = jnp.float32
BF16 = jnp.bfloat16

LANES = 128
BF16_SUBLANES = 16
VMEM_LIMIT_BYTES = 56 * 1024 * 1024

D_MODEL = 2048
GRID_W = 64
N_META = 16
EPS = 1e-6
NEG = -1e30
LOG2E = 1.4426950408889634
HEADS = 8
HEAD_DIM = 128
NA_WIDTH = HEADS * HEAD_DIM
NA_WIN_ROWS = 8
NA_WIN_COLS = 16
Q_RANK = 512
KV_RANK = 256
ROPE_DIM = 64
QK_DIM = HEAD_DIM + ROPE_DIM
QK_PAD = 2 * LANES
ROPE_THETA = 10000.0
D_FF = 5632
META_ROWS = LANES

IN_TM = 1024
IN_TN = 1024
UP_TM = 512
ATT_TQ = 1024
ATT_SUB = 512
ATT_GROUP = 4
NA_QROWS = 4
NA_TQ = NA_QROWS * GRID_W
NA_KROWS = 3 * NA_QROWS
OUT_TM = 256
FFN_TM = 512
FFN_TF = 512
HALO = BF16_SUBLANES


def _cparams(*sem):
    return pltpu.CompilerParams(dimension_semantics=sem, vmem_limit_bytes=VMEM_LIMIT_BYTES)


def _rms(v, n=None):
    if n is None:
        ms = jnp.mean(v * v, axis=-1, keepdims=True)
    else:
        ms = jnp.sum(v * v, axis=-1, keepdims=True) * (1.0 / n)
    return v * lax.rsqrt(ms + EPS)


def _dot(a, b):
    return jnp.dot(a, b, preferred_element_type=F32)


def _inproj_body(x_ref, g_ref, w_ref, qg_ref, kg_ref, cqg_ref, ckvg_ref,
                 qa_ref, ka_ref, va_ref, cq_ref, ckv_ref, kpe_ref, xn_ref, *, q_scale):
    j = pl.program_id(1)

    @pl.when(j == 0)
    def _():
        xn_ref[...] = (_rms(x_ref[...]) * g_ref[...]).astype(BF16)

    def project(lo, hi):
        return _dot(xn_ref[...], w_ref[:, lo:hi])

    def head_norm(gain_ref, out_ref, post, transposed):
        for t in range(HEADS // 2):
            y = project(2 * t * HEAD_DIM, 2 * (t + 1) * HEAD_DIM)
            for u in range(2):
                sl = slice((2 * t + u) * HEAD_DIM, (2 * t + u + 1) * HEAD_DIM)
                yh = _rms(y[:, u * HEAD_DIM:(u + 1) * HEAD_DIM]) * gain_ref[...] * post
                if transposed:
                    out_ref[sl, :] = yh.T.astype(BF16)
                else:
                    out_ref[:, sl] = yh.astype(BF16)

    @pl.when(j == 0)
    def _():
        head_norm(qg_ref, qa_ref, q_scale, False)

    @pl.when(j == 1)
    def _():
        head_norm(kg_ref, ka_ref, 1.0, True)

    @pl.when(j == 2)
    def _():
        va_ref[...] = project(0, IN_TN).astype(BF16)

    @pl.when(j == 3)
    def _():
        cq_ref[...] = (_rms(project(0, Q_RANK)) * cqg_ref[...]).astype(BF16)
        y = project(Q_RANK, IN_TN)
        ckv_ref[...] = (_rms(y[:, :KV_RANK]) * ckvg_ref[...]).astype(BF16)
        kpe_ref[...] = y[:, KV_RANK:KV_RANK + LANES]


def _inproj(x, g, w4, qg, kg, cqg, ckvg, tm):
    rows = x.shape[0]
    row = lambda i, j: (i, 0)
    const = lambda i, j: (0, 0)
    outs = (
        jax.ShapeDtypeStruct((rows, NA_WIDTH), BF16),
        jax.ShapeDtypeStruct((NA_WIDTH, rows), BF16),
        jax.ShapeDtypeStruct((rows, NA_WIDTH), BF16),
        jax.ShapeDtypeStruct((rows, Q_RANK), BF16),
        jax.ShapeDtypeStruct((rows, KV_RANK), BF16),
        jax.ShapeDtypeStruct((rows, LANES), F32),
    )
    return pl.pallas_call(
        functools.partial(_inproj_body, q_scale=HEAD_DIM ** -0.5 * LOG2E),
        grid=(rows // tm, 4),
        in_specs=[
            pl.BlockSpec((tm, D_MODEL), row),
            pl.BlockSpec((1, D_MODEL), const),
            pl.BlockSpec((None, D_MODEL, IN_TN), lambda i, j: (j, 0, 0)),
            pl.BlockSpec((1, HEAD_DIM), const),
            pl.BlockSpec((1, HEAD_DIM), const),
            pl.BlockSpec((1, Q_RANK), const),
            pl.BlockSpec((1, KV_RANK), const),
        ],
        out_specs=[
            pl.BlockSpec((tm, NA_WIDTH), row),
            pl.BlockSpec((NA_WIDTH, tm), lambda i, j: (0, i)),
            pl.BlockSpec((tm, NA_WIDTH), row),
            pl.BlockSpec((tm, Q_RANK), row),
            pl.BlockSpec((tm, KV_RANK), row),
            pl.BlockSpec((tm, LANES), row),
        ],
        out_shape=outs,
        scratch_shapes=[pltpu.VMEM((tm, D_MODEL), BF16)],
        compiler_params=_cparams("parallel", "arbitrary"),
        name="inproj",
    )(x, g, w4, qg, kg, cqg, ckvg)


def _mla_up_body(cq_ref, ckv_ref, kpe_ref, cos_ref, sin_ref, cost_ref, sint_ref,
                 wqt_ref, wk_ref, wvt_ref, qgt_ref, kg_ref, qt_ref, k_ref, vt_ref, *, q_scale):
    half = ROPE_DIM // 2
    ckv = ckv_ref[...]
    cqt = cq_ref[...].astype(F32).T.astype(BF16)
    ckvt = ckv.astype(F32).T.astype(BF16)
    qgt = qgt_ref[...]
    cost = cost_ref[...]
    sint = sint_ref[...]
    cosf = cos_ref[...]
    sinf = sin_ref[...]
    kg = kg_ref[...]

    kpe = kpe_ref[...]
    kpe_ss = jnp.sum(kpe * kpe, axis=-1, keepdims=True)
    r = kpe * kg[:, LANES:]
    kpe_rot = r * cosf + (pltpu.roll(r, half, 1) + pltpu.roll(r, LANES - half, 1)) * sinf

    for h in range(HEADS):
        q = _dot(wqt_ref[h], cqt)
        ms = jnp.sum(q * q, axis=0, keepdims=True) * (1.0 / QK_DIM)
        qn = q * (lax.rsqrt(ms + EPS) * q_scale) * qgt
        x1 = qn[LANES:LANES + half]
        x2 = qn[LANES + half:LANES + 2 * half]
        qt_ref[h, 0:LANES, :] = qn[:LANES].astype(BF16)
        qt_ref[h, LANES:LANES + half, :] = (x1 * cost - x2 * sint).astype(BF16)
        qt_ref[h, LANES + half:LANES + 2 * half, :] = (x2 * cost + x1 * sint).astype(BF16)
        qt_ref[h, LANES + 2 * half:, :] = jnp.zeros((QK_PAD - QK_DIM, q.shape[1]), BF16)
        vt_ref[h] = _dot(wvt_ref[h], ckvt).astype(BF16)

    for t in range(HEADS // 2):
        kn2 = _dot(ckv, wk_ref[t])
        for u in range(2):
            kn = kn2[:, u * LANES:(u + 1) * LANES]
            ms = (jnp.sum(kn * kn, axis=-1, keepdims=True) + kpe_ss) * (1.0 / QK_DIM)
            rn = lax.rsqrt(ms + EPS)
            k_ref[2 * t + u] = jnp.concatenate([kn * rn * kg[:, :LANES], kpe_rot * rn], axis=-1).astype(BF16)


def _mla_up(cq, ckv, kpe, cosf, sinf, cost, sint, wqt, wk, wvt, qg, kg, nb, tm):
    rows = cq.shape[0]
    nt = rows // nb // tm
    row = lambda b, i: (b * nt + i, 0)
    pos = lambda b, i: (i, 0)
    post = lambda b, i: (0, i)
    c2 = lambda b, i: (0, 0)
    c3 = lambda b, i: (0, 0, 0)
    qgt = jnp.broadcast_to(qg.reshape(QK_PAD, 1), (QK_PAD, tm))
    outs = (
        jax.ShapeDtypeStruct((nb, HEADS, QK_PAD, nt * tm), BF16),
        jax.ShapeDtypeStruct((nb, HEADS, nt, tm, QK_PAD), BF16),
        jax.ShapeDtypeStruct((nb, HEADS, nt, HEAD_DIM, tm), BF16),
    )
    return pl.pallas_call(
        functools.partial(_mla_up_body, q_scale=QK_DIM ** -0.5 * LOG2E),
        grid=(nb, nt),
        in_specs=[
            pl.BlockSpec((tm, Q_RANK), row),
            pl.BlockSpec((tm, KV_RANK), row),
            pl.BlockSpec((tm, LANES), row),
            pl.BlockSpec((tm, LANES), pos),
            pl.BlockSpec((tm, LANES), pos),
            pl.BlockSpec((ROPE_DIM // 2, tm), post),
            pl.BlockSpec((ROPE_DIM // 2, tm), post),
            pl.BlockSpec((HEADS, QK_PAD, Q_RANK), c3),
            pl.BlockSpec((HEADS // 2, KV_RANK, 2 * HEAD_DIM), c3),
            pl.BlockSpec((HEADS, HEAD_DIM, KV_RANK), c3),
            pl.BlockSpec((QK_PAD, tm), c2),
            pl.BlockSpec((1, QK_PAD), c2),
        ],
        out_specs=[
            pl.BlockSpec((None, HEADS, QK_PAD, tm), lambda b, i: (b, 0, 0, i)),
            pl.BlockSpec((None, HEADS, None, tm, QK_PAD), lambda b, i: (b, 0, i, 0, 0)),
            pl.BlockSpec((None, HEADS, None, HEAD_DIM, tm), lambda b, i: (b, 0, i, 0, 0)),
        ],
        out_shape=outs,
        compiler_params=_cparams("parallel", "parallel"),
        name="mla_up",
    )(cq, ckv, kpe, cosf, sinf, cost, sint, wqt, wk, wvt, qgt, kg)


def _mla_attn_body(qt_ref, qtn_ref, k_ref, vt_ref, km_ref, vmt_ref, o_ref, acc_ref, s_ref, gm_ref,
                   *, nk, tk, carry_over):
    qt = qt_ref[...]
    ng = nk // ATT_GROUP

    def scores(g, half, q=None):
        q = qt if q is None else q
        gm = None
        for u in range(ATT_GROUP):
            for j in range(tk // ATT_SUB):
                rows = pl.ds(j * ATT_SUB, ATT_SUB)
                s = _dot(k_ref[g * ATT_GROUP + u, rows, :], q)
                s_ref[half * ATT_GROUP + u, rows, :] = s
                m = jnp.max(s, axis=0, keepdims=True)
                gm = m if gm is None else jnp.maximum(gm, m)
        return gm

    def attend(g, half, gm, m_prev, l_prev):
        m_new = jnp.maximum(m_prev, gm)
        alpha = jnp.exp2(m_prev - m_new)
        l_new = alpha * l_prev
        pv = None
        for u in range(ATT_GROUP):
            p = jnp.exp2(s_ref[half * ATT_GROUP + u] - m_new)
            l_new = l_new + jnp.sum(p, axis=0, keepdims=True)
            d = _dot(vt_ref[g * ATT_GROUP + u], p.astype(BF16))
            pv = d if pv is None else pv + d
        acc_ref[...] = alpha * acc_ref[...] + pv
        return m_new, l_new

    def two_stages(i, carry):
        m, l, gm0 = carry
        gm1 = scores(2 * i + 1, 1)
        m, l = attend(2 * i, 0, gm0, m, l)
        gm0 = scores(2 * i + 2, 0)
        m, l = attend(2 * i + 1, 1, gm1, m, l)
        return m, l, gm0

    if carry_over:
        @pl.when(pl.program_id(2) == 0)
        def _():
            gm_ref[...] = scores(0, 0)
        gm_first = gm_ref[...]
    else:
        gm_first = scores(0, 0)

    s = _dot(km_ref[...], qt)
    key = lax.broadcasted_iota(jnp.int32, s.shape, 0)
    s = jnp.where(key < N_META, s, NEG)
    m0 = jnp.max(s, axis=0, keepdims=True)
    p = jnp.exp2(s - m0)
    l0 = jnp.sum(p, axis=0, keepdims=True)
    acc_ref[...] = _dot(vmt_ref[...], p.astype(BF16))

    m, l, gm0 = lax.fori_loop(0, ng // 2 - 1, two_stages, (m0, l0, gm_first))
    gm1 = scores(ng - 1, 1)
    m, l = attend(ng - 2, 0, gm0, m, l)
    if carry_over:
        gm_ref[...] = scores(0, 0, qtn_ref[...])
    m, l = attend(ng - 1, 1, gm1, m, l)
    o_ref[...] = (acc_ref[...] / l).T


def _mla_attn(qt, k, vt, km, vmt, tq, shared_q):
    nb, _, nk, tk, _ = k.shape
    nq = qt.shape[3] // tq
    qb = (lambda b: 0) if shared_q else (lambda b: b)
    return pl.pallas_call(
        functools.partial(_mla_attn_body, nk=nk, tk=tk, carry_over=nq > 1),
        grid=(nb, HEADS, nq),
        in_specs=[
            pl.BlockSpec((None, None, QK_PAD, tq), lambda b, h, i: (qb(b), h, 0, i)),
            pl.BlockSpec((None, None, QK_PAD, tq), lambda b, h, i: (qb(b), h, 0, jnp.minimum(i + 1, nq - 1))),
            pl.BlockSpec((None, None, nk, tk, QK_PAD), lambda b, h, i: (b, h, 0, 0, 0),
                         pipeline_mode=pl.Buffered(1)),
            pl.BlockSpec((None, None, nk, HEAD_DIM, tk), lambda b, h, i: (b, h, 0, 0, 0),
                         pipeline_mode=pl.Buffered(1)),
            pl.BlockSpec((None, None, None, META_ROWS, QK_PAD), lambda b, h, i: (0, h, 0, 0, 0)),
            pl.BlockSpec((None, None, None, HEAD_DIM, META_ROWS), lambda b, h, i: (0, h, 0, 0, 0)),
        ],
        out_specs=pl.BlockSpec((None, tq, HEAD_DIM), lambda b, h, i: (b, i, h)),
        out_shape=jax.ShapeDtypeStruct((nb, nq * tq, NA_WIDTH), F32),
        scratch_shapes=[pltpu.VMEM((HEAD_DIM, tq), F32), pltpu.VMEM((2 * ATT_GROUP, tk, tq), F32),
                        pltpu.VMEM((1, tq), F32)],
        compiler_params=_cparams("parallel", "parallel", "arbitrary"),
        name="mla_attn",
    )(qt, qt, k, vt, km, vmt)


def _na_bias_body(rpb_ref, o_ref, t_ref, *, rows):
    h = pl.program_id(0)
    n_dr = 2 * NA_WIN_ROWS - 1
    n_dc = 2 * NA_WIN_COLS - 1
    shape = (GRID_W, LANES)
    qc = lax.broadcasted_iota(jnp.int32, shape, 0)
    lane = lax.broadcasted_iota(jnp.int32, shape, 1)
    left = lane < GRID_W
    kc = jnp.where(left, lane, lane - GRID_W)
    cs = jnp.clip(qc - NA_WIN_COLS // 2, 0, GRID_W - NA_WIN_COLS)
    col_ok = (kc >= cs) & (kc < cs + NA_WIN_COLS)
    dc = kc - qc + NA_WIN_COLS - 1

    def build(dr, carry):
        def pick(j, acc):
            return jnp.where(dc == j, rpb_ref[h, dr, j], acc)

        tile = lax.fori_loop(0, n_dc, pick, jnp.zeros(shape, F32))
        t_ref[dr] = jnp.where(col_ok, tile * LOG2E, NEG)
        return carry

    lax.fori_loop(0, n_dr, build, 0)

    neg = jnp.full(shape, NEG, F32)
    for c, r0 in enumerate((0, 2 * NA_QROWS, rows - NA_QROWS)):
        for qr in range(NA_QROWS):
            abs_qr = r0 + qr
            rs = min(max(abs_qr - NA_WIN_ROWS // 2, 0), rows - NA_WIN_ROWS)
            for a in range(NA_KROWS // 2):
                halves = []
                for kr in (r0 - NA_QROWS + 2 * a, r0 - NA_QROWS + 2 * a + 1):
                    in_window = rs <= kr < rs + NA_WIN_ROWS
                    halves.append(t_ref[kr - abs_qr + NA_WIN_ROWS - 1] if in_window else neg)
                o_ref[c, qr * GRID_W:(qr + 1) * GRID_W, a * LANES:(a + 1) * LANES] = (
                    jnp.where(left, halves[0], halves[1]))


def _na_bias(rpb, rows):
    return pl.pallas_call(
        functools.partial(_na_bias_body, rows=rows),
        grid=(HEADS,),
        in_specs=[pl.BlockSpec(memory_space=pltpu.SMEM)],
        out_specs=pl.BlockSpec((3, None, NA_TQ, NA_KROWS * GRID_W), lambda h: (0, h, 0, 0)),
        out_shape=jax.ShapeDtypeStruct((3, HEADS, NA_TQ, NA_KROWS * GRID_W), F32),
        scratch_shapes=[pltpu.VMEM((2 * NA_WIN_ROWS - 1, GRID_W, LANES), F32)],
        compiler_params=_cparams("parallel"),
        name="na_bias",
    )(rpb)


def _na_body(q_ref, kp_ref, kc_ref, kn_ref, vp_ref, vc_ref, vn_ref, km_ref, vm_ref,
             bias_ref, mb_ref, g_ref, o_ref, out_ref, s_ref):
    for h in range(HEADS):
        sl = slice(h * HEAD_DIM, (h + 1) * HEAD_DIM)
        q = q_ref[:, sl]
        for c, k_ref in enumerate((kp_ref, kc_ref, kn_ref)):
            cols = slice(c * NA_TQ, (c + 1) * NA_TQ)
            s_ref[h, :, cols] = _dot(q, k_ref[sl, :]) + bias_ref[h, :, cols]
        s_ref[h, :, 3 * NA_TQ:] = _dot(q, km_ref[sl, :]) + mb_ref[h]
    for h in range(HEADS):
        sl = slice(h * HEAD_DIM, (h + 1) * HEAD_DIM)
        s = s_ref[h]
        m = jnp.max(s, axis=-1, keepdims=True)
        p = jnp.exp2(s - m)
        l = jnp.sum(p, axis=-1, keepdims=True)
        pb = p.astype(BF16)
        o = (_dot(pb[:, 3 * NA_TQ:], vm_ref[:, sl])
             + _dot(pb[:, :NA_TQ], vp_ref[:, sl])
             + _dot(pb[:, NA_TQ:2 * NA_TQ], vc_ref[:, sl])
             + _dot(pb[:, 2 * NA_TQ:3 * NA_TQ], vn_ref[:, sl]))
        out_ref[:, sl] = o / l
    o_ref[...] = (_rms(out_ref[...]) * g_ref[...]).astype(BF16)


def _na_attn(qa, kat, va, kmt, vm, bias, mb, g, nb):
    rows = qa.shape[0]
    n = rows // nb // NA_TQ
    cur = lambda b, i: (b * n + i, 0)
    prev = lambda b, i: (b * n + jnp.maximum(i - 1, 0), 0)
    nxt = lambda b, i: (b * n + jnp.minimum(i + 1, n - 1), 0)
    blk = lambda f: pl.BlockSpec((NA_TQ, NA_WIDTH), f)
    blk_t = lambda f: pl.BlockSpec((NA_WIDTH, NA_TQ), lambda b, i: f(b, i)[::-1])
    case = lambda b, i: (jnp.where(i == 0, 0, jnp.where(i == n - 1, 2, 1)), 0, 0, 0)
    return pl.pallas_call(
        _na_body,
        grid=(nb, n),
        in_specs=[
            blk(cur), blk_t(prev), blk_t(cur), blk_t(nxt), blk(prev), blk(cur), blk(nxt),
            pl.BlockSpec((NA_WIDTH, META_ROWS), lambda b, i: (0, 0)),
            pl.BlockSpec((META_ROWS, NA_WIDTH), lambda b, i: (0, 0)),
            pl.BlockSpec((None, HEADS, NA_TQ, NA_KROWS * GRID_W), case),
            pl.BlockSpec((HEADS, 1, META_ROWS), lambda b, i: (0, 0, 0)),
            pl.BlockSpec((1, NA_WIDTH), lambda b, i: (0, 0)),
        ],
        out_specs=blk(cur),
        out_shape=jax.ShapeDtypeStruct((rows, NA_WIDTH), BF16),
        scratch_shapes=[pltpu.VMEM((NA_TQ, NA_WIDTH), F32),
                        pltpu.VMEM((HEADS, NA_TQ, NA_KROWS * GRID_W + META_ROWS), F32)],
        compiler_params=_cparams("parallel", "arbitrary"),
        name="na_attn",
    )(qa, kat, kat, kat, va, va, va, kmt, vm, bias, mb, g)


def _na_meta_body(q_ref, kt_ref, v_ref, mb_ref, g_ref, o_ref, out_ref):
    for h in range(HEADS):
        sl = slice(h * HEAD_DIM, (h + 1) * HEAD_DIM)
        s = _dot(q_ref[:, sl], kt_ref[sl, :]) + mb_ref[h]
        m = jnp.max(s, axis=-1, keepdims=True)
        p = jnp.exp2(s - m)
        l = jnp.sum(p, axis=-1, keepdims=True)
        out_ref[:, sl] = _dot(p.astype(BF16), v_ref[:, sl]) / l
    o_ref[...] = (_rms(out_ref[...]) * g_ref[...]).astype(BF16)


def _na_meta(qm, kmt, vm, mb, g):
    full = lambda s: pl.BlockSpec(s, lambda i: (0,) * len(s))
    return pl.pallas_call(
        _na_meta_body,
        grid=(1,),
        in_specs=[full((META_ROWS, NA_WIDTH)), full((NA_WIDTH, META_ROWS)), full((META_ROWS, NA_WIDTH)),
                  full((HEADS, 1, META_ROWS)), full((1, NA_WIDTH))],
        out_specs=full((META_ROWS, NA_WIDTH)),
        out_shape=jax.ShapeDtypeStruct((META_ROWS, NA_WIDTH), BF16),
        scratch_shapes=[pltpu.VMEM((META_ROWS, NA_WIDTH), F32)],
        compiler_params=_cparams("arbitrary"),
        name="na_meta",
    )(qm, kmt, vm, mb, g)


def _outproj_body(x_ref, a_ref, b_ref, bg_ref, w_ref, fg_ref, h_ref, xn_ref):
    bn = (_rms(b_ref[...]) * bg_ref[...]).astype(BF16)
    mix = jnp.concatenate([a_ref[...], bn], axis=-1)
    h = x_ref[...] + _dot(mix, w_ref[...])
    h_ref[...] = h
    xn_ref[...] = (_rms(h) * fg_ref[...]).astype(BF16)


def _outproj(x, a, b, bg, w, fg, tm, x_map, a_map, b_map, rows):
    c = lambda i: (0, 0)
    return pl.pallas_call(
        _outproj_body,
        grid=(rows // tm,),
        in_specs=[
            pl.BlockSpec((tm, D_MODEL), x_map),
            pl.BlockSpec((tm, NA_WIDTH), a_map),
            pl.BlockSpec((tm, NA_WIDTH), b_map),
            pl.BlockSpec((1, NA_WIDTH), c),
            pl.BlockSpec((2 * NA_WIDTH, D_MODEL), c),
            pl.BlockSpec((1, D_MODEL), c),
        ],
        out_specs=[pl.BlockSpec((tm, D_MODEL), lambda i: (i, 0)),
                   pl.BlockSpec((tm, D_MODEL), lambda i: (i, 0))],
        out_shape=(jax.ShapeDtypeStruct((rows, D_MODEL), F32),
                   jax.ShapeDtypeStruct((rows, D_MODEL), BF16)),
        compiler_params=_cparams("parallel"),
        name="outproj",
    )(x, a, b, bg, w, fg)


def _ffn_body(xm_ref, xp_ref, xx_ref, xmeta_ref, h_ref, wg_ref, wu_ref, wd_ref, cw_ref, cb_ref,
              o_ref, xe_ref, *, tm, tpb):
    i = pl.program_id(0)
    j = pl.program_id(1)

    @pl.when(j == 0)
    def _():
        first = (i % tpb) == 0
        last = (i % tpb) == tpb - 1
        xe_ref[0:HALO, :] = jnp.where(first, xmeta_ref[...], xp_ref[...])
        xe_ref[HALO:HALO + tm, :] = xm_ref[...]
        xe_ref[HALO + tm:2 * HALO + tm, :] = jnp.where(last, jnp.zeros_like(xx_ref[...]), xx_ref[...])
        o_ref[...] = h_ref[...]

    g = _dot(xe_ref[...], wg_ref[...])
    u = _dot(xe_ref[HALO:HALO + tm, :], wu_ref[...])
    cw = cw_ref[...]
    gc = (cb_ref[...] + cw[0:1] * g[HALO - 1:HALO - 1 + tm]
          + cw[1:2] * g[HALO:HALO + tm] + cw[2:3] * g[HALO + 1:HALO + 1 + tm])
    act = gc * (1.0 / (1.0 + jnp.exp(-gc))) * u
    o_ref[...] += _dot(act.astype(BF16), wd_ref[...])


def _ffn(xn, xn_meta, h1, wg, wu, wd, cw, cb, nb, tm, tf):
    rows = xn.shape[0]
    nt = rows // tm
    tpb = nt // nb
    hb = tm // HALO
    last_hb = rows // HALO - 1
    return pl.pallas_call(
        functools.partial(_ffn_body, tm=tm, tpb=tpb),
        grid=(nt, D_FF // tf),
        in_specs=[
            pl.BlockSpec((tm, D_MODEL), lambda i, j: (i, 0)),
            pl.BlockSpec((HALO, D_MODEL), lambda i, j: (jnp.maximum(i * hb - 1, 0), 0)),
            pl.BlockSpec((HALO, D_MODEL), lambda i, j: (jnp.minimum((i + 1) * hb, last_hb), 0)),
            pl.BlockSpec((HALO, D_MODEL), lambda i, j: ((i // tpb) * (META_ROWS // HALO), 0)),
            pl.BlockSpec((tm, D_MODEL), lambda i, j: (i, 0)),
            pl.BlockSpec((D_MODEL, tf), lambda i, j: (0, j)),
            pl.BlockSpec((D_MODEL, tf), lambda i, j: (0, j)),
            pl.BlockSpec((tf, D_MODEL), lambda i, j: (j, 0)),
            pl.BlockSpec((3, tf), lambda i, j: (0, j)),
            pl.BlockSpec((1, tf), lambda i, j: (0, j)),
        ],
        out_specs=pl.BlockSpec((tm, D_MODEL), lambda i, j: (i, 0)),
        out_shape=jax.ShapeDtypeStruct((rows, D_MODEL), F32),
        scratch_shapes=[pltpu.VMEM((tm + 2 * HALO, D_MODEL), BF16)],
        compiler_params=_cparams("parallel", "arbitrary"),
        name="ffn",
    )(xn, xn, xn, xn_meta, h1, wg, wu, wd, cw, cb)


def _rope_tables(pos):
    inv = ROPE_THETA ** (-jnp.arange(0, ROPE_DIM, 2, dtype=F32) / ROPE_DIM)
    ang = pos.astype(F32)[:, None] * inv[None, :]
    cos, sin = jnp.cos(ang), jnp.sin(ang)
    z = jnp.zeros_like(cos)
    return (jnp.concatenate([cos, cos, z, z], axis=-1),
            jnp.concatenate([-sin, sin, z, z], axis=-1), cos.T, sin.T)


def _pad_lanes(v, width):
    return jnp.pad(v, ((0, 0), (0, width - v.shape[-1])))


def kernel(x, meta_tokens, mix_norm_g, w_in, na_q_g, na_k_g, na_rpb, na_meta_bias, mla_cq_g, mla_ckv_g,
           w_q_up, w_kv_up, mla_q_g, mla_k_g, na_out_g, mla_out_g, w_out, ffn_norm_g, w_gate, w_up,
           conv_w, conv_b, w_down):
    nb, seq, d = x.shape
    assert d == D_MODEL and w_in.shape[0] == 1, "one layer of width 2048"
    rows_grid = seq // GRID_W
    assert seq % max(IN_TM, UP_TM, ATT_TQ, FFN_TM) == 0 and rows_grid >= 4 * NA_QROWS

    w_in_p = _pad_lanes(w_in[0], 4 * IN_TN).astype(BF16)
    w4 = w_in_p.reshape(D_MODEL, 4, IN_TN).transpose(1, 0, 2)
    wqt = w_q_up[0].reshape(Q_RANK, HEADS, QK_DIM).transpose(1, 2, 0)
    wqt = jnp.pad(wqt, ((0, 0), (0, QK_PAD - QK_DIM), (0, 0))).astype(BF16)
    wkv = w_kv_up[0].reshape(KV_RANK, HEADS, 2 * HEAD_DIM)
    wk = wkv[:, :, :HEAD_DIM].reshape(KV_RANK, HEADS // 2, 2 * HEAD_DIM).transpose(1, 0, 2).astype(BF16)
    wvt = wkv[:, :, HEAD_DIM:].transpose(1, 2, 0).astype(BF16)
    wo = w_out[0].astype(BF16)
    wg, wu, wd = w_gate[0].astype(BF16), w_up[0].astype(BF16), w_down[0].astype(BF16)
    row = lambda v: v.reshape(1, -1)
    qg_pad, kg_pad = _pad_lanes(mla_q_g, QK_PAD), _pad_lanes(mla_k_g, QK_PAD)
    mb = jnp.pad(na_meta_bias[0] * LOG2E, ((0, 0), (0, META_ROWS - N_META)),
                 constant_values=NEG).reshape(HEADS, 1, META_ROWS)

    xr = x.reshape(nb * seq, D_MODEL)
    xm = jnp.pad(meta_tokens.astype(x.dtype), ((0, META_ROWS - N_META), (0, 0)))
    rope_r = _rope_tables(N_META + jnp.arange(seq))
    rope_m = _rope_tables(jnp.arange(META_ROWS))

    inproj = functools.partial(_inproj, g=row(mix_norm_g), w4=w4, qg=row(na_q_g), kg=row(na_k_g),
                               cqg=row(mla_cq_g), ckvg=row(mla_ckv_g))
    qa, ka, va, cq, ckv, kpe = inproj(xr, tm=IN_TM)
    qa_m, ka_m, va_m, cq_m, ckv_m, kpe_m = inproj(xm, tm=META_ROWS)

    up = functools.partial(_mla_up, wqt=wqt, wk=wk, wvt=wvt, qg=qg_pad, kg=kg_pad)
    qt, kk, vt = up(cq, ckv, kpe, *rope_r, nb=nb, tm=UP_TM)
    qt_m, kk_m, vt_m = up(cq_m, ckv_m, kpe_m, *rope_m, nb=1, tm=META_ROWS)

    bias = _na_bias(na_rpb[0] , rows_grid)
    a_n = _na_attn(qa, ka, va, ka_m, va_m, bias, mb, row(na_out_g), nb)
    a_n_m = _na_meta(qa_m, ka_m, va_m, mb, row(na_out_g))

    out_b = _mla_attn(qt, kk, vt, kk_m, vt_m, ATT_TQ, shared_q=False)
    out_b_m = _mla_attn(qt_m, kk, vt, kk_m, vt_m, META_ROWS, shared_q=True)

    op = functools.partial(_outproj, bg=row(mla_out_g), w=wo, fg=row(ffn_norm_g))
    ident = lambda i: (i, 0)
    zero = lambda i: (0, 0)
    h1, xn2 = op(xr, a_n, out_b.reshape(nb * seq, NA_WIDTH), tm=OUT_TM,
                 x_map=ident, a_map=ident, b_map=ident, rows=nb * seq)
    _, xn2_m = op(xm, a_n_m, out_b_m.reshape(nb * META_ROWS, NA_WIDTH), tm=META_ROWS,
                  x_map=zero, a_map=zero, b_map=ident, rows=nb * META_ROWS)

    out = _ffn(xn2, xn2_m, h1, wg, wu, wd, conv_w[0], row(conv_b), nb, FFN_TM, FFN_TF)
    return out.reshape(nb, seq, D_MODEL)
```

```python
import functools

import jax
import jax.numpy as jnp
import numpy as np
from jax import lax
from jax.experimental import pallas as pl
from jax.experimental.pallas import tpu as pltpu

F32 = jnp.float32
BF16 = jnp.bfloat16

LANES = 128
BF16_SUBLANES = 16
VMEM_LIMIT_BYTES = 56 * 1024 * 1024

D_MODEL = 2048
GRID_W = 64
N_META = 16
EPS = 1e-6
NEG = -1e30
LOG2E = 1.4426950408889634
HEADS = 8
HEAD_DIM = 128
NA_WIDTH = HEADS * HEAD_DIM
NA_WIN_ROWS = 8
NA_WIN_COLS = 16
Q_RANK = 512
KV_RANK = 256
ROPE_DIM = 64
QK_DIM = HEAD_DIM + ROPE_DIM
QK_PAD = 2 * LANES
ROPE_THETA = 10000.0
D_FF = 5632
META_ROWS = LANES
SOFTMAX_UNDERFLOW = 2.0 ** -100

IN_TM = 1024
IN_TN = 1024
UP_TM = 512
ATT_TQ = 1024
ATT_GROUP = 4
NA_QROWS = 4
NA_TQ = NA_QROWS * GRID_W
NA_KROWS = 3 * NA_QROWS
OUT_TM = 256
FFN_TM = 512
FFN_TF = 512
HALO = BF16_SUBLANES


def _cparams(*sem):
    return pltpu.CompilerParams(dimension_semantics=sem, vmem_limit_bytes=VMEM_LIMIT_BYTES)


def _rms(v, n=None):
    if n is None:
        ms = jnp.mean(v * v, axis=-1, keepdims=True)
    else:
        ms = jnp.sum(v * v, axis=-1, keepdims=True) * (1.0 / n)
    return v * lax.rsqrt(ms + EPS)


def _dot(a, b):
    return jnp.dot(a, b, preferred_element_type=F32)


def _inproj_body(x_ref, g_ref, w_ref, qg_ref, kg_ref, cqg_ref, ckvg_ref,
                 qa_ref, ka_ref, va_ref, cq_ref, ckv_ref, kpe_ref, xn_ref, *, q_scale):
    j = pl.program_id(1)

    @pl.when(j == 0)
    def _():
        xn_ref[...] = (_rms(x_ref[...]) * g_ref[...]).astype(BF16)

    def project(lo, hi):
        return _dot(xn_ref[...], w_ref[:, lo:hi])

    def head_norm(gain_ref, out_ref, post, transposed):
        for t in range(HEADS // 2):
            y = project(2 * t * HEAD_DIM, 2 * (t + 1) * HEAD_DIM)
            for u in range(2):
                sl = slice((2 * t + u) * HEAD_DIM, (2 * t + u + 1) * HEAD_DIM)
                yh = _rms(y[:, u * HEAD_DIM:(u + 1) * HEAD_DIM]) * gain_ref[...] * post
                if transposed:
                    out_ref[sl, :] = yh.T.astype(BF16)
                else:
                    out_ref[:, sl] = yh.astype(BF16)

    @pl.when(j == 0)
    def _():
        head_norm(qg_ref, qa_ref, q_scale, False)

    @pl.when(j == 1)
    def _():
        head_norm(kg_ref, ka_ref, 1.0, True)

    @pl.when(j == 2)
    def _():
        va_ref[...] = project(0, IN_TN).astype(BF16)

    @pl.when(j == 3)
    def _():
        cq_ref[...] = (_rms(project(0, Q_RANK)) * cqg_ref[...]).astype(BF16)
        y = project(Q_RANK, IN_TN)
        ckv_ref[...] = (_rms(y[:, :KV_RANK]) * ckvg_ref[...]).astype(BF16)
        kpe_ref[...] = y[:, KV_RANK:KV_RANK + LANES]


def _inproj(x, g, w4, qg, kg, cqg, ckvg, tm):
    rows = x.shape[0]
    row = lambda i, j: (i, 0)
    const = lambda i, j: (0, 0)
    outs = (
        jax.ShapeDtypeStruct((rows, NA_WIDTH), BF16),
        jax.ShapeDtypeStruct((NA_WIDTH, rows), BF16),
        jax.ShapeDtypeStruct((rows, NA_WIDTH), BF16),
        jax.ShapeDtypeStruct((rows, Q_RANK), BF16),
        jax.ShapeDtypeStruct((rows, KV_RANK), BF16),
        jax.ShapeDtypeStruct((rows, LANES), F32),
    )
    return pl.pallas_call(
        functools.partial(_inproj_body, q_scale=HEAD_DIM ** -0.5 * LOG2E),
        grid=(rows // tm, 4),
        in_specs=[
            pl.BlockSpec((tm, D_MODEL), row),
            pl.BlockSpec((1, D_MODEL), const),
            pl.BlockSpec((None, D_MODEL, IN_TN), lambda i, j: (j, 0, 0)),
            pl.BlockSpec((1, HEAD_DIM), const),
            pl.BlockSpec((1, HEAD_DIM), const),
            pl.BlockSpec((1, Q_RANK), const),
            pl.BlockSpec((1, KV_RANK), const),
        ],
        out_specs=[
            pl.BlockSpec((tm, NA_WIDTH), row),
            pl.BlockSpec((NA_WIDTH, tm), lambda i, j: (0, i)),
            pl.BlockSpec((tm, NA_WIDTH), row),
            pl.BlockSpec((tm, Q_RANK), row),
            pl.BlockSpec((tm, KV_RANK), row),
            pl.BlockSpec((tm, LANES), row),
        ],
        out_shape=outs,
        scratch_shapes=[pltpu.VMEM((tm, D_MODEL), BF16)],
        compiler_params=_cparams("parallel", "arbitrary"),
        name="inproj",
    )(x, g, w4, qg, kg, cqg, ckvg)


def _mla_up_body(cq_ref, ckv_ref, kpe_ref, cos_ref, sin_ref, cost_ref, sint_ref,
                 wqt_ref, wk_ref, wvt_ref, qgt_ref, qpad_ref, kg_ref, qt_ref, k_ref, vt_ref, *, q_scale):
    half = ROPE_DIM // 2
    ckv = ckv_ref[...]
    cqt = cq_ref[...].astype(F32).T.astype(BF16)
    ckvt = ckv.astype(F32).T.astype(BF16)
    qgt = qgt_ref[...]
    cost = cost_ref[...]
    sint = sint_ref[...]
    cosf = cos_ref[...]
    sinf = sin_ref[...]
    kg = kg_ref[...]

    kpe = kpe_ref[...]
    kpe_ss = jnp.sum(kpe * kpe, axis=-1, keepdims=True)
    r = kpe * kg[:, LANES:]
    kpe_rot = r * cosf + (pltpu.roll(r, half, 1) + pltpu.roll(r, LANES - half, 1)) * sinf

    for h in range(HEADS):
        q = _dot(wqt_ref[h], cqt)
        ms = jnp.sum(q * q, axis=0, keepdims=True) * (1.0 / QK_DIM)
        qn = q * (lax.rsqrt(ms + EPS) * q_scale) * qgt
        x1 = qn[LANES:LANES + half]
        x2 = qn[LANES + half:LANES + 2 * half]
        qt_ref[h, 0:LANES, :] = qn[:LANES].astype(BF16)
        qt_ref[h, LANES:LANES + half, :] = (x1 * cost - x2 * sint).astype(BF16)
        qt_ref[h, LANES + half:LANES + 2 * half, :] = (x2 * cost + x1 * sint).astype(BF16)
        qt_ref[h, LANES + 2 * half:, :] = qpad_ref[...]
        vt_ref[h] = _dot(wvt_ref[h], ckvt).astype(BF16)

    one_lane = (lax.broadcasted_iota(jnp.int32, (1, LANES), 1) == ROPE_DIM).astype(F32)
    for t in range(HEADS // 2):
        kn2 = _dot(ckv, wk_ref[t])
        for u in range(2):
            kn = kn2[:, u * LANES:(u + 1) * LANES]
            ms = (jnp.sum(kn * kn, axis=-1, keepdims=True) + kpe_ss) * (1.0 / QK_DIM)
            rn = lax.rsqrt(ms + EPS)
            k_ref[2 * t + u] = jnp.concatenate([kn * rn * kg[:, :LANES], kpe_rot * rn + one_lane],
                                               axis=-1).astype(BF16)


def _mla_up(cq, ckv, kpe, cosf, sinf, cost, sint, wqt, wk, wvt, qg, kg, shift, nb, tm):
    rows = cq.shape[0]
    nt = rows // nb // tm
    row = lambda b, i: (b * nt + i, 0)
    pos = lambda b, i: (i, 0)
    post = lambda b, i: (0, i)
    c2 = lambda b, i: (0, 0)
    c3 = lambda b, i: (0, 0, 0)
    qgt = jnp.broadcast_to(qg.reshape(QK_PAD, 1), (QK_PAD, tm))
    qpad = jnp.zeros((QK_PAD - QK_DIM, tm), F32).at[0].set(-shift).astype(BF16)
    outs = (
        jax.ShapeDtypeStruct((nb, HEADS, QK_PAD, nt * tm), BF16),
        jax.ShapeDtypeStruct((nb, HEADS, nt, tm, QK_PAD), BF16),
        jax.ShapeDtypeStruct((nb, HEADS, nt, HEAD_DIM, tm), BF16),
    )
    return pl.pallas_call(
        functools.partial(_mla_up_body, q_scale=QK_DIM ** -0.5 * LOG2E),
        grid=(nb, nt),
        in_specs=[
            pl.BlockSpec((tm, Q_RANK), row),
            pl.BlockSpec((tm, KV_RANK), row),
            pl.BlockSpec((tm, LANES), row),
            pl.BlockSpec((tm, LANES), pos),
            pl.BlockSpec((tm, LANES), pos),
            pl.BlockSpec((ROPE_DIM // 2, tm), post),
            pl.BlockSpec((ROPE_DIM // 2, tm), post),
            pl.BlockSpec((HEADS, QK_PAD, Q_RANK), c3),
            pl.BlockSpec((HEADS // 2, KV_RANK, 2 * HEAD_DIM), c3),
            pl.BlockSpec((HEADS, HEAD_DIM, KV_RANK), c3),
            pl.BlockSpec((QK_PAD, tm), c2),
            pl.BlockSpec((QK_PAD - QK_DIM, tm), c2),
            pl.BlockSpec((1, QK_PAD), c2),
        ],
        out_specs=[
            pl.BlockSpec((None, HEADS, QK_PAD, tm), lambda b, i: (b, 0, 0, i)),
            pl.BlockSpec((None, HEADS, None, tm, QK_PAD), lambda b, i: (b, 0, i, 0, 0)),
            pl.BlockSpec((None, HEADS, None, HEAD_DIM, tm), lambda b, i: (b, 0, i, 0, 0)),
        ],
        out_shape=outs,
        compiler_params=_cparams("parallel", "parallel"),
        name="mla_up",
    )(cq, ckv, kpe, cosf, sinf, cost, sint, wqt, wk, wvt, qgt, qpad, kg)


def _mla_attn_body(qt_ref, qtn_ref, k_ref, vt_ref, km_ref, vmt_ref, o_ref, acc_ref, s_ref,
                   *, nk, carry_over):
    qt = qt_ref[...]
    ng = nk // ATT_GROUP

    def scores(g, half, q=None):
        q = qt if q is None else q
        for u in range(ATT_GROUP):
            s_ref[half * ATT_GROUP + u] = _dot(k_ref[g * ATT_GROUP + u], q)

    def attend(g, half, l):
        pv = None
        for u in range(ATT_GROUP):
            p = jnp.exp2(s_ref[half * ATT_GROUP + u])
            l = l + jnp.sum(p, axis=0, keepdims=True)
            d = _dot(vt_ref[g * ATT_GROUP + u], p.astype(BF16))
            pv = d if pv is None else pv + d
        acc_ref[...] += pv
        return l

    def meta_scores():
        s = _dot(km_ref[...], qt)
        key = lax.broadcasted_iota(jnp.int32, s.shape, 0)
        return jnp.where(key < N_META, s, NEG)

    def two_stages(i, l):
        scores(2 * i + 1, 1)
        l = attend(2 * i, 0, l)
        scores(2 * i + 2, 0)
        return attend(2 * i + 1, 1, l)

    if carry_over:
        @pl.when(pl.program_id(2) == 0)
        def _():
            scores(0, 0)
    else:
        scores(0, 0)

    p = jnp.exp2(meta_scores())
    acc_ref[...] = _dot(vmt_ref[...], p.astype(BF16))
    l = lax.fori_loop(0, ng // 2 - 1, two_stages, jnp.sum(p, axis=0, keepdims=True))
    scores(ng - 1, 1)
    l = attend(ng - 2, 0, l)
    if carry_over:
        scores(0, 0, qtn_ref[...])
    l = attend(ng - 1, 1, l)
    o_ref[...] = (acc_ref[...] / l).T

    @pl.when(jnp.min(l) < SOFTMAX_UNDERFLOW)
    def _():
        s = meta_scores()
        m0 = jnp.max(s, axis=0, keepdims=True)
        p = jnp.exp2(s - m0)
        acc_ref[...] = _dot(vmt_ref[...], p.astype(BF16))

        def chunk(c, carry):
            m_prev, l_prev = carry
            s = _dot(k_ref[c], qt)
            m_new = jnp.maximum(m_prev, jnp.max(s, axis=0, keepdims=True))
            alpha = jnp.exp2(m_prev - m_new)
            p = jnp.exp2(s - m_new)
            acc_ref[...] = alpha * acc_ref[...] + _dot(vt_ref[c], p.astype(BF16))
            return m_new, alpha * l_prev + jnp.sum(p, axis=0, keepdims=True)

        _, l_exact = lax.fori_loop(0, nk, chunk, (m0, jnp.sum(p, axis=0, keepdims=True)))
        o_ref[...] = (acc_ref[...] / l_exact).T


def _mla_attn(qt, k, vt, km, vmt, tq, shared_q):
    nb, _, nk, tk, _ = k.shape
    nq = qt.shape[3] // tq
    qb = (lambda b: 0) if shared_q else (lambda b: b)
    return pl.pallas_call(
        functools.partial(_mla_attn_body, nk=nk, carry_over=nq > 1),
        grid=(nb, HEADS, nq),
        in_specs=[
            pl.BlockSpec((None, None, QK_PAD, tq), lambda b, h, i: (qb(b), h, 0, i)),
            pl.BlockSpec((None, None, QK_PAD, tq), lambda b, h, i: (qb(b), h, 0, jnp.minimum(i + 1, nq - 1))),
            pl.BlockSpec((None, None, nk, tk, QK_PAD), lambda b, h, i: (b, h, 0, 0, 0),
                         pipeline_mode=pl.Buffered(1)),
            pl.BlockSpec((None, None, nk, HEAD_DIM, tk), lambda b, h, i: (b, h, 0, 0, 0),
                         pipeline_mode=pl.Buffered(1)),
            pl.BlockSpec((None, None, None, META_ROWS, QK_PAD), lambda b, h, i: (0, h, 0, 0, 0)),
            pl.BlockSpec((None, None, None, HEAD_DIM, META_ROWS), lambda b, h, i: (0, h, 0, 0, 0)),
        ],
        out_specs=pl.BlockSpec((None, tq, HEAD_DIM), lambda b, h, i: (b, i, h)),
        out_shape=jax.ShapeDtypeStruct((nb, nq * tq, NA_WIDTH), F32),
        scratch_shapes=[pltpu.VMEM((HEAD_DIM, tq), F32), pltpu.VMEM((2 * ATT_GROUP, tk, tq), F32)],
        compiler_params=_cparams("parallel", "parallel", "arbitrary"),
        name="mla_attn",
    )(qt, qt, k, vt, km, vmt)


def _na_bias_body(rpb_ref, o_ref, t_ref, *, rows):
    h = pl.program_id(0)
    n_dr = 2 * NA_WIN_ROWS - 1
    n_dc = 2 * NA_WIN_COLS - 1
    shape = (GRID_W, LANES)
    qc = lax.broadcasted_iota(jnp.int32, shape, 0)
    lane = lax.broadcasted_iota(jnp.int32, shape, 1)
    left = lane < GRID_W
    kc = jnp.where(left, lane, lane - GRID_W)
    cs = jnp.clip(qc - NA_WIN_COLS // 2, 0, GRID_W - NA_WIN_COLS)
    col_ok = (kc >= cs) & (kc < cs + NA_WIN_COLS)
    dc = kc - qc + NA_WIN_COLS - 1

    def build(dr, carry):
        def pick(j, acc):
            return jnp.where(dc == j, rpb_ref[h, dr, j], acc)

        tile = lax.fori_loop(0, n_dc, pick, jnp.zeros(shape, F32))
        t_ref[dr] = jnp.where(col_ok, tile * LOG2E, NEG)
        return carry

    lax.fori_loop(0, n_dr, build, 0)

    neg = jnp.full(shape, NEG, F32)
    for c, r0 in enumerate((0, 2 * NA_QROWS, rows - NA_QROWS)):
        for qr in range(NA_QROWS):
            abs_qr = r0 + qr
            rs = min(max(abs_qr - NA_WIN_ROWS // 2, 0), rows - NA_WIN_ROWS)
            for a in range(NA_KROWS // 2):
                halves = []
                for kr in (r0 - NA_QROWS + 2 * a, r0 - NA_QROWS + 2 * a + 1):
                    in_window = rs <= kr < rs + NA_WIN_ROWS
                    halves.append(t_ref[kr - abs_qr + NA_WIN_ROWS - 1] if in_window else neg)
                o_ref[c, qr * GRID_W:(qr + 1) * GRID_W, a * LANES:(a + 1) * LANES] = (
                    jnp.where(left, halves[0], halves[1]))


def _na_bias(rpb, rows):
    return pl.pallas_call(
        functools.partial(_na_bias_body, rows=rows),
        grid=(HEADS,),
        in_specs=[pl.BlockSpec(memory_space=pltpu.SMEM)],
        out_specs=pl.BlockSpec((3, None, NA_TQ, NA_KROWS * GRID_W), lambda h: (0, h, 0, 0)),
        out_shape=jax.ShapeDtypeStruct((3, HEADS, NA_TQ, NA_KROWS * GRID_W), F32),
        scratch_shapes=[pltpu.VMEM((2 * NA_WIN_ROWS - 1, GRID_W, LANES), F32)],
        compiler_params=_cparams("parallel"),
        name="na_bias",
    )(rpb)


def _na_body(q_ref, kp_ref, kc_ref, kn_ref, vp_ref, vc_ref, vn_ref, km_ref, vm_ref,
             bias_ref, mb_ref, g_ref, o_ref, out_ref, s_ref):
    for h in range(HEADS):
        sl = slice(h * HEAD_DIM, (h + 1) * HEAD_DIM)
        q = q_ref[:, sl]
        for c, k_ref in enumerate((kp_ref, kc_ref, kn_ref)):
            cols = slice(c * NA_TQ, (c + 1) * NA_TQ)
            s_ref[h, :, cols] = _dot(q, k_ref[sl, :]) + bias_ref[h, :, cols]
        s_ref[h, :, 3 * NA_TQ:] = _dot(q, km_ref[sl, :]) + mb_ref[h]
    for h in range(HEADS):
        sl = slice(h * HEAD_DIM, (h + 1) * HEAD_DIM)
        s = s_ref[h]
        m = jnp.max(s, axis=-1, keepdims=True)
        p = jnp.exp2(s - m)
        l = jnp.sum(p, axis=-1, keepdims=True)
        pb = p.astype(BF16)
        o = (_dot(pb[:, 3 * NA_TQ:], vm_ref[:, sl])
             + _dot(pb[:, :NA_TQ], vp_ref[:, sl])
             + _dot(pb[:, NA_TQ:2 * NA_TQ], vc_ref[:, sl])
             + _dot(pb[:, 2 * NA_TQ:3 * NA_TQ], vn_ref[:, sl]))
        out_ref[:, sl] = o / l
    o_ref[...] = (_rms(out_ref[...]) * g_ref[...]).astype(BF16)


def _na_attn(qa, kat, va, kmt, vm, bias, mb, g, nb):
    rows = qa.shape[0]
    n = rows // nb // NA_TQ
    cur = lambda b, i: (b * n + i, 0)
    prev = lambda b, i: (b * n + jnp.maximum(i - 1, 0), 0)
    nxt = lambda b, i: (b * n + jnp.minimum(i + 1, n - 1), 0)
    blk = lambda f: pl.BlockSpec((NA_TQ, NA_WIDTH), f)
    blk_t = lambda f: pl.BlockSpec((NA_WIDTH, NA_TQ), lambda b, i: f(b, i)[::-1])
    case = lambda b, i: (jnp.where(i == 0, 0, jnp.where(i == n - 1, 2, 1)), 0, 0, 0)
    return pl.pallas_call(
        _na_body,
        grid=(nb, n),
        in_specs=[
            blk(cur), blk_t(prev), blk_t(cur), blk_t(nxt), blk(prev), blk(cur), blk(nxt),
            pl.BlockSpec((NA_WIDTH, META_ROWS), lambda b, i: (0, 0)),
            pl.BlockSpec((META_ROWS, NA_WIDTH), lambda b, i: (0, 0)),
            pl.BlockSpec((None, HEADS, NA_TQ, NA_KROWS * GRID_W), case),
            pl.BlockSpec((HEADS, 1, META_ROWS), lambda b, i: (0, 0, 0)),
            pl.BlockSpec((1, NA_WIDTH), lambda b, i: (0, 0)),
        ],
        out_specs=blk(cur),
        out_shape=jax.ShapeDtypeStruct((rows, NA_WIDTH), BF16),
        scratch_shapes=[pltpu.VMEM((NA_TQ, NA_WIDTH), F32),
                        pltpu.VMEM((HEADS, NA_TQ, NA_KROWS * GRID_W + META_ROWS), F32)],
        compiler_params=_cparams("parallel", "arbitrary"),
        name="na_attn",
    )(qa, kat, kat, kat, va, va, va, kmt, vm, bias, mb, g)


def _na_meta_body(q_ref, kt_ref, v_ref, mb_ref, g_ref, o_ref, out_ref):
    for h in range(HEADS):
        sl = slice(h * HEAD_DIM, (h + 1) * HEAD_DIM)
        s = _dot(q_ref[:, sl], kt_ref[sl, :]) + mb_ref[h]
        m = jnp.max(s, axis=-1, keepdims=True)
        p = jnp.exp2(s - m)
        l = jnp.sum(p, axis=-1, keepdims=True)
        out_ref[:, sl] = _dot(p.astype(BF16), v_ref[:, sl]) / l
    o_ref[...] = (_rms(out_ref[...]) * g_ref[...]).astype(BF16)


def _na_meta(qm, kmt, vm, mb, g):
    full = lambda s: pl.BlockSpec(s, lambda i: (0,) * len(s))
    return pl.pallas_call(
        _na_meta_body,
        grid=(1,),
        in_specs=[full((META_ROWS, NA_WIDTH)), full((NA_WIDTH, META_ROWS)), full((META_ROWS, NA_WIDTH)),
                  full((HEADS, 1, META_ROWS)), full((1, NA_WIDTH))],
        out_specs=full((META_ROWS, NA_WIDTH)),
        out_shape=jax.ShapeDtypeStruct((META_ROWS, NA_WIDTH), BF16),
        scratch_shapes=[pltpu.VMEM((META_ROWS, NA_WIDTH), F32)],
        compiler_params=_cparams("arbitrary"),
        name="na_meta",
    )(qm, kmt, vm, mb, g)


def _outproj_body(x_ref, a_ref, b_ref, bg_ref, w_ref, fg_ref, h_ref, xn_ref):
    bn = (_rms(b_ref[...]) * bg_ref[...]).astype(BF16)
    mix = jnp.concatenate([a_ref[...], bn], axis=-1)
    h = x_ref[...] + _dot(mix, w_ref[...])
    h_ref[...] = h
    xn_ref[...] = (_rms(h) * fg_ref[...]).astype(BF16)


def _outproj(x, a, b, bg, w, fg, tm, x_map, a_map, b_map, rows):
    c = lambda i: (0, 0)
    return pl.pallas_call(
        _outproj_body,
        grid=(rows // tm,),
        in_specs=[
            pl.BlockSpec((tm, D_MODEL), x_map),
            pl.BlockSpec((tm, NA_WIDTH), a_map),
            pl.BlockSpec((tm, NA_WIDTH), b_map),
            pl.BlockSpec((1, NA_WIDTH), c),
            pl.BlockSpec((2 * NA_WIDTH, D_MODEL), c),
            pl.BlockSpec((1, D_MODEL), c),
        ],
        out_specs=[pl.BlockSpec((tm, D_MODEL), lambda i: (i, 0)),
                   pl.BlockSpec((tm, D_MODEL), lambda i: (i, 0))],
        out_shape=(jax.ShapeDtypeStruct((rows, D_MODEL), F32),
                   jax.ShapeDtypeStruct((rows, D_MODEL), BF16)),
        compiler_params=_cparams("parallel"),
        name="outproj",
    )(x, a, b, bg, w, fg)


def _ffn_body(xm_ref, xp_ref, xx_ref, xmeta_ref, h_ref, wg_ref, wu_ref, wd_ref, cw_ref, cb_ref,
              o_ref, xe_ref, *, tm, tpb):
    i = pl.program_id(0)
    j = pl.program_id(1)

    @pl.when(j == 0)
    def _():
        first = (i % tpb) == 0
        last = (i % tpb) == tpb - 1
        xe_ref[0:HALO, :] = jnp.where(first, xmeta_ref[...], xp_ref[...])
        xe_ref[HALO:HALO + tm, :] = xm_ref[...]
        xe_ref[HALO + tm:2 * HALO + tm, :] = jnp.where(last, jnp.zeros_like(xx_ref[...]), xx_ref[...])
        o_ref[...] = h_ref[...]

    g = _dot(xe_ref[...], wg_ref[...])
    u = _dot(xe_ref[HALO:HALO + tm, :], wu_ref[...])
    cw = cw_ref[...]
    gc = (cb_ref[...] + cw[0:1] * g[HALO - 1:HALO - 1 + tm]
          + cw[1:2] * g[HALO:HALO + tm] + cw[2:3] * g[HALO + 1:HALO + 1 + tm])
    act = gc * (1.0 / (1.0 + jnp.exp(-gc))) * u
    o_ref[...] += _dot(act.astype(BF16), wd_ref[...])


def _ffn(xn, xn_meta, h1, wg, wu, wd, cw, cb, nb, tm, tf):
    rows = xn.shape[0]
    nt = rows // tm
    tpb = nt // nb
    hb = tm // HALO
    last_hb = rows // HALO - 1
    return pl.pallas_call(
        functools.partial(_ffn_body, tm=tm, tpb=tpb),
        grid=(nt, D_FF // tf),
        in_specs=[
            pl.BlockSpec((tm, D_MODEL), lambda i, j: (i, 0)),
            pl.BlockSpec((HALO, D_MODEL), lambda i, j: (jnp.maximum(i * hb - 1, 0), 0)),
            pl.BlockSpec((HALO, D_MODEL), lambda i, j: (jnp.minimum((i + 1) * hb, last_hb), 0)),
            pl.BlockSpec((HALO, D_MODEL), lambda i, j: ((i // tpb) * (META_ROWS // HALO), 0)),
            pl.BlockSpec((tm, D_MODEL), lambda i, j: (i, 0)),
            pl.BlockSpec((D_MODEL, tf), lambda i, j: (0, j)),
            pl.BlockSpec((D_MODEL, tf), lambda i, j: (0, j)),
            pl.BlockSpec((tf, D_MODEL), lambda i, j: (j, 0)),
            pl.BlockSpec((3, tf), lambda i, j: (0, j)),
            pl.BlockSpec((1, tf), lambda i, j: (0, j)),
        ],
        out_specs=pl.BlockSpec((tm, D_MODEL), lambda i, j: (i, 0)),
        out_shape=jax.ShapeDtypeStruct((rows, D_MODEL), F32),
        scratch_shapes=[pltpu.VMEM((tm + 2 * HALO, D_MODEL), BF16)],
        compiler_params=_cparams("parallel", "arbitrary"),
        name="ffn",
    )(xn, xn, xn, xn_meta, h1, wg, wu, wd, cw, cb)


def _rope_tables(pos):
    inv = ROPE_THETA ** (-jnp.arange(0, ROPE_DIM, 2, dtype=F32) / ROPE_DIM)
    ang = pos.astype(F32)[:, None] * inv[None, :]
    cos, sin = jnp.cos(ang), jnp.sin(ang)
    z = jnp.zeros_like(cos)
    return (jnp.concatenate([cos, cos, z, z], axis=-1),
            jnp.concatenate([-sin, sin, z, z], axis=-1), cos.T, sin.T)


def _softmax_shift(q_gain, k_gain):
    bound = LOG2E * QK_DIM ** 0.5 * jnp.max(jnp.abs(q_gain)) * jnp.max(jnp.abs(k_gain))
    return bound * (1.02 * (1.0 + 2.0 ** -7))


def _pad_lanes(v, width):
    return jnp.pad(v, ((0, 0), (0, width - v.shape[-1])))


def kernel(x, meta_tokens, mix_norm_g, w_in, na_q_g, na_k_g, na_rpb, na_meta_bias, mla_cq_g, mla_ckv_g,
           w_q_up, w_kv_up, mla_q_g, mla_k_g, na_out_g, mla_out_g, w_out, ffn_norm_g, w_gate, w_up,
           conv_w, conv_b, w_down):
    nb, seq, d = x.shape
    assert d == D_MODEL and w_in.shape[0] == 1, "one layer of width 2048"
    rows_grid = seq // GRID_W
    assert seq % max(IN_TM, UP_TM, ATT_TQ, FFN_TM) == 0 and rows_grid >= 4 * NA_QROWS

    w_in_p = _pad_lanes(w_in[0], 4 * IN_TN).astype(BF16)
    w4 = w_in_p.reshape(D_MODEL, 4, IN_TN).transpose(1, 0, 2)
    wqt = w_q_up[0].reshape(Q_RANK, HEADS, QK_DIM).transpose(1, 2, 0)
    wqt = jnp.pad(wqt, ((0, 0), (0, QK_PAD - QK_DIM), (0, 0))).astype(BF16)
    wkv = w_kv_up[0].reshape(KV_RANK, HEADS, 2 * HEAD_DIM)
    wk = wkv[:, :, :HEAD_DIM].reshape(KV_RANK, HEADS // 2, 2 * HEAD_DIM).transpose(1, 0, 2).astype(BF16)
    wvt = wkv[:, :, HEAD_DIM:].transpose(1, 2, 0).astype(BF16)
    wo = w_out[0].astype(BF16)
    wg, wu, wd = w_gate[0].astype(BF16), w_up[0].astype(BF16), w_down[0].astype(BF16)
    row = lambda v: v.reshape(1, -1)
    qg_pad, kg_pad = _pad_lanes(mla_q_g, QK_PAD), _pad_lanes(mla_k_g, QK_PAD)
    mb = jnp.pad(na_meta_bias[0] * LOG2E, ((0, 0), (0, META_ROWS - N_META)),
                 constant_values=NEG).reshape(HEADS, 1, META_ROWS)

    xr = x.reshape(nb * seq, D_MODEL)
    xm = jnp.pad(meta_tokens.astype(x.dtype), ((0, META_ROWS - N_META), (0, 0)))
    rope_r = _rope_tables(N_META + jnp.arange(seq))
    rope_m = _rope_tables(jnp.arange(META_ROWS))

    inproj = functools.partial(_inproj, g=row(mix_norm_g), w4=w4, qg=row(na_q_g), kg=row(na_k_g),
                               cqg=row(mla_cq_g), ckvg=row(mla_ckv_g))
    qa, ka, va, cq, ckv, kpe = inproj(xr, tm=IN_TM)
    qa_m, ka_m, va_m, cq_m, ckv_m, kpe_m = inproj(xm, tm=META_ROWS)

    up = functools.partial(_mla_up, wqt=wqt, wk=wk, wvt=wvt, qg=qg_pad, kg=kg_pad,
                           shift=_softmax_shift(mla_q_g, mla_k_g))
    qt, kk, vt = up(cq, ckv, kpe, *rope_r, nb=nb, tm=UP_TM)
    qt_m, kk_m, vt_m = up(cq_m, ckv_m, kpe_m, *rope_m, nb=1, tm=META_ROWS)

    bias = _na_bias(na_rpb[0] , rows_grid)
    a_n = _na_attn(qa, ka, va, ka_m, va_m, bias, mb, row(na_out_g), nb)
    a_n_m = _na_meta(qa_m, ka_m, va_m, mb, row(na_out_g))

    out_b = _mla_attn(qt, kk, vt, kk_m, vt_m, ATT_TQ, shared_q=False)
    out_b_m = _mla_attn(qt_m, kk, vt, kk_m, vt_m, META_ROWS, shared_q=True)

    op = functools.partial(_outproj, bg=row(mla_out_g), w=wo, fg=row(ffn_norm_g))
    ident = lambda i: (i, 0)
    zero = lambda i: (0, 0)
    h1, xn2 = op(xr, a_n, out_b.reshape(nb * seq, NA_WIDTH), tm=OUT_TM,
                 x_map=ident, a_map=ident, b_map=ident, rows=nb * seq)
    _, xn2_m = op(xm, a_n_m, out_b_m.reshape(nb * META_ROWS, NA_WIDTH), tm=META_ROWS,
                  x_map=zero, a_map=zero, b_map=ident, rows=nb * META_ROWS)

    out = _ffn(xn2, xn2_m, h1, wg, wu, wd, conv_w[0], row(conv_b), nb, FFN_TM, FFN_TF)
    return out.reshape(nb, seq, D_MODEL)
```

```python
import functools

import jax
import jax.numpy as jnp
import numpy as np
from jax import lax
from jax.experimental import pallas as pl
from jax.experimental.pallas import tpu as pltpu

F32 = jnp.float32
BF16 = jnp.bfloat16

LANES = 128
BF16_SUBLANES = 16
VMEM_LIMIT_BYTES = 56 * 1024 * 1024

D_MODEL = 2048
GRID_W = 64
N_META = 16
EPS = 1e-6
NEG = -1e30
LOG2E = 1.4426950408889634
HEADS = 8
HEAD_DIM = 128
NA_WIDTH = HEADS * HEAD_DIM
NA_WIN_ROWS = 8
NA_WIN_COLS = 16
Q_RANK = 512
KV_RANK = 256
ROPE_DIM = 64
QK_DIM = HEAD_DIM + ROPE_DIM
QK_PAD = 2 * LANES
ROPE_THETA = 10000.0
D_FF = 5632
META_ROWS = LANES
SOFTMAX_UNDERFLOW = 2.0 ** -100

IN_TM = 1024
IN_TN = 1024
UP_TM = 512
ATT_TQ = 1024
ATT_GROUP = 4
NA_QROWS = 4
NA_TQ = NA_QROWS * GRID_W
NA_KROWS = 3 * NA_QROWS
OUT_TM = 256
FFN_TM = 512
FFN_TF = 512
HALO = BF16_SUBLANES


def _cparams(*sem):
    return pltpu.CompilerParams(dimension_semantics=sem, vmem_limit_bytes=VMEM_LIMIT_BYTES)


def _rms(v, n=None):
    if n is None:
        ms = jnp.mean(v * v, axis=-1, keepdims=True)
    else:
        ms = jnp.sum(v * v, axis=-1, keepdims=True) * (1.0 / n)
    return v * lax.rsqrt(ms + EPS)


def _dot(a, b):
    return jnp.dot(a, b, preferred_element_type=F32)


def _inproj_body(x_ref, g_ref, w_ref, qg_ref, kg_ref, cqg_ref, ckvg_ref,
                 qa_ref, ka_ref, va_ref, cq_ref, ckv_ref, kpe_ref, xn_ref, *, q_scale):
    j = pl.program_id(1)

    @pl.when(j == 0)
    def _():
        xn_ref[...] = (_rms(x_ref[...]) * g_ref[...]).astype(BF16)

    def project(lo, hi):
        return _dot(xn_ref[...], w_ref[:, lo:hi])

    def head_norm(gain_ref, out_ref, post, transposed):
        for t in range(HEADS // 2):
            y = project(2 * t * HEAD_DIM, 2 * (t + 1) * HEAD_DIM)
            for u in range(2):
                sl = slice((2 * t + u) * HEAD_DIM, (2 * t + u + 1) * HEAD_DIM)
                yh = _rms(y[:, u * HEAD_DIM:(u + 1) * HEAD_DIM]) * gain_ref[...] * post
                if transposed:
                    out_ref[sl, :] = yh.T.astype(BF16)
                else:
                    out_ref[:, sl] = yh.astype(BF16)

    @pl.when(j == 0)
    def _():
        head_norm(qg_ref, qa_ref, q_scale, False)

    @pl.when(j == 1)
    def _():
        head_norm(kg_ref, ka_ref, 1.0, True)

    @pl.when(j == 2)
    def _():
        va_ref[...] = project(0, IN_TN).astype(BF16)

    @pl.when(j == 3)
    def _():
        cq_ref[...] = (_rms(project(0, Q_RANK)) * cqg_ref[...]).astype(BF16)
        y = project(Q_RANK, IN_TN)
        ckv_ref[...] = (_rms(y[:, :KV_RANK]) * ckvg_ref[...]).astype(BF16)
        kpe_ref[...] = y[:, KV_RANK:KV_RANK + LANES]


def _inproj(x, g, w4, qg, kg, cqg, ckvg, tm):
    rows = x.shape[0]
    row = lambda i, j: (i, 0)
    const = lambda i, j: (0, 0)
    outs = (
        jax.ShapeDtypeStruct((rows, NA_WIDTH), BF16),
        jax.ShapeDtypeStruct((NA_WIDTH, rows), BF16),
        jax.ShapeDtypeStruct((rows, NA_WIDTH), BF16),
        jax.ShapeDtypeStruct((rows, Q_RANK), BF16),
        jax.ShapeDtypeStruct((rows, KV_RANK), BF16),
        jax.ShapeDtypeStruct((rows, LANES), F32),
    )
    return pl.pallas_call(
        functools.partial(_inproj_body, q_scale=HEAD_DIM ** -0.5 * LOG2E),
        grid=(rows // tm, 4),
        in_specs=[
            pl.BlockSpec((tm, D_MODEL), row),
            pl.BlockSpec((1, D_MODEL), const),
            pl.BlockSpec((None, D_MODEL, IN_TN), lambda i, j: (j, 0, 0)),
            pl.BlockSpec((1, HEAD_DIM), const),
            pl.BlockSpec((1, HEAD_DIM), const),
            pl.BlockSpec((1, Q_RANK), const),
            pl.BlockSpec((1, KV_RANK), const),
        ],
        out_specs=[
            pl.BlockSpec((tm, NA_WIDTH), row),
            pl.BlockSpec((NA_WIDTH, tm), lambda i, j: (0, i)),
            pl.BlockSpec((tm, NA_WIDTH), row),
            pl.BlockSpec((tm, Q_RANK), row),
            pl.BlockSpec((tm, KV_RANK), row),
            pl.BlockSpec((tm, LANES), row),
        ],
        out_shape=outs,
        scratch_shapes=[pltpu.VMEM((tm, D_MODEL), BF16)],
        compiler_params=_cparams("parallel", "arbitrary"),
        name="inproj",
    )(x, g, w4, qg, kg, cqg, ckvg)


def _mla_up_body(cq_ref, ckv_ref, kpe_ref, cos_ref, sin_ref, cost_ref, sint_ref,
                 wqt_ref, wk_ref, wvt_ref, qgt_ref, qpad_ref, kg_ref, qt_ref, k_ref, vt_ref, *, q_scale):
    half = ROPE_DIM // 2
    ckv = ckv_ref[...]
    cqt = cq_ref[...].astype(F32).T.astype(BF16)
    ckvt = ckv.astype(F32).T.astype(BF16)
    qgt = qgt_ref[...]
    cost = cost_ref[...]
    sint = sint_ref[...]
    cosf = cos_ref[...]
    sinf = sin_ref[...]
    kg = kg_ref[...]

    kpe = kpe_ref[...]
    kpe_ss = jnp.sum(kpe * kpe, axis=-1, keepdims=True)
    r = kpe * kg[:, LANES:]
    kpe_rot = r * cosf + (pltpu.roll(r, half, 1) + pltpu.roll(r, LANES - half, 1)) * sinf

    for h in range(HEADS):
        q = _dot(wqt_ref[h], cqt)
        ms = jnp.sum(q * q, axis=0, keepdims=True) * (1.0 / QK_DIM)
        qn = q * (lax.rsqrt(ms + EPS) * q_scale) * qgt
        x1 = qn[LANES:LANES + half]
        x2 = qn[LANES + half:LANES + 2 * half]
        qt_ref[h, 0:LANES, :] = qn[:LANES].astype(BF16)
        qt_ref[h, LANES:LANES + half, :] = (x1 * cost - x2 * sint).astype(BF16)
        qt_ref[h, LANES + half:LANES + 2 * half, :] = (x2 * cost + x1 * sint).astype(BF16)
        qt_ref[h, LANES + 2 * half:, :] = qpad_ref[...]
        vt_ref[h] = _dot(wvt_ref[h], ckvt).astype(BF16)

    one_lane = (lax.broadcasted_iota(jnp.int32, (1, LANES), 1) == ROPE_DIM).astype(F32)
    for t in range(HEADS // 2):
        kn2 = _dot(ckv, wk_ref[t])
        for u in range(2):
            kn = kn2[:, u * LANES:(u + 1) * LANES]
            ms = (jnp.sum(kn * kn, axis=-1, keepdims=True) + kpe_ss) * (1.0 / QK_DIM)
            rn = lax.rsqrt(ms + EPS)
            k_ref[2 * t + u] = jnp.concatenate([kn * rn * kg[:, :LANES], kpe_rot * rn + one_lane],
                                               axis=-1).astype(BF16)


def _mla_up(cq, ckv, kpe, cosf, sinf, cost, sint, wqt, wk, wvt, qg, kg, shift, nb, tm):
    rows = cq.shape[0]
    nt = rows // nb // tm
    row = lambda b, i: (b * nt + i, 0)
    pos = lambda b, i: (i, 0)
    post = lambda b, i: (0, i)
    c2 = lambda b, i: (0, 0)
    c3 = lambda b, i: (0, 0, 0)
    qgt = jnp.broadcast_to(qg.reshape(QK_PAD, 1), (QK_PAD, tm))
    qpad = jnp.zeros((QK_PAD - QK_DIM, tm), F32).at[0].set(-shift).astype(BF16)
    outs = (
        jax.ShapeDtypeStruct((nb, HEADS, QK_PAD, nt * tm), BF16),
        jax.ShapeDtypeStruct((nb, HEADS, nt, tm, QK_PAD), BF16),
        jax.ShapeDtypeStruct((nb, HEADS, nt, HEAD_DIM, tm), BF16),
    )
    return pl.pallas_call(
        functools.partial(_mla_up_body, q_scale=QK_DIM ** -0.5 * LOG2E),
        grid=(nb, nt),
        in_specs=[
            pl.BlockSpec((tm, Q_RANK), row),
            pl.BlockSpec((tm, KV_RANK), row),
            pl.BlockSpec((tm, LANES), row),
            pl.BlockSpec((tm, LANES), pos),
            pl.BlockSpec((tm, LANES), pos),
            pl.BlockSpec((ROPE_DIM // 2, tm), post),
            pl.BlockSpec((ROPE_DIM // 2, tm), post),
            pl.BlockSpec((HEADS, QK_PAD, Q_RANK), c3),
            pl.BlockSpec((HEADS // 2, KV_RANK, 2 * HEAD_DIM), c3),
            pl.BlockSpec((HEADS, HEAD_DIM, KV_RANK), c3),
            pl.BlockSpec((QK_PAD, tm), c2),
            pl.BlockSpec((QK_PAD - QK_DIM, tm), c2),
            pl.BlockSpec((1, QK_PAD), c2),
        ],
        out_specs=[
            pl.BlockSpec((None, HEADS, QK_PAD, tm), lambda b, i: (b, 0, 0, i)),
            pl.BlockSpec((None, HEADS, None, tm, QK_PAD), lambda b, i: (b, 0, i, 0, 0)),
            pl.BlockSpec((None, HEADS, None, HEAD_DIM, tm), lambda b, i: (b, 0, i, 0, 0)),
        ],
        out_shape=outs,
        compiler_params=_cparams("parallel", "parallel"),
        name="mla_up",
    )(cq, ckv, kpe, cosf, sinf, cost, sint, wqt, wk, wvt, qgt, qpad, kg)


def _mla_attn_body(qt_ref, qtn_ref, k_ref, vt_ref, km_ref, vmt_ref, o_ref, acc_ref, s_ref,
                   *, nk, carry_over):
    qt = qt_ref[...]
    ng = nk // ATT_GROUP

    def scores(g, half, q=None):
        q = qt if q is None else q
        for u in range(ATT_GROUP):
            s_ref[half * ATT_GROUP + u] = _dot(k_ref[g * ATT_GROUP + u], q)

    def attend(g, half, l):
        pv = None
        for u in range(ATT_GROUP):
            p = jnp.exp2(s_ref[half * ATT_GROUP + u])
            l = l + jnp.sum(p, axis=0, keepdims=True)
            d = _dot(vt_ref[g * ATT_GROUP + u], p.astype(BF16))
            pv = d if pv is None else pv + d
        acc_ref[...] += pv
        return l

    def meta_scores():
        s = _dot(km_ref[...], qt)
        key = lax.broadcasted_iota(jnp.int32, s.shape, 0)
        return jnp.where(key < N_META, s, NEG)

    def two_stages(i, l):
        scores(2 * i + 1, 1)
        l = attend(2 * i, 0, l)
        scores(2 * i + 2, 0)
        return attend(2 * i + 1, 1, l)

    if carry_over:
        @pl.when(pl.program_id(2) == 0)
        def _():
            scores(0, 0)
    else:
        scores(0, 0)

    p = jnp.exp2(meta_scores())
    acc_ref[...] = _dot(vmt_ref[...], p.astype(BF16))
    l = lax.fori_loop(0, ng // 2 - 1, two_stages, jnp.sum(p, axis=0, keepdims=True))
    scores(ng - 1, 1)
    l = attend(ng - 2, 0, l)
    if carry_over:
        scores(0, 0, qtn_ref[...])
    l = attend(ng - 1, 1, l)
    o_ref[...] = (acc_ref[...] / l).T

    @pl.when(jnp.min(l) < SOFTMAX_UNDERFLOW)
    def _():
        s = meta_scores()
        m0 = jnp.max(s, axis=0, keepdims=True)
        p = jnp.exp2(s - m0)
        acc_ref[...] = _dot(vmt_ref[...], p.astype(BF16))

        def chunk(c, carry):
            m_prev, l_prev = carry
            s = _dot(k_ref[c], qt)
            m_new = jnp.maximum(m_prev, jnp.max(s, axis=0, keepdims=True))
            alpha = jnp.exp2(m_prev - m_new)
            p = jnp.exp2(s - m_new)
            acc_ref[...] = alpha * acc_ref[...] + _dot(vt_ref[c], p.astype(BF16))
            return m_new, alpha * l_prev + jnp.sum(p, axis=0, keepdims=True)

        _, l_exact = lax.fori_loop(0, nk, chunk, (m0, jnp.sum(p, axis=0, keepdims=True)))
        o_ref[...] = (acc_ref[...] / l_exact).T


def _mla_attn(qt, k, vt, km, vmt, tq, shared_q):
    nb, _, nk, tk, _ = k.shape
    nq = qt.shape[3] // tq
    qb = (lambda b: 0) if shared_q else (lambda b: b)
    return pl.pallas_call(
        functools.partial(_mla_attn_body, nk=nk, carry_over=nq > 1),
        grid=(nb, HEADS, nq),
        in_specs=[
            pl.BlockSpec((None, None, QK_PAD, tq), lambda b, h, i: (qb(b), h, 0, i)),
            pl.BlockSpec((None, None, QK_PAD, tq), lambda b, h, i: (qb(b), h, 0, jnp.minimum(i + 1, nq - 1))),
            pl.BlockSpec((None, None, nk, tk, QK_PAD), lambda b, h, i: (b, h, 0, 0, 0),
                         pipeline_mode=pl.Buffered(1)),
            pl.BlockSpec((None, None, nk, HEAD_DIM, tk), lambda b, h, i: (b, h, 0, 0, 0),
                         pipeline_mode=pl.Buffered(1)),
            pl.BlockSpec((None, None, None, META_ROWS, QK_PAD), lambda b, h, i: (0, h, 0, 0, 0)),
            pl.BlockSpec((None, None, None, HEAD_DIM, META_ROWS), lambda b, h, i: (0, h, 0, 0, 0)),
        ],
        out_specs=pl.BlockSpec((None, tq, HEAD_DIM), lambda b, h, i: (b, i, h)),
        out_shape=jax.ShapeDtypeStruct((nb, nq * tq, NA_WIDTH), F32),
        scratch_shapes=[pltpu.VMEM((HEAD_DIM, tq), F32), pltpu.VMEM((2 * ATT_GROUP, tk, tq), F32)],
        compiler_params=_cparams("parallel", "parallel", "arbitrary"),
        name="mla_attn",
    )(qt, qt, k, vt, km, vmt)


def _na_bias_body(rpb_ref, o_ref, t_ref, *, rows):
    h = pl.program_id(0)
    n_dr = 2 * NA_WIN_ROWS - 1
    n_dc = 2 * NA_WIN_COLS - 1
    shape = (GRID_W, LANES)
    qc = lax.broadcasted_iota(jnp.int32, shape, 0)
    lane = lax.broadcasted_iota(jnp.int32, shape, 1)
    left = lane < GRID_W
    kc = jnp.where(left, lane, lane - GRID_W)
    cs = jnp.clip(qc - NA_WIN_COLS // 2, 0, GRID_W - NA_WIN_COLS)
    col_ok = (kc >= cs) & (kc < cs + NA_WIN_COLS)
    dc = kc - qc + NA_WIN_COLS - 1

    def build(dr, carry):
        def pick(j, acc):
            return jnp.where(dc == j, rpb_ref[h, dr, j], acc)

        tile = lax.fori_loop(0, n_dc, pick, jnp.zeros(shape, F32))
        t_ref[dr] = jnp.where(col_ok, tile * LOG2E, NEG)
        return carry

    lax.fori_loop(0, n_dr, build, 0)

    neg = jnp.full(shape, NEG, F32)
    for c, r0 in enumerate((0, 2 * NA_QROWS, rows - NA_QROWS)):
        for qr in range(NA_QROWS):
            abs_qr = r0 + qr
            rs = min(max(abs_qr - NA_WIN_ROWS // 2, 0), rows - NA_WIN_ROWS)
            for a in range(NA_KROWS // 2):
                halves = []
                for kr in (r0 - NA_QROWS + 2 * a, r0 - NA_QROWS + 2 * a + 1):
                    in_window = rs <= kr < rs + NA_WIN_ROWS
                    halves.append(t_ref[kr - abs_qr + NA_WIN_ROWS - 1] if in_window else neg)
                o_ref[c, qr * GRID_W:(qr + 1) * GRID_W, a * LANES:(a + 1) * LANES] = (
                    jnp.where(left, halves[0], halves[1]))


def _na_bias(rpb, rows):
    return pl.pallas_call(
        functools.partial(_na_bias_body, rows=rows),
        grid=(HEADS,),
        in_specs=[pl.BlockSpec(memory_space=pltpu.SMEM)],
        out_specs=pl.BlockSpec((3, None, NA_TQ, NA_KROWS * GRID_W), lambda h: (0, h, 0, 0)),
        out_shape=jax.ShapeDtypeStruct((3, HEADS, NA_TQ, NA_KROWS * GRID_W), F32),
        scratch_shapes=[pltpu.VMEM((2 * NA_WIN_ROWS - 1, GRID_W, LANES), F32)],
        compiler_params=_cparams("parallel"),
        name="na_bias",
    )(rpb)


def _na_body(q_ref, kp_ref, kc_ref, kn_ref, vp_ref, vc_ref, vn_ref, km_ref, vm_ref,
             bias_ref, mb_ref, g_ref, o_ref, out_ref, s_ref):
    for h in range(HEADS):
        sl = slice(h * HEAD_DIM, (h + 1) * HEAD_DIM)
        q = q_ref[:, sl]
        for c, k_ref in enumerate((kp_ref, kc_ref, kn_ref)):
            cols = slice(c * NA_TQ, (c + 1) * NA_TQ)
            s_ref[h, :, cols] = _dot(q, k_ref[sl, :]) + bias_ref[h, :, cols]
        s_ref[h, :, 3 * NA_TQ:] = _dot(q, km_ref[sl, :]) + mb_ref[h]
    for h in range(HEADS):
        sl = slice(h * HEAD_DIM, (h + 1) * HEAD_DIM)
        s = s_ref[h]
        m = jnp.max(s, axis=-1, keepdims=True)
        p = jnp.exp2(s - m)
        l = jnp.sum(p, axis=-1, keepdims=True)
        pb = p.astype(BF16)
        o = (_dot(pb[:, 3 * NA_TQ:], vm_ref[:, sl])
             + _dot(pb[:, :NA_TQ], vp_ref[:, sl])
             + _dot(pb[:, NA_TQ:2 * NA_TQ], vc_ref[:, sl])
             + _dot(pb[:, 2 * NA_TQ:3 * NA_TQ], vn_ref[:, sl]))
        out_ref[:, sl] = o / l
    o_ref[...] = (_rms(out_ref[...]) * g_ref[...]).astype(BF16)


def _na_attn(qa, kat, va, kmt, vm, bias, mb, g, nb):
    rows = qa.shape[0]
    n = rows // nb // NA_TQ
    cur = lambda b, i: (b * n + i, 0)
    prev = lambda b, i: (b * n + jnp.maximum(i - 1, 0), 0)
    nxt = lambda b, i: (b * n + jnp.minimum(i + 1, n - 1), 0)
    blk = lambda f: pl.BlockSpec((NA_TQ, NA_WIDTH), f)
    blk_t = lambda f: pl.BlockSpec((NA_WIDTH, NA_TQ), lambda b, i: f(b, i)[::-1])
    case = lambda b, i: (jnp.where(i == 0, 0, jnp.where(i == n - 1, 2, 1)), 0, 0, 0)
    return pl.pallas_call(
        _na_body,
        grid=(nb, n),
        in_specs=[
            blk(cur), blk_t(prev), blk_t(cur), blk_t(nxt), blk(prev), blk(cur), blk(nxt),
            pl.BlockSpec((NA_WIDTH, META_ROWS), lambda b, i: (0, 0)),
            pl.BlockSpec((META_ROWS, NA_WIDTH), lambda b, i: (0, 0)),
            pl.BlockSpec((None, HEADS, NA_TQ, NA_KROWS * GRID_W), case),
            pl.BlockSpec((HEADS, 1, META_ROWS), lambda b, i: (0, 0, 0)),
            pl.BlockSpec((1, NA_WIDTH), lambda b, i: (0, 0)),
        ],
        out_specs=blk(cur),
        out_shape=jax.ShapeDtypeStruct((rows, NA_WIDTH), BF16),
        scratch_shapes=[pltpu.VMEM((NA_TQ, NA_WIDTH), F32),
                        pltpu.VMEM((HEADS, NA_TQ, NA_KROWS * GRID_W + META_ROWS), F32)],
        compiler_params=_cparams("parallel", "arbitrary"),
        name="na_attn",
    )(qa, kat, kat, kat, va, va, va, kmt, vm, bias, mb, g)


def _na_meta_body(q_ref, kt_ref, v_ref, mb_ref, g_ref, o_ref, out_ref):
    for h in range(HEADS):
        sl = slice(h * HEAD_DIM, (h + 1) * HEAD_DIM)
        s = _dot(q_ref[:, sl], kt_ref[sl, :]) + mb_ref[h]
        m = jnp.max(s, axis=-1, keepdims=True)
        p = jnp.exp2(s - m)
        l = jnp.sum(p, axis=-1, keepdims=True)
        out_ref[:, sl] = _dot(p.astype(BF16), v_ref[:, sl]) / l
    o_ref[...] = (_rms(out_ref[...]) * g_ref[...]).astype(BF16)


def _na_meta(qm, kmt, vm, mb, g):
    full = lambda s: pl.BlockSpec(s, lambda i: (0,) * len(s))
    return pl.pallas_call(
        _na_meta_body,
        grid=(1,),
        in_specs=[full((META_ROWS, NA_WIDTH)), full((NA_WIDTH, META_ROWS)), full((META_ROWS, NA_WIDTH)),
                  full((HEADS, 1, META_ROWS)), full((1, NA_WIDTH))],
        out_specs=full((META_ROWS, NA_WIDTH)),
        out_shape=jax.ShapeDtypeStruct((META_ROWS, NA_WIDTH), BF16),
        scratch_shapes=[pltpu.VMEM((META_ROWS, NA_WIDTH), F32)],
        compiler_params=_cparams("arbitrary"),
        name="na_meta",
    )(qm, kmt, vm, mb, g)


def _outproj_body(x_ref, a_ref, b_ref, bg_ref, w_ref, fg_ref, h_ref, xn_ref):
    bn = (_rms(b_ref[...]) * bg_ref[...]).astype(BF16)
    mix = jnp.concatenate([a_ref[...], bn], axis=-1)
    h = x_ref[...] + _dot(mix, w_ref[...])
    h_ref[...] = h
    xn_ref[...] = (_rms(h) * fg_ref[...]).astype(BF16)


def _outproj(x, a, b, bg, w, fg, tm, x_map, a_map, b_map, rows):
    c = lambda i: (0, 0)
    return pl.pallas_call(
        _outproj_body,
        grid=(rows // tm,),
        in_specs=[
            pl.BlockSpec((tm, D_MODEL), x_map),
            pl.BlockSpec((tm, NA_WIDTH), a_map),
            pl.BlockSpec((tm, NA_WIDTH), b_map),
            pl.BlockSpec((1, NA_WIDTH), c),
            pl.BlockSpec((2 * NA_WIDTH, D_MODEL), c),
            pl.BlockSpec((1, D_MODEL), c),
        ],
        out_specs=[pl.BlockSpec((tm, D_MODEL), lambda i: (i, 0)),
                   pl.BlockSpec((tm, D_MODEL), lambda i: (i, 0))],
        out_shape=(jax.ShapeDtypeStruct((rows, D_MODEL), F32),
                   jax.ShapeDtypeStruct((rows, D_MODEL), BF16)),
        compiler_params=_cparams("parallel"),
        name="outproj",
    )(x, a, b, bg, w, fg)


def _ffn_body(xm_ref, xp_ref, xx_ref, xmeta_ref, h_ref, wg_ref, wu_ref, wd_ref, cw_ref, cb_ref,
              o_ref, xe_ref, *, tm, tpb):
    i = pl.program_id(0)
    j = pl.program_id(1)

    @pl.when(j == 0)
    def _():
        first = (i % tpb) == 0
        last = (i % tpb) == tpb - 1
        xe_ref[0:HALO, :] = jnp.where(first, xmeta_ref[...], xp_ref[...])
        xe_ref[HALO:HALO + tm, :] = xm_ref[...]
        xe_ref[HALO + tm:2 * HALO + tm, :] = jnp.where(last, jnp.zeros_like(xx_ref[...]), xx_ref[...])
        o_ref[...] = h_ref[...]

    g = _dot(xe_ref[...], wg_ref[...])
    u = _dot(xe_ref[HALO:HALO + tm, :], wu_ref[...])
    cw = cw_ref[...]
    gc = (cb_ref[...] + cw[0:1] * g[HALO - 1:HALO - 1 + tm]
          + cw[1:2] * g[HALO:HALO + tm] + cw[2:3] * g[HALO + 1:HALO + 1 + tm])
    act = gc * (1.0 / (1.0 + jnp.exp(-gc))) * u
    o_ref[...] += _dot(act.astype(BF16), wd_ref[...])


def _ffn(xn, xn_meta, h1, wg, wu, wd, cw, cb, nb, tm, tf):
    rows = xn.shape[0]
    nt = rows // tm
    tpb = nt // nb
    hb = tm // HALO
    last_hb = rows // HALO - 1
    return pl.pallas_call(
        functools.partial(_ffn_body, tm=tm, tpb=tpb),
        grid=(nt, D_FF // tf),
        in_specs=[
            pl.BlockSpec((tm, D_MODEL), lambda i, j: (i, 0)),
            pl.BlockSpec((HALO, D_MODEL), lambda i, j: (jnp.maximum(i * hb - 1, 0), 0)),
            pl.BlockSpec((HALO, D_MODEL), lambda i, j: (jnp.minimum((i + 1) * hb, last_hb), 0)),
            pl.BlockSpec((HALO, D_MODEL), lambda i, j: ((i // tpb) * (META_ROWS // HALO), 0)),
            pl.BlockSpec((tm, D_MODEL), lambda i, j: (i, 0)),
            pl.BlockSpec((D_MODEL, tf), lambda i, j: (0, j)),
            pl.BlockSpec((D_MODEL, tf), lambda i, j: (0, j)),
            pl.BlockSpec((tf, D_MODEL), lambda i, j: (j, 0)),
            pl.BlockSpec((3, tf), lambda i, j: (0, j)),
            pl.BlockSpec((1, tf), lambda i, j: (0, j)),
        ],
        out_specs=pl.BlockSpec((tm, D_MODEL), lambda i, j: (i, 0)),
        out_shape=jax.ShapeDtypeStruct((rows, D_MODEL), F32),
        scratch_shapes=[pltpu.VMEM((tm + 2 * HALO, D_MODEL), BF16)],
        compiler_params=_cparams("parallel", "arbitrary"),
        name="ffn",
    )(xn, xn, xn, xn_meta, h1, wg, wu, wd, cw, cb)


def _rope_tables(first, n):
    inv = ROPE_THETA ** (-np.arange(0, ROPE_DIM, 2, dtype=np.float64) / ROPE_DIM)
    ang = (first + np.arange(n, dtype=np.float64))[:, None] * inv[None, :]
    cos, sin = np.cos(ang), np.sin(ang)
    z = np.zeros_like(cos)
    tables = (np.concatenate([cos, cos, z, z], axis=-1), np.concatenate([-sin, sin, z, z], axis=-1), cos.T, sin.T)
    return tuple(jnp.asarray(t, F32) for t in tables)


def _softmax_shift(q_gain, k_gain):
    bound = LOG2E * QK_DIM ** 0.5 * jnp.max(jnp.abs(q_gain)) * jnp.max(jnp.abs(k_gain))
    return bound * (1.02 * (1.0 + 2.0 ** -7))


def _pad_lanes(v, width):
    return jnp.pad(v, ((0, 0), (0, width - v.shape[-1])))


def kernel(x, meta_tokens, mix_norm_g, w_in, na_q_g, na_k_g, na_rpb, na_meta_bias, mla_cq_g, mla_ckv_g,
           w_q_up, w_kv_up, mla_q_g, mla_k_g, na_out_g, mla_out_g, w_out, ffn_norm_g, w_gate, w_up,
           conv_w, conv_b, w_down):
    nb, seq, d = x.shape
    assert d == D_MODEL and w_in.shape[0] == 1, "one layer of width 2048"
    rows_grid = seq // GRID_W
    assert seq % max(IN_TM, UP_TM, ATT_TQ, FFN_TM) == 0 and rows_grid >= 4 * NA_QROWS

    w_in_p = _pad_lanes(w_in[0], 4 * IN_TN).astype(BF16)
    w4 = w_in_p.reshape(D_MODEL, 4, IN_TN).transpose(1, 0, 2)
    wqt = w_q_up[0].reshape(Q_RANK, HEADS, QK_DIM).transpose(1, 2, 0)
    wqt = jnp.pad(wqt, ((0, 0), (0, QK_PAD - QK_DIM), (0, 0))).astype(BF16)
    wkv = w_kv_up[0].reshape(KV_RANK, HEADS, 2 * HEAD_DIM)
    wk = wkv[:, :, :HEAD_DIM].reshape(KV_RANK, HEADS // 2, 2 * HEAD_DIM).transpose(1, 0, 2).astype(BF16)
    wvt = wkv[:, :, HEAD_DIM:].transpose(1, 2, 0).astype(BF16)
    wo = w_out[0].astype(BF16)
    wg, wu, wd = w_gate[0].astype(BF16), w_up[0].astype(BF16), w_down[0].astype(BF16)
    row = lambda v: v.reshape(1, -1)
    qg_pad, kg_pad = _pad_lanes(mla_q_g, QK_PAD), _pad_lanes(mla_k_g, QK_PAD)
    mb = jnp.pad(na_meta_bias[0] * LOG2E, ((0, 0), (0, META_ROWS - N_META)),
                 constant_values=NEG).reshape(HEADS, 1, META_ROWS)

    xr = x.reshape(nb * seq, D_MODEL)
    xm = jnp.pad(meta_tokens.astype(x.dtype), ((0, META_ROWS - N_META), (0, 0)))
    rope_r = _rope_tables(N_META, seq)
    rope_m = _rope_tables(0, META_ROWS)

    inproj = functools.partial(_inproj, g=row(mix_norm_g), w4=w4, qg=row(na_q_g), kg=row(na_k_g),
                               cqg=row(mla_cq_g), ckvg=row(mla_ckv_g))
    qa, ka, va, cq, ckv, kpe = inproj(xr, tm=IN_TM)
    qa_m, ka_m, va_m, cq_m, ckv_m, kpe_m = inproj(xm, tm=META_ROWS)

    up = functools.partial(_mla_up, wqt=wqt, wk=wk, wvt=wvt, qg=qg_pad, kg=kg_pad,
                           shift=_softmax_shift(mla_q_g, mla_k_g))
    qt, kk, vt = up(cq, ckv, kpe, *rope_r, nb=nb, tm=UP_TM)
    qt_m, kk_m, vt_m = up(cq_m, ckv_m, kpe_m, *rope_m, nb=1, tm=META_ROWS)

    bias = _na_bias(na_rpb[0] , rows_grid)
    a_n = _na_attn(qa, ka, va, ka_m, va_m, bias, mb, row(na_out_g), nb)
    a_n_m = _na_meta(qa_m, ka_m, va_m, mb, row(na_out_g))

    out_b = _mla_attn(qt, kk, vt, kk_m, vt_m, ATT_TQ, shared_q=False)
    out_b_m = _mla_attn(qt_m, kk, vt, kk_m, vt_m, META_ROWS, shared_q=True)

    op = functools.partial(_outproj, bg=row(mla_out_g), w=wo, fg=row(ffn_norm_g))
    ident = lambda i: (i, 0)
    zero = lambda i: (0, 0)
    h1, xn2 = op(xr, a_n, out_b.reshape(nb * seq, NA_WIDTH), tm=OUT_TM,
                 x_map=ident, a_map=ident, b_map=ident, rows=nb * seq)
    _, xn2_m = op(xm, a_n_m, out_b_m.reshape(nb * META_ROWS, NA_WIDTH), tm=META_ROWS,
                  x_map=zero, a_map=zero, b_map=ident, rows=nb * META_ROWS)

    out = _ffn(xn2, xn2_m, h1, wg, wu, wd, conv_w[0], row(conv_b), nb, FFN_TM, FFN_TF)
    return out.reshape(nb, seq, D_MODEL)
```

```python
import functools

import jax
import jax.numpy as jnp
import numpy as np
from jax import lax
from jax.experimental import pallas as pl
from jax.experimental.pallas import tpu as pltpu

F32 = jnp.float32
BF16 = jnp.bfloat16

LANES = 128
BF16_SUBLANES = 16
VMEM_LIMIT_BYTES = 56 * 1024 * 1024

D_MODEL = 2048
GRID_W = 64
N_META = 16
EPS = 1e-6
NEG = -1e30
LOG2E = 1.4426950408889634
HEADS = 8
HEAD_DIM = 128
NA_WIDTH = HEADS * HEAD_DIM
NA_WIN_ROWS = 8
NA_WIN_COLS = 16
Q_RANK = 512
KV_RANK = 256
ROPE_DIM = 64
QK_DIM = HEAD_DIM + ROPE_DIM
QK_PAD = 2 * LANES
ROPE_THETA = 10000.0
D_FF = 5632
META_ROWS = LANES
SOFTMAX_UNDERFLOW = 2.0 ** -100

IN_TM = 1024
IN_TN = 1024
UP_TM = 512
ATT_TQ = 1024
ATT_GROUP = 4
NA_QROWS = 4
NA_TQ = NA_QROWS * GRID_W
NA_KROWS = 3 * NA_QROWS
OUT_TM = 256
FFN_TM = 512
FFN_TF = 512
HALO = BF16_SUBLANES


def _cparams(*sem):
    return pltpu.CompilerParams(dimension_semantics=sem, vmem_limit_bytes=VMEM_LIMIT_BYTES)


def _rms(v):
    return v * lax.rsqrt(jnp.mean(v * v, axis=-1, keepdims=True) + EPS)


def _dot(a, b):
    return jnp.dot(a, b, preferred_element_type=F32)


def _inproj_body(x_ref, g_ref, w_ref, qg_ref, kg_ref, cqg_ref, ckvg_ref,
                 qa_ref, ka_ref, va_ref, cq_ref, ckv_ref, kpe_ref, xn_ref, *, q_scale):
    j = pl.program_id(1)

    @pl.when(j == 0)
    def _():
        xn_ref[...] = (_rms(x_ref[...]) * g_ref[...]).astype(BF16)

    def project(lo, hi):
        return _dot(xn_ref[...], w_ref[:, lo:hi])

    def head_norm(gain_ref, out_ref, post, transposed):
        for t in range(HEADS // 2):
            y = project(2 * t * HEAD_DIM, 2 * (t + 1) * HEAD_DIM)
            for u in range(2):
                sl = slice((2 * t + u) * HEAD_DIM, (2 * t + u + 1) * HEAD_DIM)
                yh = _rms(y[:, u * HEAD_DIM:(u + 1) * HEAD_DIM]) * gain_ref[...] * post
                if transposed:
                    out_ref[sl, :] = yh.T.astype(BF16)
                else:
                    out_ref[:, sl] = yh.astype(BF16)

    @pl.when(j == 0)
    def _():
        head_norm(qg_ref, qa_ref, q_scale, False)

    @pl.when(j == 1)
    def _():
        head_norm(kg_ref, ka_ref, 1.0, True)

    @pl.when(j == 2)
    def _():
        va_ref[...] = project(0, IN_TN).astype(BF16)

    @pl.when(j == 3)
    def _():
        cq_ref[...] = (_rms(project(0, Q_RANK)) * cqg_ref[...]).astype(BF16)
        y = project(Q_RANK, IN_TN)
        ckv_ref[...] = (_rms(y[:, :KV_RANK]) * ckvg_ref[...]).astype(BF16)
        kpe_ref[...] = y[:, KV_RANK:KV_RANK + LANES]


def _inproj(x, g, w4, qg, kg, cqg, ckvg, tm):
    rows = x.shape[0]
    row = lambda i, j: (i, 0)
    const = lambda i, j: (0, 0)
    outs = (
        jax.ShapeDtypeStruct((rows, NA_WIDTH), BF16),
        jax.ShapeDtypeStruct((NA_WIDTH, rows), BF16),
        jax.ShapeDtypeStruct((rows, NA_WIDTH), BF16),
        jax.ShapeDtypeStruct((rows, Q_RANK), BF16),
        jax.ShapeDtypeStruct((rows, KV_RANK), BF16),
        jax.ShapeDtypeStruct((rows, LANES), F32),
    )
    return pl.pallas_call(
        functools.partial(_inproj_body, q_scale=HEAD_DIM ** -0.5 * LOG2E),
        grid=(rows // tm, 4),
        in_specs=[
            pl.BlockSpec((tm, D_MODEL), row),
            pl.BlockSpec((1, D_MODEL), const),
            pl.BlockSpec((None, D_MODEL, IN_TN), lambda i, j: (j, 0, 0)),
            pl.BlockSpec((1, HEAD_DIM), const),
            pl.BlockSpec((1, HEAD_DIM), const),
            pl.BlockSpec((1, Q_RANK), const),
            pl.BlockSpec((1, KV_RANK), const),
        ],
        out_specs=[
            pl.BlockSpec((tm, NA_WIDTH), row),
            pl.BlockSpec((NA_WIDTH, tm), lambda i, j: (0, i)),
            pl.BlockSpec((tm, NA_WIDTH), row),
            pl.BlockSpec((tm, Q_RANK), row),
            pl.BlockSpec((tm, KV_RANK), row),
            pl.BlockSpec((tm, LANES), row),
        ],
        out_shape=outs,
        scratch_shapes=[pltpu.VMEM((tm, D_MODEL), BF16)],
        compiler_params=_cparams("parallel", "arbitrary"),
        name="inproj",
    )(x, g, w4, qg, kg, cqg, ckvg)


def _mla_up_body(cq_ref, ckv_ref, kpe_ref, cos_ref, sin_ref, cost_ref, sint_ref,
                 wqt_ref, wk_ref, wvt_ref, qgt_ref, qpad_ref, kg_ref, qt_ref, k_ref, vt_ref, *, q_scale):
    half = ROPE_DIM // 2
    ckv = ckv_ref[...]
    cqt = cq_ref[...].astype(F32).T.astype(BF16)
    ckvt = ckv.astype(F32).T.astype(BF16)
    qgt = qgt_ref[...]
    cost = cost_ref[...]
    sint = sint_ref[...]
    cosf = cos_ref[...]
    sinf = sin_ref[...]
    kg = kg_ref[...]

    kpe = kpe_ref[...]
    kpe_ss = jnp.sum(kpe * kpe, axis=-1, keepdims=True)
    r = kpe * kg[:, LANES:]
    kpe_rot = r * cosf + (pltpu.roll(r, half, 1) + pltpu.roll(r, LANES - half, 1)) * sinf

    for h in range(HEADS):
        q = _dot(wqt_ref[h], cqt)
        ms = jnp.sum(q * q, axis=0, keepdims=True) * (1.0 / QK_DIM)
        qn = q * (lax.rsqrt(ms + EPS) * q_scale) * qgt
        x1 = qn[LANES:LANES + half]
        x2 = qn[LANES + half:LANES + 2 * half]
        qt_ref[h, 0:LANES, :] = qn[:LANES].astype(BF16)
        qt_ref[h, LANES:LANES + half, :] = (x1 * cost - x2 * sint).astype(BF16)
        qt_ref[h, LANES + half:LANES + 2 * half, :] = (x2 * cost + x1 * sint).astype(BF16)
        qt_ref[h, LANES + 2 * half:, :] = qpad_ref[...]
        vt_ref[h] = _dot(wvt_ref[h], ckvt).astype(BF16)

    one_lane = (lax.broadcasted_iota(jnp.int32, (1, LANES), 1) == ROPE_DIM).astype(F32)
    for t in range(HEADS // 2):
        kn2 = _dot(ckv, wk_ref[t])
        for u in range(2):
            kn = kn2[:, u * LANES:(u + 1) * LANES]
            ms = (jnp.sum(kn * kn, axis=-1, keepdims=True) + kpe_ss) * (1.0 / QK_DIM)
            rn = lax.rsqrt(ms + EPS)
            k_ref[2 * t + u] = jnp.concatenate([kn * rn * kg[:, :LANES], kpe_rot * rn + one_lane],
                                               axis=-1).astype(BF16)


def _mla_up(cq, ckv, kpe, cosf, sinf, cost, sint, wqt, wk, wvt, qg, kg, shift, nb, tm):
    rows = cq.shape[0]
    nt = rows // nb // tm
    row = lambda b, i: (b * nt + i, 0)
    pos = lambda b, i: (i, 0)
    post = lambda b, i: (0, i)
    c2 = lambda b, i: (0, 0)
    c3 = lambda b, i: (0, 0, 0)
    qgt = jnp.broadcast_to(qg.reshape(QK_PAD, 1), (QK_PAD, tm))
    qpad = jnp.zeros((QK_PAD - QK_DIM, tm), F32).at[0].set(-shift).astype(BF16)
    outs = (
        jax.ShapeDtypeStruct((nb, HEADS, QK_PAD, nt * tm), BF16),
        jax.ShapeDtypeStruct((nb, HEADS, nt, tm, QK_PAD), BF16),
        jax.ShapeDtypeStruct((nb, HEADS, nt, HEAD_DIM, tm), BF16),
    )
    return pl.pallas_call(
        functools.partial(_mla_up_body, q_scale=QK_DIM ** -0.5 * LOG2E),
        grid=(nb, nt),
        in_specs=[
            pl.BlockSpec((tm, Q_RANK), row),
            pl.BlockSpec((tm, KV_RANK), row),
            pl.BlockSpec((tm, LANES), row),
            pl.BlockSpec((tm, LANES), pos),
            pl.BlockSpec((tm, LANES), pos),
            pl.BlockSpec((ROPE_DIM // 2, tm), post),
            pl.BlockSpec((ROPE_DIM // 2, tm), post),
            pl.BlockSpec((HEADS, QK_PAD, Q_RANK), c3),
            pl.BlockSpec((HEADS // 2, KV_RANK, 2 * HEAD_DIM), c3),
            pl.BlockSpec((HEADS, HEAD_DIM, KV_RANK), c3),
            pl.BlockSpec((QK_PAD, tm), c2),
            pl.BlockSpec((QK_PAD - QK_DIM, tm), c2),
            pl.BlockSpec((1, QK_PAD), c2),
        ],
        out_specs=[
            pl.BlockSpec((None, HEADS, QK_PAD, tm), lambda b, i: (b, 0, 0, i)),
            pl.BlockSpec((None, HEADS, None, tm, QK_PAD), lambda b, i: (b, 0, i, 0, 0)),
            pl.BlockSpec((None, HEADS, None, HEAD_DIM, tm), lambda b, i: (b, 0, i, 0, 0)),
        ],
        out_shape=outs,
        compiler_params=_cparams("parallel", "parallel"),
        name="mla_up",
    )(cq, ckv, kpe, cosf, sinf, cost, sint, wqt, wk, wvt, qgt, qpad, kg)


def _mla_attn_body(qt_ref, qtn_ref, k_ref, vt_ref, km_ref, vmt_ref, o_ref, acc_ref, s_ref,
                   *, nk, carry_over):
    qt = qt_ref[...]
    ng = nk // ATT_GROUP

    def scores(g, half, q=None):
        q = qt if q is None else q
        for u in range(ATT_GROUP):
            s_ref[half * ATT_GROUP + u] = _dot(k_ref[g * ATT_GROUP + u], q)

    def attend(g, half, l):
        pv = None
        for u in range(ATT_GROUP):
            p = jnp.exp2(s_ref[half * ATT_GROUP + u])
            l = l + jnp.sum(p, axis=0, keepdims=True)
            d = _dot(vt_ref[g * ATT_GROUP + u], p.astype(BF16))
            pv = d if pv is None else pv + d
        acc_ref[...] += pv
        return l

    def meta_scores():
        s = _dot(km_ref[...], qt)
        key = lax.broadcasted_iota(jnp.int32, s.shape, 0)
        return jnp.where(key < N_META, s, NEG)

    def two_stages(i, l):
        scores(2 * i + 1, 1)
        l = attend(2 * i, 0, l)
        scores(2 * i + 2, 0)
        return attend(2 * i + 1, 1, l)

    if carry_over:
        @pl.when(pl.program_id(2) == 0)
        def _():
            scores(0, 0)
    else:
        scores(0, 0)

    p = jnp.exp2(meta_scores())
    acc_ref[...] = _dot(vmt_ref[...], p.astype(BF16))
    l = lax.fori_loop(0, ng // 2 - 1, two_stages, jnp.sum(p, axis=0, keepdims=True))
    scores(ng - 1, 1)
    l = attend(ng - 2, 0, l)
    if carry_over:
        scores(0, 0, qtn_ref[...])
    l = attend(ng - 1, 1, l)
    o_ref[...] = (acc_ref[...] / l).T

    @pl.when(jnp.min(l) < SOFTMAX_UNDERFLOW)
    def _():
        s = meta_scores()
        m0 = jnp.max(s, axis=0, keepdims=True)
        p = jnp.exp2(s - m0)
        acc_ref[...] = _dot(vmt_ref[...], p.astype(BF16))

        def chunk(c, carry):
            m_prev, l_prev = carry
            s = _dot(k_ref[c], qt)
            m_new = jnp.maximum(m_prev, jnp.max(s, axis=0, keepdims=True))
            alpha = jnp.exp2(m_prev - m_new)
            p = jnp.exp2(s - m_new)
            acc_ref[...] = alpha * acc_ref[...] + _dot(vt_ref[c], p.astype(BF16))
            return m_new, alpha * l_prev + jnp.sum(p, axis=0, keepdims=True)

        _, l_exact = lax.fori_loop(0, nk, chunk, (m0, jnp.sum(p, axis=0, keepdims=True)))
        o_ref[...] = (acc_ref[...] / l_exact).T


def _mla_attn(qt, k, vt, km, vmt, tq, shared_q):
    nb, _, nk, tk, _ = k.shape
    nq = qt.shape[3] // tq
    qb = (lambda b: 0) if shared_q else (lambda b: b)
    return pl.pallas_call(
        functools.partial(_mla_attn_body, nk=nk, carry_over=nq > 1),
        grid=(nb, HEADS, nq),
        in_specs=[
            pl.BlockSpec((None, None, QK_PAD, tq), lambda b, h, i: (qb(b), h, 0, i)),
            pl.BlockSpec((None, None, QK_PAD, tq), lambda b, h, i: (qb(b), h, 0, jnp.minimum(i + 1, nq - 1))),
            pl.BlockSpec((None, None, nk, tk, QK_PAD), lambda b, h, i: (b, h, 0, 0, 0),
                         pipeline_mode=pl.Buffered(1 if nq > 1 else 2)),
            pl.BlockSpec((None, None, nk, HEAD_DIM, tk), lambda b, h, i: (b, h, 0, 0, 0),
                         pipeline_mode=pl.Buffered(1 if nq > 1 else 2)),
            pl.BlockSpec((None, None, None, META_ROWS, QK_PAD), lambda b, h, i: (0, h, 0, 0, 0)),
            pl.BlockSpec((None, None, None, HEAD_DIM, META_ROWS), lambda b, h, i: (0, h, 0, 0, 0)),
        ],
        out_specs=pl.BlockSpec((None, tq, HEAD_DIM), lambda b, h, i: (b, i, h)),
        out_shape=jax.ShapeDtypeStruct((nb, nq * tq, NA_WIDTH), F32),
        scratch_shapes=[pltpu.VMEM((HEAD_DIM, tq), F32), pltpu.VMEM((2 * ATT_GROUP, tk, tq), F32)],
        compiler_params=_cparams("parallel", "parallel", "arbitrary"),
        name="mla_attn",
    )(qt, qt, k, vt, km, vmt)


def _na_bias_body(rpb_ref, shift_ref, o_ref, t_ref, *, rows):
    h = pl.program_id(0)
    n_dr = 2 * NA_WIN_ROWS - 1
    n_dc = 2 * NA_WIN_COLS - 1
    shape = (GRID_W, LANES)
    qc = lax.broadcasted_iota(jnp.int32, shape, 0)
    lane = lax.broadcasted_iota(jnp.int32, shape, 1)
    left = lane < GRID_W
    kc = jnp.where(left, lane, lane - GRID_W)
    cs = jnp.clip(qc - NA_WIN_COLS // 2, 0, GRID_W - NA_WIN_COLS)
    col_ok = (kc >= cs) & (kc < cs + NA_WIN_COLS)
    dc = kc - qc + NA_WIN_COLS - 1

    def build(dr, carry):
        def pick(j, acc):
            return jnp.where(dc == j, rpb_ref[h, dr, j], acc)

        tile = lax.fori_loop(0, n_dc, pick, jnp.zeros(shape, F32))
        t_ref[dr] = jnp.where(col_ok, tile * LOG2E - shift_ref[0], NEG)
        return carry

    lax.fori_loop(0, n_dr, build, 0)

    neg = jnp.full(shape, NEG, F32)
    for c, r0 in enumerate((0, 2 * NA_QROWS, rows - NA_QROWS)):
        for qr in range(NA_QROWS):
            abs_qr = r0 + qr
            rs = min(max(abs_qr - NA_WIN_ROWS // 2, 0), rows - NA_WIN_ROWS)
            for a in range(NA_KROWS // 2):
                halves = []
                for kr in (r0 - NA_QROWS + 2 * a, r0 - NA_QROWS + 2 * a + 1):
                    in_window = rs <= kr < rs + NA_WIN_ROWS
                    halves.append(t_ref[kr - abs_qr + NA_WIN_ROWS - 1] if in_window else neg)
                o_ref[c, qr * GRID_W:(qr + 1) * GRID_W, a * LANES:(a + 1) * LANES] = (
                    jnp.where(left, halves[0], halves[1]))


def _na_bias(rpb, shift, rows):
    return pl.pallas_call(
        functools.partial(_na_bias_body, rows=rows),
        grid=(HEADS,),
        in_specs=[pl.BlockSpec(memory_space=pltpu.SMEM), pl.BlockSpec(memory_space=pltpu.SMEM)],
        out_specs=pl.BlockSpec((3, None, NA_TQ, NA_KROWS * GRID_W), lambda h: (0, h, 0, 0)),
        out_shape=jax.ShapeDtypeStruct((3, HEADS, NA_TQ, NA_KROWS * GRID_W), F32),
        scratch_shapes=[pltpu.VMEM((2 * NA_WIN_ROWS - 1, GRID_W, LANES), F32)],
        compiler_params=_cparams("parallel"),
        name="na_bias",
    )(rpb, shift.reshape(1))


def _na_body(q_ref, kp_ref, kc_ref, kn_ref, vp_ref, vc_ref, vn_ref, km_ref, vm_ref,
             bias_ref, mb_ref, g_ref, o_ref, out_ref, s_ref):
    for h in range(HEADS):
        sl = slice(h * HEAD_DIM, (h + 1) * HEAD_DIM)
        q = q_ref[:, sl]
        for c, k_ref in enumerate((kp_ref, kc_ref, kn_ref)):
            cols = slice(c * NA_TQ, (c + 1) * NA_TQ)
            s_ref[h, :, cols] = _dot(q, k_ref[sl, :]) + bias_ref[h, :, cols]
        s_ref[h, :, 3 * NA_TQ:] = _dot(q, km_ref[sl, :]) + mb_ref[h]
    def softmax_pv(running_max):
        l_min = None
        for h in range(HEADS):
            sl = slice(h * HEAD_DIM, (h + 1) * HEAD_DIM)
            s = s_ref[h]
            if running_max:
                s = s - jnp.max(s, axis=-1, keepdims=True)
            p = jnp.exp2(s)
            l = jnp.sum(p, axis=-1, keepdims=True)
            pb = p.astype(BF16)
            o = (_dot(pb[:, 3 * NA_TQ:], vm_ref[:, sl])
                 + _dot(pb[:, :NA_TQ], vp_ref[:, sl])
                 + _dot(pb[:, NA_TQ:2 * NA_TQ], vc_ref[:, sl])
                 + _dot(pb[:, 2 * NA_TQ:3 * NA_TQ], vn_ref[:, sl]))
            out_ref[:, sl] = o / l
            l_min = l if l_min is None else jnp.minimum(l_min, l)
        return l_min

    l_min = softmax_pv(False)

    @pl.when(jnp.min(l_min) < SOFTMAX_UNDERFLOW)
    def _():
        softmax_pv(True)

    o_ref[...] = (_rms(out_ref[...]) * g_ref[...]).astype(BF16)


def _na_attn(qa, kat, va, kmt, vm, bias, mb, g, nb):
    rows = qa.shape[0]
    n = rows // nb // NA_TQ
    cur = lambda b, i: (b * n + i, 0)
    prev = lambda b, i: (b * n + jnp.maximum(i - 1, 0), 0)
    nxt = lambda b, i: (b * n + jnp.minimum(i + 1, n - 1), 0)
    blk = lambda f: pl.BlockSpec((NA_TQ, NA_WIDTH), f)
    blk_t = lambda f: pl.BlockSpec((NA_WIDTH, NA_TQ), lambda b, i: f(b, i)[::-1])
    case = lambda b, i: (jnp.where(i == 0, 0, jnp.where(i == n - 1, 2, 1)), 0, 0, 0)
    return pl.pallas_call(
        _na_body,
        grid=(nb, n),
        in_specs=[
            blk(cur), blk_t(prev), blk_t(cur), blk_t(nxt), blk(prev), blk(cur), blk(nxt),
            pl.BlockSpec((NA_WIDTH, META_ROWS), lambda b, i: (0, 0)),
            pl.BlockSpec((META_ROWS, NA_WIDTH), lambda b, i: (0, 0)),
            pl.BlockSpec((None, HEADS, NA_TQ, NA_KROWS * GRID_W), case),
            pl.BlockSpec((HEADS, 1, META_ROWS), lambda b, i: (0, 0, 0)),
            pl.BlockSpec((1, NA_WIDTH), lambda b, i: (0, 0)),
        ],
        out_specs=blk(cur),
        out_shape=jax.ShapeDtypeStruct((rows, NA_WIDTH), BF16),
        scratch_shapes=[pltpu.VMEM((NA_TQ, NA_WIDTH), F32),
                        pltpu.VMEM((HEADS, NA_TQ, NA_KROWS * GRID_W + META_ROWS), F32)],
        compiler_params=_cparams("parallel", "arbitrary"),
        name="na_attn",
    )(qa, kat, kat, kat, va, va, va, kmt, vm, bias, mb, g)


def _na_meta_body(q_ref, kt_ref, v_ref, mb_ref, g_ref, o_ref, out_ref):
    for h in range(HEADS):
        sl = slice(h * HEAD_DIM, (h + 1) * HEAD_DIM)
        s = _dot(q_ref[:, sl], kt_ref[sl, :]) + mb_ref[h]
        m = jnp.max(s, axis=-1, keepdims=True)
        p = jnp.exp2(s - m)
        l = jnp.sum(p, axis=-1, keepdims=True)
        out_ref[:, sl] = _dot(p.astype(BF16), v_ref[:, sl]) / l
    o_ref[...] = (_rms(out_ref[...]) * g_ref[...]).astype(BF16)


def _na_meta(qm, kmt, vm, mb, g):
    full = lambda s: pl.BlockSpec(s, lambda i: (0,) * len(s))
    return pl.pallas_call(
        _na_meta_body,
        grid=(1,),
        in_specs=[full((META_ROWS, NA_WIDTH)), full((NA_WIDTH, META_ROWS)), full((META_ROWS, NA_WIDTH)),
                  full((HEADS, 1, META_ROWS)), full((1, NA_WIDTH))],
        out_specs=full((META_ROWS, NA_WIDTH)),
        out_shape=jax.ShapeDtypeStruct((META_ROWS, NA_WIDTH), BF16),
        scratch_shapes=[pltpu.VMEM((META_ROWS, NA_WIDTH), F32)],
        compiler_params=_cparams("arbitrary"),
        name="na_meta",
    )(qm, kmt, vm, mb, g)


def _outproj_body(x_ref, a_ref, b_ref, bg_ref, w_ref, fg_ref, h_ref, xn_ref):
    bn = (_rms(b_ref[...]) * bg_ref[...]).astype(BF16)
    mix = jnp.concatenate([a_ref[...], bn], axis=-1)
    h = x_ref[...] + _dot(mix, w_ref[...])
    h_ref[...] = h
    xn_ref[...] = (_rms(h) * fg_ref[...]).astype(BF16)


def _outproj(x, a, b, bg, w, fg, tm, x_map, a_map, b_map, rows):
    c = lambda i: (0, 0)
    return pl.pallas_call(
        _outproj_body,
        grid=(rows // tm,),
        in_specs=[
            pl.BlockSpec((tm, D_MODEL), x_map),
            pl.BlockSpec((tm, NA_WIDTH), a_map),
            pl.BlockSpec((tm, NA_WIDTH), b_map),
            pl.BlockSpec((1, NA_WIDTH), c),
            pl.BlockSpec((2 * NA_WIDTH, D_MODEL), c),
            pl.BlockSpec((1, D_MODEL), c),
        ],
        out_specs=[pl.BlockSpec((tm, D_MODEL), lambda i: (i, 0)),
                   pl.BlockSpec((tm, D_MODEL), lambda i: (i, 0))],
        out_shape=(jax.ShapeDtypeStruct((rows, D_MODEL), F32),
                   jax.ShapeDtypeStruct((rows, D_MODEL), BF16)),
        compiler_params=_cparams("parallel"),
        name="outproj",
    )(x, a, b, bg, w, fg)


def _ffn_body(xm_ref, xp_ref, xx_ref, xmeta_ref, h_ref, wg_ref, wu_ref, wd_ref, cw_ref, cb_ref,
              o_ref, xe_ref, *, tm, tpb):
    i = pl.program_id(0)
    j = pl.program_id(1)

    @pl.when(j == 0)
    def _():
        first = (i % tpb) == 0
        last = (i % tpb) == tpb - 1
        xe_ref[0:HALO, :] = jnp.where(first, xmeta_ref[...], xp_ref[...])
        xe_ref[HALO:HALO + tm, :] = xm_ref[...]
        xe_ref[HALO + tm:2 * HALO + tm, :] = jnp.where(last, jnp.zeros_like(xx_ref[...]), xx_ref[...])
        o_ref[...] = h_ref[...]

    g = _dot(xe_ref[...], wg_ref[...])
    u = _dot(xe_ref[HALO:HALO + tm, :], wu_ref[...])
    cw = cw_ref[...]
    gc = (cb_ref[...] + cw[0:1] * g[HALO - 1:HALO - 1 + tm]
          + cw[1:2] * g[HALO:HALO + tm] + cw[2:3] * g[HALO + 1:HALO + 1 + tm])
    act = gc * (1.0 / (1.0 + jnp.exp(-gc))) * u
    o_ref[...] += _dot(act.astype(BF16), wd_ref[...])


def _ffn(xn, xn_meta, h1, wg, wu, wd, cw, cb, nb, tm, tf):
    rows = xn.shape[0]
    nt = rows // tm
    tpb = nt // nb
    hb = tm // HALO
    last_hb = rows // HALO - 1
    return pl.pallas_call(
        functools.partial(_ffn_body, tm=tm, tpb=tpb),
        grid=(nt, D_FF // tf),
        in_specs=[
            pl.BlockSpec((tm, D_MODEL), lambda i, j: (i, 0)),
            pl.BlockSpec((HALO, D_MODEL), lambda i, j: (jnp.maximum(i * hb - 1, 0), 0)),
            pl.BlockSpec((HALO, D_MODEL), lambda i, j: (jnp.minimum((i + 1) * hb, last_hb), 0)),
            pl.BlockSpec((HALO, D_MODEL), lambda i, j: ((i // tpb) * (META_ROWS // HALO), 0)),
            pl.BlockSpec((tm, D_MODEL), lambda i, j: (i, 0)),
            pl.BlockSpec((D_MODEL, tf), lambda i, j: (0, j)),
            pl.BlockSpec((D_MODEL, tf), lambda i, j: (0, j)),
            pl.BlockSpec((tf, D_MODEL), lambda i, j: (j, 0)),
            pl.BlockSpec((3, tf), lambda i, j: (0, j)),
            pl.BlockSpec((1, tf), lambda i, j: (0, j)),
        ],
        out_specs=pl.BlockSpec((tm, D_MODEL), lambda i, j: (i, 0)),
        out_shape=jax.ShapeDtypeStruct((rows, D_MODEL), F32),
        scratch_shapes=[pltpu.VMEM((tm + 2 * HALO, D_MODEL), BF16)],
        compiler_params=_cparams("parallel", "arbitrary"),
        name="ffn",
    )(xn, xn, xn, xn_meta, h1, wg, wu, wd, cw, cb)


def _rope_tables(first, n):
    inv = ROPE_THETA ** (-np.arange(0, ROPE_DIM, 2, dtype=np.float64) / ROPE_DIM)
    ang = (first + np.arange(n, dtype=np.float64))[:, None] * inv[None, :]
    cos, sin = np.cos(ang), np.sin(ang)
    z = np.zeros_like(cos)
    tables = (np.concatenate([cos, cos, z, z], axis=-1), np.concatenate([-sin, sin, z, z], axis=-1), cos.T, sin.T)
    return tuple(jnp.asarray(t, F32) for t in tables)


def _softmax_shift(q_gain, k_gain):
    bound = LOG2E * QK_DIM ** 0.5 * jnp.max(jnp.abs(q_gain)) * jnp.max(jnp.abs(k_gain))
    return bound * (1.02 * (1.0 + 2.0 ** -7))


def _na_shift(q_gain, k_gain, rpb, meta_bias):
    qk = LOG2E * HEAD_DIM ** 0.5 * jnp.max(jnp.abs(q_gain)) * jnp.max(jnp.abs(k_gain))
    return 1.02 * qk + LOG2E * jnp.maximum(jnp.max(rpb), jnp.max(meta_bias))


def _pad_lanes(v, width):
    return jnp.pad(v, ((0, 0), (0, width - v.shape[-1])))


def kernel(x, meta_tokens, mix_norm_g, w_in, na_q_g, na_k_g, na_rpb, na_meta_bias, mla_cq_g, mla_ckv_g,
           w_q_up, w_kv_up, mla_q_g, mla_k_g, na_out_g, mla_out_g, w_out, ffn_norm_g, w_gate, w_up,
           conv_w, conv_b, w_down):
    nb, seq, d = x.shape
    assert d == D_MODEL and w_in.shape[0] == 1, "one layer of width 2048"
    rows_grid = seq // GRID_W
    assert seq % max(IN_TM, UP_TM, ATT_TQ, FFN_TM) == 0 and rows_grid >= 4 * NA_QROWS

    w_in_p = _pad_lanes(w_in[0], 4 * IN_TN).astype(BF16)
    w4 = w_in_p.reshape(D_MODEL, 4, IN_TN).transpose(1, 0, 2)
    wqt = w_q_up[0].reshape(Q_RANK, HEADS, QK_DIM).transpose(1, 2, 0)
    wqt = jnp.pad(wqt, ((0, 0), (0, QK_PAD - QK_DIM), (0, 0))).astype(BF16)
    wkv = w_kv_up[0].reshape(KV_RANK, HEADS, 2 * HEAD_DIM)
    wk = wkv[:, :, :HEAD_DIM].reshape(KV_RANK, HEADS // 2, 2 * HEAD_DIM).transpose(1, 0, 2).astype(BF16)
    wvt = wkv[:, :, HEAD_DIM:].transpose(1, 2, 0).astype(BF16)
    wo = w_out[0].astype(BF16)
    wg, wu, wd = w_gate[0].astype(BF16), w_up[0].astype(BF16), w_down[0].astype(BF16)
    row = lambda v: v.reshape(1, -1)
    qg_pad, kg_pad = _pad_lanes(mla_q_g, QK_PAD), _pad_lanes(mla_k_g, QK_PAD)
    na_shift = _na_shift(na_q_g, na_k_g, na_rpb, na_meta_bias)
    mb = jnp.pad(na_meta_bias[0] * LOG2E - na_shift, ((0, 0), (0, META_ROWS - N_META)),
                 constant_values=NEG).reshape(HEADS, 1, META_ROWS)

    xr = x.reshape(nb * seq, D_MODEL)
    xm = jnp.pad(meta_tokens.astype(x.dtype), ((0, META_ROWS - N_META), (0, 0)))
    rope_r = _rope_tables(N_META, seq)
    rope_m = _rope_tables(0, META_ROWS)

    inproj = functools.partial(_inproj, g=row(mix_norm_g), w4=w4, qg=row(na_q_g), kg=row(na_k_g),
                               cqg=row(mla_cq_g), ckvg=row(mla_ckv_g))
    qa, ka, va, cq, ckv, kpe = inproj(xr, tm=IN_TM)
    qa_m, ka_m, va_m, cq_m, ckv_m, kpe_m = inproj(xm, tm=META_ROWS)

    up = functools.partial(_mla_up, wqt=wqt, wk=wk, wvt=wvt, qg=qg_pad, kg=kg_pad,
                           shift=_softmax_shift(mla_q_g, mla_k_g))
    qt, kk, vt = up(cq, ckv, kpe, *rope_r, nb=nb, tm=UP_TM)
    qt_m, kk_m, vt_m = up(cq_m, ckv_m, kpe_m, *rope_m, nb=1, tm=META_ROWS)

    bias = _na_bias(na_rpb[0], na_shift, rows_grid)
    a_n = _na_attn(qa, ka, va, ka_m, va_m, bias, mb, row(na_out_g), nb)
    a_n_m = _na_meta(qa_m, ka_m, va_m, mb, row(na_out_g))

    out_b = _mla_attn(qt, kk, vt, kk_m, vt_m, ATT_TQ, shared_q=False)
    out_b_m = _mla_attn(qt_m, kk, vt, kk_m, vt_m, META_ROWS, shared_q=True)

    op = functools.partial(_outproj, bg=row(mla_out_g), w=wo, fg=row(ffn_norm_g))
    ident = lambda i: (i, 0)
    zero = lambda i: (0, 0)
    h1, xn2 = op(xr, a_n, out_b.reshape(nb * seq, NA_WIDTH), tm=OUT_TM,
                 x_map=ident, a_map=ident, b_map=ident, rows=nb * seq)
    _, xn2_m = op(xm, a_n_m, out_b_m.reshape(nb * META_ROWS, NA_WIDTH), tm=META_ROWS,
                  x_map=zero, a_map=zero, b_map=ident, rows=nb * META_ROWS)

    out = _ffn(xn2, xn2_m, h1, wg, wu, wd, conv_w[0], row(conv_b), nb, FFN_TM, FFN_TF)
    return out.reshape(nb, seq, D_MODEL)
```

```python
import functools

import jax
import jax.numpy as jnp
import numpy as np
from jax import lax
from jax.experimental import pallas as pl
from jax.experimental.pallas import tpu as pltpu

F32 = jnp.float32
BF16 = jnp.bfloat16

LANES = 128
BF16_SUBLANES = 16
VMEM_LIMIT_BYTES = 56 * 1024 * 1024

D_MODEL = 2048
GRID_W = 64
N_META = 16
EPS = 1e-6
NEG = -1e30
LOG2E = 1.4426950408889634
HEADS = 8
HEAD_DIM = 128
NA_WIDTH = HEADS * HEAD_DIM
NA_WIN_ROWS = 8
NA_WIN_COLS = 16
Q_RANK = 512
KV_RANK = 256
ROPE_DIM = 64
QK_DIM = HEAD_DIM + ROPE_DIM
QK_PAD = 2 * LANES
ROPE_THETA = 10000.0
D_FF = 5632
META_ROWS = LANES
SOFTMAX_UNDERFLOW = 2.0 ** -100

IN_TM = 1024
IN_TN = 1024
UP_TM = 512
ATT_TQ = 1024
ATT_GROUP = 4
NA_QROWS = 4
NA_TQ = NA_QROWS * GRID_W
NA_KROWS = 3 * NA_QROWS
OUT_TM = 256
FFN_TM = 512
FFN_TF = 512
HALO = BF16_SUBLANES


def _cparams(*sem):
    return pltpu.CompilerParams(dimension_semantics=sem, vmem_limit_bytes=VMEM_LIMIT_BYTES)


def _rms(v):
    return v * lax.rsqrt(jnp.mean(v * v, axis=-1, keepdims=True) + EPS)


def _dot(a, b):
    return jnp.dot(a, b, preferred_element_type=F32)


def _inproj_body(x_ref, g_ref, w_ref, qg_ref, kg_ref, cqg_ref, ckvg_ref,
                 qa_ref, ka_ref, va_ref, cq_ref, ckv_ref, kpe_ref, xn_ref, *, q_scale):
    j = pl.program_id(1)

    @pl.when(j == 0)
    def _():
        xn_ref[...] = (_rms(x_ref[...]) * g_ref[...]).astype(BF16)

    def project(lo, hi):
        return _dot(xn_ref[...], w_ref[:, lo:hi])

    def head_norm(gain_ref, out_ref, post, transposed):
        for t in range(HEADS // 2):
            y = project(2 * t * HEAD_DIM, 2 * (t + 1) * HEAD_DIM)
            for u in range(2):
                sl = slice((2 * t + u) * HEAD_DIM, (2 * t + u + 1) * HEAD_DIM)
                yh = _rms(y[:, u * HEAD_DIM:(u + 1) * HEAD_DIM]) * gain_ref[...] * post
                if transposed:
                    out_ref[sl, :] = yh.T.astype(BF16)
                else:
                    out_ref[:, sl] = yh.astype(BF16)

    @pl.when(j == 0)
    def _():
        head_norm(qg_ref, qa_ref, q_scale, False)

    @pl.when(j == 1)
    def _():
        head_norm(kg_ref, ka_ref, 1.0, True)

    @pl.when(j == 2)
    def _():
        va_ref[...] = project(0, IN_TN).astype(BF16)

    @pl.when(j == 3)
    def _():
        cq_ref[...] = (_rms(project(0, Q_RANK)) * cqg_ref[...]).astype(BF16)
        y = project(Q_RANK, IN_TN)
        ckv_ref[...] = (_rms(y[:, :KV_RANK]) * ckvg_ref[...]).astype(BF16)
        kpe_ref[...] = y[:, KV_RANK:KV_RANK + LANES]


def _inproj(x, g, w4, qg, kg, cqg, ckvg, tm):
    rows = x.shape[0]
    row = lambda i, j: (i, 0)
    const = lambda i, j: (0, 0)
    outs = (
        jax.ShapeDtypeStruct((rows, NA_WIDTH), BF16),
        jax.ShapeDtypeStruct((NA_WIDTH, rows), BF16),
        jax.ShapeDtypeStruct((rows, NA_WIDTH), BF16),
        jax.ShapeDtypeStruct((rows, Q_RANK), BF16),
        jax.ShapeDtypeStruct((rows, KV_RANK), BF16),
        jax.ShapeDtypeStruct((rows, LANES), F32),
    )
    return pl.pallas_call(
        functools.partial(_inproj_body, q_scale=HEAD_DIM ** -0.5 * LOG2E),
        grid=(rows // tm, 4),
        in_specs=[
            pl.BlockSpec((tm, D_MODEL), row),
            pl.BlockSpec((1, D_MODEL), const),
            pl.BlockSpec((None, D_MODEL, IN_TN), lambda i, j: (j, 0, 0)),
            pl.BlockSpec((1, HEAD_DIM), const),
            pl.BlockSpec((1, HEAD_DIM), const),
            pl.BlockSpec((1, Q_RANK), const),
            pl.BlockSpec((1, KV_RANK), const),
        ],
        out_specs=[
            pl.BlockSpec((tm, NA_WIDTH), row),
            pl.BlockSpec((NA_WIDTH, tm), lambda i, j: (0, i)),
            pl.BlockSpec((tm, NA_WIDTH), row),
            pl.BlockSpec((tm, Q_RANK), row),
            pl.BlockSpec((tm, KV_RANK), row),
            pl.BlockSpec((tm, LANES), row),
        ],
        out_shape=outs,
        scratch_shapes=[pltpu.VMEM((tm, D_MODEL), BF16)],
        compiler_params=_cparams("parallel", "arbitrary"),
        name="inproj",
    )(x, g, w4, qg, kg, cqg, ckvg)


def _mla_up_body(cq_ref, ckv_ref, kpe_ref, cos_ref, sin_ref, cost_ref, sint_ref,
                 wqt_ref, wk_ref, wvt_ref, qgt_ref, qpad_ref, kg_ref, qt_ref, k_ref, vt_ref, *, q_scale):
    half = ROPE_DIM // 2
    ckv = ckv_ref[...]
    cqt = cq_ref[...].astype(F32).T.astype(BF16)
    ckvt = ckv.astype(F32).T.astype(BF16)
    qgt = qgt_ref[...]
    cost = cost_ref[...]
    sint = sint_ref[...]
    cosf = cos_ref[...]
    sinf = sin_ref[...]
    kg = kg_ref[...]

    kpe = kpe_ref[...]
    kpe_ss = jnp.sum(kpe * kpe, axis=-1, keepdims=True)
    r = kpe * kg[:, LANES:]
    kpe_rot = r * cosf + (pltpu.roll(r, half, 1) + pltpu.roll(r, LANES - half, 1)) * sinf

    for h in range(HEADS):
        q = _dot(wqt_ref[h], cqt)
        ms = jnp.sum(q * q, axis=0, keepdims=True) * (1.0 / QK_DIM)
        qn = q * (lax.rsqrt(ms + EPS) * q_scale) * qgt
        x1 = qn[LANES:LANES + half]
        x2 = qn[LANES + half:LANES + 2 * half]
        qt_ref[h, 0:LANES, :] = qn[:LANES].astype(BF16)
        qt_ref[h, LANES:LANES + half, :] = (x1 * cost - x2 * sint).astype(BF16)
        qt_ref[h, LANES + half:LANES + 2 * half, :] = (x2 * cost + x1 * sint).astype(BF16)
        qt_ref[h, LANES + 2 * half:, :] = qpad_ref[...]
        vt_ref[h] = _dot(wvt_ref[h], ckvt).astype(BF16)

    one_lane = (lax.broadcasted_iota(jnp.int32, (1, LANES), 1) == ROPE_DIM).astype(F32)
    for t in range(HEADS // 2):
        kn2 = _dot(ckv, wk_ref[t])
        for u in range(2):
            kn = kn2[:, u * LANES:(u + 1) * LANES]
            ms = (jnp.sum(kn * kn, axis=-1, keepdims=True) + kpe_ss) * (1.0 / QK_DIM)
            rn = lax.rsqrt(ms + EPS)
            k_ref[2 * t + u] = jnp.concatenate([kn * rn * kg[:, :LANES], kpe_rot * rn + one_lane],
                                               axis=-1).astype(BF16)


def _mla_up(cq, ckv, kpe, cosf, sinf, cost, sint, wqt, wk, wvt, qg, kg, shift, nb, tm):
    rows = cq.shape[0]
    nt = rows // nb // tm
    row = lambda b, i: (b * nt + i, 0)
    pos = lambda b, i: (i, 0)
    post = lambda b, i: (0, i)
    c2 = lambda b, i: (0, 0)
    c3 = lambda b, i: (0, 0, 0)
    qgt = jnp.broadcast_to(qg.reshape(QK_DIM, 1), (QK_DIM, tm))
    qpad = jnp.zeros((QK_PAD - QK_DIM, tm), F32).at[0].set(-shift).astype(BF16)
    outs = (
        jax.ShapeDtypeStruct((nb, HEADS, QK_PAD, nt * tm), BF16),
        jax.ShapeDtypeStruct((nb, HEADS, nt, tm, QK_PAD), BF16),
        jax.ShapeDtypeStruct((nb, HEADS, nt, HEAD_DIM, tm), BF16),
    )
    return pl.pallas_call(
        functools.partial(_mla_up_body, q_scale=QK_DIM ** -0.5 * LOG2E),
        grid=(nb, nt),
        in_specs=[
            pl.BlockSpec((tm, Q_RANK), row),
            pl.BlockSpec((tm, KV_RANK), row),
            pl.BlockSpec((tm, LANES), row),
            pl.BlockSpec((tm, LANES), pos),
            pl.BlockSpec((tm, LANES), pos),
            pl.BlockSpec((ROPE_DIM // 2, tm), post),
            pl.BlockSpec((ROPE_DIM // 2, tm), post),
            pl.BlockSpec((HEADS, QK_DIM, Q_RANK), c3),
            pl.BlockSpec((HEADS // 2, KV_RANK, 2 * HEAD_DIM), c3),
            pl.BlockSpec((HEADS, HEAD_DIM, KV_RANK), c3),
            pl.BlockSpec((QK_DIM, tm), c2),
            pl.BlockSpec((QK_PAD - QK_DIM, tm), c2),
            pl.BlockSpec((1, QK_PAD), c2),
        ],
        out_specs=[
            pl.BlockSpec((None, HEADS, QK_PAD, tm), lambda b, i: (b, 0, 0, i)),
            pl.BlockSpec((None, HEADS, None, tm, QK_PAD), lambda b, i: (b, 0, i, 0, 0)),
            pl.BlockSpec((None, HEADS, None, HEAD_DIM, tm), lambda b, i: (b, 0, i, 0, 0)),
        ],
        out_shape=outs,
        compiler_params=_cparams("parallel", "parallel"),
        name="mla_up",
    )(cq, ckv, kpe, cosf, sinf, cost, sint, wqt, wk, wvt, qgt, qpad, kg)


def _mla_attn_body(qt_ref, qtn_ref, k_ref, vt_ref, km_ref, vmt_ref, o_ref, acc_ref, s_ref,
                   *, nk, carry_over):
    qt = qt_ref[...]
    ng = nk // ATT_GROUP

    def scores(g, half, q=None):
        q = qt if q is None else q
        for u in range(ATT_GROUP):
            s_ref[half * ATT_GROUP + u] = _dot(k_ref[g * ATT_GROUP + u], q)

    def attend(g, half, l):
        pv = None
        for u in range(ATT_GROUP):
            p = jnp.exp2(s_ref[half * ATT_GROUP + u])
            l = l + jnp.sum(p, axis=0, keepdims=True)
            d = _dot(vt_ref[g * ATT_GROUP + u], p.astype(BF16))
            pv = d if pv is None else pv + d
        acc_ref[...] += pv
        return l

    def meta_scores():
        s = _dot(km_ref[...], qt)
        key = lax.broadcasted_iota(jnp.int32, s.shape, 0)
        return jnp.where(key < N_META, s, NEG)

    def two_stages(i, l):
        scores(2 * i + 1, 1)
        l = attend(2 * i, 0, l)
        scores(2 * i + 2, 0)
        return attend(2 * i + 1, 1, l)

    if carry_over:
        @pl.when(pl.program_id(2) == 0)
        def _():
            scores(0, 0)
    else:
        scores(0, 0)

    p = jnp.exp2(meta_scores())
    acc_ref[...] = _dot(vmt_ref[...], p.astype(BF16))
    l = lax.fori_loop(0, ng // 2 - 1, two_stages, jnp.sum(p, axis=0, keepdims=True))
    scores(ng - 1, 1)
    l = attend(ng - 2, 0, l)
    if carry_over:
        scores(0, 0, qtn_ref[...])
    l = attend(ng - 1, 1, l)
    o_ref[...] = (acc_ref[...] / l).T

    @pl.when(jnp.min(l) < SOFTMAX_UNDERFLOW)
    def _():
        s = meta_scores()
        m0 = jnp.max(s, axis=0, keepdims=True)
        p = jnp.exp2(s - m0)
        acc_ref[...] = _dot(vmt_ref[...], p.astype(BF16))

        def chunk(c, carry):
            m_prev, l_prev = carry
            s = _dot(k_ref[c], qt)
            m_new = jnp.maximum(m_prev, jnp.max(s, axis=0, keepdims=True))
            alpha = jnp.exp2(m_prev - m_new)
            p = jnp.exp2(s - m_new)
            acc_ref[...] = alpha * acc_ref[...] + _dot(vt_ref[c], p.astype(BF16))
            return m_new, alpha * l_prev + jnp.sum(p, axis=0, keepdims=True)

        _, l_exact = lax.fori_loop(0, nk, chunk, (m0, jnp.sum(p, axis=0, keepdims=True)))
        o_ref[...] = (acc_ref[...] / l_exact).T


def _mla_attn(qt, k, vt, km, vmt, tq, shared_q):
    nb, _, nk, tk, _ = k.shape
    nq = qt.shape[3] // tq
    qb = (lambda b: 0) if shared_q else (lambda b: b)
    return pl.pallas_call(
        functools.partial(_mla_attn_body, nk=nk, carry_over=nq > 1),
        grid=(nb, HEADS, nq),
        in_specs=[
            pl.BlockSpec((None, None, QK_PAD, tq), lambda b, h, i: (qb(b), h, 0, i)),
            pl.BlockSpec((None, None, QK_PAD, tq), lambda b, h, i: (qb(b), h, 0, jnp.minimum(i + 1, nq - 1))),
            pl.BlockSpec((None, None, nk, tk, QK_PAD), lambda b, h, i: (b, h, 0, 0, 0),
                         pipeline_mode=pl.Buffered(1 if nq > 1 else 2)),
            pl.BlockSpec((None, None, nk, HEAD_DIM, tk), lambda b, h, i: (b, h, 0, 0, 0),
                         pipeline_mode=pl.Buffered(1 if nq > 1 else 2)),
            pl.BlockSpec((None, None, None, META_ROWS, QK_PAD), lambda b, h, i: (0, h, 0, 0, 0)),
            pl.BlockSpec((None, None, None, HEAD_DIM, META_ROWS), lambda b, h, i: (0, h, 0, 0, 0)),
        ],
        out_specs=pl.BlockSpec((None, tq, HEAD_DIM), lambda b, h, i: (b, i, h)),
        out_shape=jax.ShapeDtypeStruct((nb, nq * tq, NA_WIDTH), F32),
        scratch_shapes=[pltpu.VMEM((HEAD_DIM, tq), F32), pltpu.VMEM((2 * ATT_GROUP, tk, tq), F32)],
        compiler_params=_cparams("parallel", "parallel", "arbitrary"),
        name="mla_attn",
    )(qt, qt, k, vt, km, vmt)


def _na_bias_body(rpb_ref, shift_ref, o_ref, t_ref, *, rows):
    n_dr = 2 * NA_WIN_ROWS - 1
    shape = (GRID_W, LANES)
    qc = lax.broadcasted_iota(jnp.int32, shape, 0)
    lane = lax.broadcasted_iota(jnp.int32, shape, 1)
    left = lane < GRID_W
    kc = jnp.where(left, lane, lane - GRID_W)
    cs = jnp.clip(qc - NA_WIN_COLS // 2, 0, GRID_W - NA_WIN_COLS)
    col_ok = (kc >= cs) & (kc < cs + NA_WIN_COLS)

    for dr in range(n_dr):
        x = jnp.broadcast_to(rpb_ref[dr:dr + 1, :], shape)
        a = pltpu.roll(x, LANES - (NA_WIN_COLS - 1), 1, stride=1, stride_axis=0)
        b = pltpu.roll(x, GRID_W - (NA_WIN_COLS - 1), 1, stride=1, stride_axis=0)
        t_ref[dr] = jnp.where(col_ok, jnp.where(left, a, b) * LOG2E - shift_ref[0], NEG)

    neg = jnp.full(shape, NEG, F32)
    for c, r0 in enumerate((0, 2 * NA_QROWS, rows - NA_QROWS)):
        for qr in range(NA_QROWS):
            abs_qr = r0 + qr
            rs = min(max(abs_qr - NA_WIN_ROWS // 2, 0), rows - NA_WIN_ROWS)
            for a in range(NA_KROWS // 2):
                halves = []
                for kr in (r0 - NA_QROWS + 2 * a, r0 - NA_QROWS + 2 * a + 1):
                    in_window = rs <= kr < rs + NA_WIN_ROWS
                    halves.append(t_ref[kr - abs_qr + NA_WIN_ROWS - 1] if in_window else neg)
                o_ref[c, qr * GRID_W:(qr + 1) * GRID_W, a * LANES:(a + 1) * LANES] = (
                    jnp.where(left, halves[0], halves[1]))


def _na_bias(rpb, shift, rows):
    n_dr, n_dc = rpb.shape[1:]
    rpb = jnp.pad(rpb, ((0, 0), (0, -n_dr % 8), (0, LANES - n_dc)))
    return pl.pallas_call(
        functools.partial(_na_bias_body, rows=rows),
        grid=(HEADS,),
        in_specs=[pl.BlockSpec((None,) + rpb.shape[1:], lambda h: (h, 0, 0)),
                  pl.BlockSpec(memory_space=pltpu.SMEM)],
        out_specs=pl.BlockSpec((3, None, NA_TQ, NA_KROWS * GRID_W), lambda h: (0, h, 0, 0)),
        out_shape=jax.ShapeDtypeStruct((3, HEADS, NA_TQ, NA_KROWS * GRID_W), F32),
        scratch_shapes=[pltpu.VMEM((2 * NA_WIN_ROWS - 1, GRID_W, LANES), F32)],
        compiler_params=_cparams("parallel"),
        name="na_bias",
    )(rpb, shift.reshape(1))


def _na_body(q_ref, kp_ref, kc_ref, kn_ref, vp_ref, vc_ref, vn_ref, km_ref, vm_ref,
             bias_ref, mb_ref, g_ref, o_ref, out_ref, s_ref):
    for h in range(HEADS):
        sl = slice(h * HEAD_DIM, (h + 1) * HEAD_DIM)
        q = q_ref[:, sl]
        for c, k_ref in enumerate((kp_ref, kc_ref, kn_ref)):
            cols = slice(c * NA_TQ, (c + 1) * NA_TQ)
            s_ref[h, :, cols] = _dot(q, k_ref[sl, :]) + bias_ref[h, :, cols]
        s_ref[h, :, 3 * NA_TQ:] = _dot(q, km_ref[sl, :]) + mb_ref[h]
    def softmax_pv(running_max):
        l_min = None
        for h in range(HEADS):
            sl = slice(h * HEAD_DIM, (h + 1) * HEAD_DIM)
            s = s_ref[h]
            if running_max:
                s = s - jnp.max(s, axis=-1, keepdims=True)
            p = jnp.exp2(s)
            l = jnp.sum(p, axis=-1, keepdims=True)
            pb = p.astype(BF16)
            o = (_dot(pb[:, 3 * NA_TQ:], vm_ref[:, sl])
                 + _dot(pb[:, :NA_TQ], vp_ref[:, sl])
                 + _dot(pb[:, NA_TQ:2 * NA_TQ], vc_ref[:, sl])
                 + _dot(pb[:, 2 * NA_TQ:3 * NA_TQ], vn_ref[:, sl]))
            out_ref[:, sl] = o / l
            l_min = l if l_min is None else jnp.minimum(l_min, l)
        return l_min

    l_min = softmax_pv(False)

    @pl.when(jnp.min(l_min) < SOFTMAX_UNDERFLOW)
    def _():
        softmax_pv(True)

    o_ref[...] = (_rms(out_ref[...]) * g_ref[...]).astype(BF16)


def _na_attn(qa, kat, va, kmt, vm, bias, mb, g, nb):
    rows = qa.shape[0]
    n = rows // nb // NA_TQ
    cur = lambda b, i: (b * n + i, 0)
    prev = lambda b, i: (b * n + jnp.maximum(i - 1, 0), 0)
    nxt = lambda b, i: (b * n + jnp.minimum(i + 1, n - 1), 0)
    blk = lambda f: pl.BlockSpec((NA_TQ, NA_WIDTH), f)
    blk_t = lambda f: pl.BlockSpec((NA_WIDTH, NA_TQ), lambda b, i: f(b, i)[::-1])
    case = lambda b, i: (jnp.where(i == 0, 0, jnp.where(i == n - 1, 2, 1)), 0, 0, 0)
    return pl.pallas_call(
        _na_body,
        grid=(nb, n),
        in_specs=[
            blk(cur), blk_t(prev), blk_t(cur), blk_t(nxt), blk(prev), blk(cur), blk(nxt),
            pl.BlockSpec((NA_WIDTH, META_ROWS), lambda b, i: (0, 0)),
            pl.BlockSpec((META_ROWS, NA_WIDTH), lambda b, i: (0, 0)),
            pl.BlockSpec((None, HEADS, NA_TQ, NA_KROWS * GRID_W), case),
            pl.BlockSpec((HEADS, 1, META_ROWS), lambda b, i: (0, 0, 0)),
            pl.BlockSpec((1, NA_WIDTH), lambda b, i: (0, 0)),
        ],
        out_specs=blk(cur),
        out_shape=jax.ShapeDtypeStruct((rows, NA_WIDTH), BF16),
        scratch_shapes=[pltpu.VMEM((NA_TQ, NA_WIDTH), F32),
                        pltpu.VMEM((HEADS, NA_TQ, NA_KROWS * GRID_W + META_ROWS), F32)],
        compiler_params=_cparams("parallel", "arbitrary"),
        name="na_attn",
    )(qa, kat, kat, kat, va, va, va, kmt, vm, bias, mb, g)


def _na_meta_body(q_ref, kt_ref, v_ref, mb_ref, g_ref, o_ref, out_ref):
    for h in range(HEADS):
        sl = slice(h * HEAD_DIM, (h + 1) * HEAD_DIM)
        s = _dot(q_ref[:, sl], kt_ref[sl, :]) + mb_ref[h]
        m = jnp.max(s, axis=-1, keepdims=True)
        p = jnp.exp2(s - m)
        l = jnp.sum(p, axis=-1, keepdims=True)
        out_ref[:, sl] = _dot(p.astype(BF16), v_ref[:, sl]) / l
    o_ref[...] = (_rms(out_ref[...]) * g_ref[...]).astype(BF16)


def _na_meta(qm, kmt, vm, mb, g):
    full = lambda s: pl.BlockSpec(s, lambda i: (0,) * len(s))
    return pl.pallas_call(
        _na_meta_body,
        grid=(1,),
        in_specs=[full((META_ROWS, NA_WIDTH)), full((NA_WIDTH, META_ROWS)), full((META_ROWS, NA_WIDTH)),
                  full((HEADS, 1, META_ROWS)), full((1, NA_WIDTH))],
        out_specs=full((META_ROWS, NA_WIDTH)),
        out_shape=jax.ShapeDtypeStruct((META_ROWS, NA_WIDTH), BF16),
        scratch_shapes=[pltpu.VMEM((META_ROWS, NA_WIDTH), F32)],
        compiler_params=_cparams("arbitrary"),
        name="na_meta",
    )(qm, kmt, vm, mb, g)


def _outproj_body(x_ref, a_ref, b_ref, bg_ref, w_ref, fg_ref, h_ref, xn_ref):
    bn = (_rms(b_ref[...]) * bg_ref[...]).astype(BF16)
    mix = jnp.concatenate([a_ref[...], bn], axis=-1)
    h = x_ref[...] + _dot(mix, w_ref[...])
    h_ref[...] = h
    xn_ref[...] = (_rms(h) * fg_ref[...]).astype(BF16)


def _outproj(x, a, b, bg, w, fg, tm, x_map, a_map, b_map, rows):
    c = lambda i: (0, 0)
    return pl.pallas_call(
        _outproj_body,
        grid=(rows // tm,),
        in_specs=[
            pl.BlockSpec((tm, D_MODEL), x_map),
            pl.BlockSpec((tm, NA_WIDTH), a_map),
            pl.BlockSpec((tm, NA_WIDTH), b_map),
            pl.BlockSpec((1, NA_WIDTH), c),
            pl.BlockSpec((2 * NA_WIDTH, D_MODEL), c),
            pl.BlockSpec((1, D_MODEL), c),
        ],
        out_specs=[pl.BlockSpec((tm, D_MODEL), lambda i: (i, 0)),
                   pl.BlockSpec((tm, D_MODEL), lambda i: (i, 0))],
        out_shape=(jax.ShapeDtypeStruct((rows, D_MODEL), F32),
                   jax.ShapeDtypeStruct((rows, D_MODEL), BF16)),
        compiler_params=_cparams("parallel"),
        name="outproj",
    )(x, a, b, bg, w, fg)


def _ffn_body(xm_ref, xp_ref, xx_ref, xmeta_ref, h_ref, wg_ref, wu_ref, wd_ref, cw_ref, cb_ref,
              o_ref, xe_ref, *, tm, tpb):
    i = pl.program_id(0)
    j = pl.program_id(1)

    @pl.when(j == 0)
    def _():
        first = (i % tpb) == 0
        last = (i % tpb) == tpb - 1
        xe_ref[0:HALO, :] = jnp.where(first, xmeta_ref[...], xp_ref[...])
        xe_ref[HALO:HALO + tm, :] = xm_ref[...]
        xe_ref[HALO + tm:2 * HALO + tm, :] = jnp.where(last, jnp.zeros_like(xx_ref[...]), xx_ref[...])
        o_ref[...] = h_ref[...]

    g = _dot(xe_ref[...], wg_ref[...])
    u = _dot(xe_ref[HALO:HALO + tm, :], wu_ref[...])
    cw = cw_ref[...]
    gc = (cb_ref[...] + cw[0:1] * g[HALO - 1:HALO - 1 + tm]
          + cw[1:2] * g[HALO:HALO + tm] + cw[2:3] * g[HALO + 1:HALO + 1 + tm])
    act = gc * (1.0 / (1.0 + jnp.exp(-gc))) * u
    o_ref[...] += _dot(act.astype(BF16), wd_ref[...])


def _ffn(xn, xn_meta, h1, wg, wu, wd, cw, cb, nb, tm, tf):
    rows = xn.shape[0]
    nt = rows // tm
    tpb = nt // nb
    hb = tm // HALO
    last_hb = rows // HALO - 1
    return pl.pallas_call(
        functools.partial(_ffn_body, tm=tm, tpb=tpb),
        grid=(nt, D_FF // tf),
        in_specs=[
            pl.BlockSpec((tm, D_MODEL), lambda i, j: (i, 0)),
            pl.BlockSpec((HALO, D_MODEL), lambda i, j: (jnp.maximum(i * hb - 1, 0), 0)),
            pl.BlockSpec((HALO, D_MODEL), lambda i, j: (jnp.minimum((i + 1) * hb, last_hb), 0)),
            pl.BlockSpec((HALO, D_MODEL), lambda i, j: ((i // tpb) * (META_ROWS // HALO), 0)),
            pl.BlockSpec((tm, D_MODEL), lambda i, j: (i, 0)),
            pl.BlockSpec((D_MODEL, tf), lambda i, j: (0, j)),
            pl.BlockSpec((D_MODEL, tf), lambda i, j: (0, j)),
            pl.BlockSpec((tf, D_MODEL), lambda i, j: (j, 0)),
            pl.BlockSpec((3, tf), lambda i, j: (0, j)),
            pl.BlockSpec((1, tf), lambda i, j: (0, j)),
        ],
        out_specs=pl.BlockSpec((tm, D_MODEL), lambda i, j: (i, 0)),
        out_shape=jax.ShapeDtypeStruct((rows, D_MODEL), F32),
        scratch_shapes=[pltpu.VMEM((tm + 2 * HALO, D_MODEL), BF16)],
        compiler_params=_cparams("parallel", "arbitrary"),
        name="ffn",
    )(xn, xn, xn, xn_meta, h1, wg, wu, wd, cw, cb)


def _rope_tables(first, n):
    inv = ROPE_THETA ** (-np.arange(0, ROPE_DIM, 2, dtype=np.float64) / ROPE_DIM)
    ang = (first + np.arange(n, dtype=np.float64))[:, None] * inv[None, :]
    cos, sin = np.cos(ang), np.sin(ang)
    z = np.zeros_like(cos)
    tables = (np.concatenate([cos, cos, z, z], axis=-1), np.concatenate([-sin, sin, z, z], axis=-1), cos.T, sin.T)
    return tuple(jnp.asarray(t, F32) for t in tables)


def _softmax_shift(q_gain, k_gain):
    bound = LOG2E * QK_DIM ** 0.5 * jnp.max(jnp.abs(q_gain)) * jnp.max(jnp.abs(k_gain))
    return bound * (1.02 * (1.0 + 2.0 ** -7))


def _na_shift(q_gain, k_gain, rpb, meta_bias):
    qk = LOG2E * HEAD_DIM ** 0.5 * jnp.max(jnp.abs(q_gain)) * jnp.max(jnp.abs(k_gain))
    return 1.02 * qk + LOG2E * jnp.maximum(jnp.max(rpb), jnp.max(meta_bias))


def _pad_lanes(v, width):
    return jnp.pad(v, ((0, 0), (0, width - v.shape[-1])))


def kernel(x, meta_tokens, mix_norm_g, w_in, na_q_g, na_k_g, na_rpb, na_meta_bias, mla_cq_g, mla_ckv_g,
           w_q_up, w_kv_up, mla_q_g, mla_k_g, na_out_g, mla_out_g, w_out, ffn_norm_g, w_gate, w_up,
           conv_w, conv_b, w_down):
    nb, seq, d = x.shape
    assert d == D_MODEL and w_in.shape[0] == 1, "one layer of width 2048"
    rows_grid = seq // GRID_W
    assert seq % max(IN_TM, UP_TM, ATT_TQ, FFN_TM) == 0 and rows_grid >= 4 * NA_QROWS

    w_in_p = _pad_lanes(w_in[0], 4 * IN_TN).astype(BF16)
    w4 = w_in_p.reshape(D_MODEL, 4, IN_TN).transpose(1, 0, 2)
    wqt = w_q_up[0].reshape(Q_RANK, HEADS, QK_DIM).transpose(1, 2, 0).astype(BF16)
    wkv = w_kv_up[0].reshape(KV_RANK, HEADS, 2 * HEAD_DIM)
    wk = wkv[:, :, :HEAD_DIM].reshape(KV_RANK, HEADS // 2, 2 * HEAD_DIM).transpose(1, 0, 2).astype(BF16)
    wvt = wkv[:, :, HEAD_DIM:].transpose(1, 2, 0).astype(BF16)
    wo = w_out[0].astype(BF16)
    wg, wu, wd = w_gate[0].astype(BF16), w_up[0].astype(BF16), w_down[0].astype(BF16)
    row = lambda v: v.reshape(1, -1)
    kg_pad = _pad_lanes(mla_k_g, QK_PAD)
    na_shift = _na_shift(na_q_g, na_k_g, na_rpb, na_meta_bias)
    mb = jnp.pad(na_meta_bias[0] * LOG2E - na_shift, ((0, 0), (0, META_ROWS - N_META)),
                 constant_values=NEG).reshape(HEADS, 1, META_ROWS)

    xr = x.reshape(nb * seq, D_MODEL)
    xm = jnp.pad(meta_tokens.astype(x.dtype), ((0, META_ROWS - N_META), (0, 0)))
    rope_r = _rope_tables(N_META, seq)
    rope_m = _rope_tables(0, META_ROWS)

    inproj = functools.partial(_inproj, g=row(mix_norm_g), w4=w4, qg=row(na_q_g), kg=row(na_k_g),
                               cqg=row(mla_cq_g), ckvg=row(mla_ckv_g))
    qa, ka, va, cq, ckv, kpe = inproj(xr, tm=IN_TM)
    qa_m, ka_m, va_m, cq_m, ckv_m, kpe_m = inproj(xm, tm=META_ROWS)

    up = functools.partial(_mla_up, wqt=wqt, wk=wk, wvt=wvt, qg=mla_q_g, kg=kg_pad,
                           shift=_softmax_shift(mla_q_g, mla_k_g))
    qt, kk, vt = up(cq, ckv, kpe, *rope_r, nb=nb, tm=UP_TM)
    qt_m, kk_m, vt_m = up(cq_m, ckv_m, kpe_m, *rope_m, nb=1, tm=META_ROWS)

    bias = _na_bias(na_rpb[0], na_shift, rows_grid)
    a_n = _na_attn(qa, ka, va, ka_m, va_m, bias, mb, row(na_out_g), nb)
    a_n_m = _na_meta(qa_m, ka_m, va_m, mb, row(na_out_g))

    out_b = _mla_attn(qt, kk, vt, kk_m, vt_m, ATT_TQ, shared_q=False)
    out_b_m = _mla_attn(qt_m, kk, vt, kk_m, vt_m, META_ROWS, shared_q=True)

    op = functools.partial(_outproj, bg=row(mla_out_g), w=wo, fg=row(ffn_norm_g))
    ident = lambda i: (i, 0)
    zero = lambda i: (0, 0)
    h1, xn2 = op(xr, a_n, out_b.reshape(nb * seq, NA_WIDTH), tm=OUT_TM,
                 x_map=ident, a_map=ident, b_map=ident, rows=nb * seq)
    _, xn2_m = op(xm, a_n_m, out_b_m.reshape(nb * META_ROWS, NA_WIDTH), tm=META_ROWS,
                  x_map=zero, a_map=zero, b_map=ident, rows=nb * META_ROWS)

    out = _ffn(xn2, xn2_m, h1, wg, wu, wd, conv_w[0], row(conv_b), nb, FFN_TM, FFN_TF)
    return out.reshape(nb, seq, D_MODEL)
```

```python
import functools

import jax
import jax.numpy as jnp
import numpy as np
from jax import lax
from jax.experimental import pallas as pl
from jax.experimental.pallas import tpu as pltpu

F32 = jnp.float32
BF16 = jnp.bfloat16

LANES = 128
BF16_SUBLANES = 16
VMEM_LIMIT_BYTES = 56 * 1024 * 1024

D_MODEL = 2048
GRID_W = 64
N_META = 16
EPS = 1e-6
NEG = -1e30
LOG2E = 1.4426950408889634
HEADS = 8
HEAD_DIM = 128
NA_WIDTH = HEADS * HEAD_DIM
NA_WIN_ROWS = 8
NA_WIN_COLS = 16
Q_RANK = 512
KV_RANK = 256
ROPE_DIM = 64
QK_DIM = HEAD_DIM + ROPE_DIM
QK_PAD = 2 * LANES
ROPE_THETA = 10000.0
D_FF = 5632
META_ROWS = LANES
SOFTMAX_UNDERFLOW = 2.0 ** -100

IN_TM = 1024
IN_TN = 1024
UP_TM = 512
ATT_TQ = 1024
ATT_GROUP = 4
NA_QROWS = 4
NA_TQ = NA_QROWS * GRID_W
NA_KROWS = 3 * NA_QROWS
OUT_TM = 256
FFN_TM = 512
FFN_TF = 512
HALO = BF16_SUBLANES


def _cparams(*sem):
    return pltpu.CompilerParams(dimension_semantics=sem, vmem_limit_bytes=VMEM_LIMIT_BYTES)


def _rms(v):
    return v * lax.rsqrt(jnp.mean(v * v, axis=-1, keepdims=True) + EPS)


def _dot(a, b):
    return jnp.dot(a, b, preferred_element_type=F32)


def _inproj_body(x_ref, g_ref, w_ref, qg_ref, kg_ref, cqg_ref, ckvg_ref,
                 qa_ref, ka_ref, va_ref, cq_ref, ckv_ref, kpe_ref, xn_ref, *, q_scale):
    j = pl.program_id(1)

    @pl.when(j == 0)
    def _():
        xn_ref[...] = (_rms(x_ref[...]) * g_ref[...]).astype(BF16)

    def project(lo, hi):
        return _dot(xn_ref[...], w_ref[:, lo:hi])

    def head_norm(gain_ref, out_ref, post, transposed):
        for t in range(HEADS // 2):
            y = project(2 * t * HEAD_DIM, 2 * (t + 1) * HEAD_DIM)
            for u in range(2):
                sl = slice((2 * t + u) * HEAD_DIM, (2 * t + u + 1) * HEAD_DIM)
                yh = _rms(y[:, u * HEAD_DIM:(u + 1) * HEAD_DIM]) * gain_ref[...] * post
                if transposed:
                    out_ref[sl, :] = yh.T.astype(BF16)
                else:
                    out_ref[:, sl] = yh.astype(BF16)

    @pl.when(j == 0)
    def _():
        head_norm(qg_ref, qa_ref, q_scale, False)

    @pl.when(j == 1)
    def _():
        head_norm(kg_ref, ka_ref, 1.0, True)

    @pl.when(j == 2)
    def _():
        va_ref[...] = project(0, IN_TN).astype(BF16)

    @pl.when(j == 3)
    def _():
        cq_ref[...] = (_rms(project(0, Q_RANK)) * cqg_ref[...]).astype(BF16)
        y = project(Q_RANK, IN_TN)
        ckv_ref[...] = (_rms(y[:, :KV_RANK]) * ckvg_ref[...]).astype(BF16)
        kpe_ref[...] = y[:, KV_RANK:KV_RANK + LANES]


def _inproj(x, g, w4, qg, kg, cqg, ckvg, tm):
    rows = x.shape[0]
    row = lambda i, j: (i, 0)
    const = lambda i, j: (0, 0)
    outs = (
        jax.ShapeDtypeStruct((rows, NA_WIDTH), BF16),
        jax.ShapeDtypeStruct((NA_WIDTH, rows), BF16),
        jax.ShapeDtypeStruct((rows, NA_WIDTH), BF16),
        jax.ShapeDtypeStruct((rows, Q_RANK), BF16),
        jax.ShapeDtypeStruct((rows, KV_RANK), BF16),
        jax.ShapeDtypeStruct((rows, LANES), F32),
    )
    return pl.pallas_call(
        functools.partial(_inproj_body, q_scale=HEAD_DIM ** -0.5 * LOG2E),
        grid=(rows // tm, 4),
        in_specs=[
            pl.BlockSpec((tm, D_MODEL), row),
            pl.BlockSpec((1, D_MODEL), const),
            pl.BlockSpec((None, D_MODEL, IN_TN), lambda i, j: (j, 0, 0)),
            pl.BlockSpec((1, HEAD_DIM), const),
            pl.BlockSpec((1, HEAD_DIM), const),
            pl.BlockSpec((1, Q_RANK), const),
            pl.BlockSpec((1, KV_RANK), const),
        ],
        out_specs=[
            pl.BlockSpec((tm, NA_WIDTH), row),
            pl.BlockSpec((NA_WIDTH, tm), lambda i, j: (0, i)),
            pl.BlockSpec((tm, NA_WIDTH), row),
            pl.BlockSpec((tm, Q_RANK), row),
            pl.BlockSpec((tm, KV_RANK), row),
            pl.BlockSpec((tm, LANES), row),
        ],
        out_shape=outs,
        scratch_shapes=[pltpu.VMEM((tm, D_MODEL), BF16)],
        compiler_params=_cparams("parallel", "arbitrary"),
        name="inproj",
    )(x, g, w4, qg, kg, cqg, ckvg)


def _mla_up_body(cq_ref, ckv_ref, kpe_ref, cos_ref, sin_ref, cost_ref, sint_ref,
                 wqt_ref, wk_ref, wvt_ref, qgt_ref, qpad_ref, kg_ref, qt_ref, k_ref, vt_ref, *, q_scale):
    half = ROPE_DIM // 2
    ckv = ckv_ref[...]
    cqt = cq_ref[...].astype(F32).T.astype(BF16)
    ckvt = ckv.astype(F32).T.astype(BF16)
    qgt = qgt_ref[...]
    cost = cost_ref[...]
    sint = sint_ref[...]
    cosf = cos_ref[...]
    sinf = sin_ref[...]
    kg = kg_ref[...]

    kpe = kpe_ref[...]
    kpe_ss = jnp.sum(kpe * kpe, axis=-1, keepdims=True)
    r = kpe * kg[:, LANES:]
    kpe_rot = r * cosf + (pltpu.roll(r, half, 1) + pltpu.roll(r, LANES - half, 1)) * sinf

    for h in range(HEADS):
        q = _dot(wqt_ref[h], cqt)
        ms = jnp.sum(q * q, axis=0, keepdims=True) * (1.0 / QK_DIM)
        qn = q * (lax.rsqrt(ms + EPS) * q_scale) * qgt
        x1 = qn[LANES:LANES + half]
        x2 = qn[LANES + half:LANES + 2 * half]
        qt_ref[h, 0:LANES, :] = qn[:LANES].astype(BF16)
        qt_ref[h, LANES:LANES + half, :] = (x1 * cost - x2 * sint).astype(BF16)
        qt_ref[h, LANES + half:LANES + 2 * half, :] = (x2 * cost + x1 * sint).astype(BF16)
        qt_ref[h, LANES + 2 * half:, :] = qpad_ref[...]
        vt_ref[h] = _dot(wvt_ref[h], ckvt).astype(BF16)

    one_lane = (lax.broadcasted_iota(jnp.int32, (1, LANES), 1) == ROPE_DIM).astype(F32)
    for t in range(HEADS // 2):
        kn2 = _dot(ckv, wk_ref[t])
        for u in range(2):
            kn = kn2[:, u * LANES:(u + 1) * LANES]
            ms = (jnp.sum(kn * kn, axis=-1, keepdims=True) + kpe_ss) * (1.0 / QK_DIM)
            rn = lax.rsqrt(ms + EPS)
            k_ref[2 * t + u] = jnp.concatenate([kn * rn * kg[:, :LANES], kpe_rot * rn + one_lane],
                                               axis=-1).astype(BF16)


def _mla_up(cq, ckv, kpe, cosf, sinf, cost, sint, wqt, wk, wvt, qg, kg, shift, nb, tm):
    rows = cq.shape[0]
    nt = rows // nb // tm
    row = lambda b, i: (b * nt + i, 0)
    pos = lambda b, i: (i, 0)
    post = lambda b, i: (0, i)
    c2 = lambda b, i: (0, 0)
    c3 = lambda b, i: (0, 0, 0)
    qgt = jnp.broadcast_to(qg.reshape(QK_DIM, 1), (QK_DIM, tm))
    qpad = jnp.zeros((QK_PAD - QK_DIM, tm), F32).at[0].set(-shift).astype(BF16)
    outs = (
        jax.ShapeDtypeStruct((nb, HEADS, QK_PAD, nt * tm), BF16),
        jax.ShapeDtypeStruct((nb, HEADS, nt, tm, QK_PAD), BF16),
        jax.ShapeDtypeStruct((nb, HEADS, nt, HEAD_DIM, tm), BF16),
    )
    return pl.pallas_call(
        functools.partial(_mla_up_body, q_scale=QK_DIM ** -0.5 * LOG2E),
        grid=(nb, nt),
        in_specs=[
            pl.BlockSpec((tm, Q_RANK), row),
            pl.BlockSpec((tm, KV_RANK), row),
            pl.BlockSpec((tm, LANES), row),
            pl.BlockSpec((tm, LANES), pos),
            pl.BlockSpec((tm, LANES), pos),
            pl.BlockSpec((ROPE_DIM // 2, tm), post),
            pl.BlockSpec((ROPE_DIM // 2, tm), post),
            pl.BlockSpec((HEADS, QK_DIM, Q_RANK), c3),
            pl.BlockSpec((HEADS // 2, KV_RANK, 2 * HEAD_DIM), c3),
            pl.BlockSpec((HEADS, HEAD_DIM, KV_RANK), c3),
            pl.BlockSpec((QK_DIM, tm), c2),
            pl.BlockSpec((QK_PAD - QK_DIM, tm), c2),
            pl.BlockSpec((1, QK_PAD), c2),
        ],
        out_specs=[
            pl.BlockSpec((None, HEADS, QK_PAD, tm), lambda b, i: (b, 0, 0, i)),
            pl.BlockSpec((None, HEADS, None, tm, QK_PAD), lambda b, i: (b, 0, i, 0, 0)),
            pl.BlockSpec((None, HEADS, None, HEAD_DIM, tm), lambda b, i: (b, 0, i, 0, 0)),
        ],
        out_shape=outs,
        compiler_params=_cparams("parallel", "parallel"),
        name="mla_up",
    )(cq, ckv, kpe, cosf, sinf, cost, sint, wqt, wk, wvt, qgt, qpad, kg)


def _mla_attn_body(qt_ref, qtn_ref, k_ref, vt_ref, km_ref, vmt_ref, o_ref, acc_ref, s_ref,
                   *, nk, carry_over):
    qt = qt_ref[...]
    ng = nk // ATT_GROUP

    def scores(g, half, q=None):
        q = qt if q is None else q
        for u in range(ATT_GROUP):
            s_ref[half * ATT_GROUP + u] = _dot(k_ref[g * ATT_GROUP + u], q)

    def attend(g, half, l):
        pv = None
        for u in range(ATT_GROUP):
            p = jnp.exp2(s_ref[half * ATT_GROUP + u])
            l = l + jnp.sum(p, axis=0, keepdims=True)
            d = _dot(vt_ref[g * ATT_GROUP + u], p.astype(BF16))
            pv = d if pv is None else pv + d
        acc_ref[...] += pv
        return l

    def meta_scores():
        s = _dot(km_ref[...], qt)
        key = lax.broadcasted_iota(jnp.int32, s.shape, 0)
        return jnp.where(key < N_META, s, NEG)

    def two_stages(i, l):
        scores(2 * i + 1, 1)
        l = attend(2 * i, 0, l)
        scores(2 * i + 2, 0)
        return attend(2 * i + 1, 1, l)

    if carry_over:
        @pl.when(pl.program_id(2) == 0)
        def _():
            scores(0, 0)
    else:
        scores(0, 0)

    p = jnp.exp2(meta_scores())
    acc_ref[...] = _dot(vmt_ref[...], p.astype(BF16))
    l = lax.fori_loop(0, ng // 2 - 1, two_stages, jnp.sum(p, axis=0, keepdims=True))
    scores(ng - 1, 1)
    l = attend(ng - 2, 0, l)
    if carry_over:
        scores(0, 0, qtn_ref[...])
    l = attend(ng - 1, 1, l)
    o_ref[...] = acc_ref[...] / l

    @pl.when(jnp.min(l) < SOFTMAX_UNDERFLOW)
    def _():
        s = meta_scores()
        m0 = jnp.max(s, axis=0, keepdims=True)
        p = jnp.exp2(s - m0)
        acc_ref[...] = _dot(vmt_ref[...], p.astype(BF16))

        def chunk(c, carry):
            m_prev, l_prev = carry
            s = _dot(k_ref[c], qt)
            m_new = jnp.maximum(m_prev, jnp.max(s, axis=0, keepdims=True))
            alpha = jnp.exp2(m_prev - m_new)
            p = jnp.exp2(s - m_new)
            acc_ref[...] = alpha * acc_ref[...] + _dot(vt_ref[c], p.astype(BF16))
            return m_new, alpha * l_prev + jnp.sum(p, axis=0, keepdims=True)

        _, l_exact = lax.fori_loop(0, nk, chunk, (m0, jnp.sum(p, axis=0, keepdims=True)))
        o_ref[...] = acc_ref[...] / l_exact


def _mla_attn(qt, k, vt, km, vmt, tq, shared_q):
    nb, _, nk, tk, _ = k.shape
    nq = qt.shape[3] // tq
    qb = (lambda b: 0) if shared_q else (lambda b: b)
    return pl.pallas_call(
        functools.partial(_mla_attn_body, nk=nk, carry_over=nq > 1),
        grid=(nb, HEADS, nq),
        in_specs=[
            pl.BlockSpec((None, None, QK_PAD, tq), lambda b, h, i: (qb(b), h, 0, i)),
            pl.BlockSpec((None, None, QK_PAD, tq), lambda b, h, i: (qb(b), h, 0, jnp.minimum(i + 1, nq - 1))),
            pl.BlockSpec((None, None, nk, tk, QK_PAD), lambda b, h, i: (b, h, 0, 0, 0),
                         pipeline_mode=pl.Buffered(1 if nq > 1 else 2)),
            pl.BlockSpec((None, None, nk, HEAD_DIM, tk), lambda b, h, i: (b, h, 0, 0, 0),
                         pipeline_mode=pl.Buffered(1 if nq > 1 else 2)),
            pl.BlockSpec((None, None, None, META_ROWS, QK_PAD), lambda b, h, i: (0, h, 0, 0, 0)),
            pl.BlockSpec((None, None, None, HEAD_DIM, META_ROWS), lambda b, h, i: (0, h, 0, 0, 0)),
        ],
        out_specs=pl.BlockSpec((None, HEAD_DIM, tq), lambda b, h, i: (b, h, i)),
        out_shape=jax.ShapeDtypeStruct((nb, NA_WIDTH, nq * tq), F32),
        scratch_shapes=[pltpu.VMEM((HEAD_DIM, tq), F32), pltpu.VMEM((2 * ATT_GROUP, tk, tq), F32)],
        compiler_params=_cparams("parallel", "parallel", "arbitrary"),
        name="mla_attn",
    )(qt, qt, k, vt, km, vmt)


def _na_bias_body(rpb_ref, shift_ref, o_ref, t_ref, *, rows):
    n_dr = 2 * NA_WIN_ROWS - 1
    shape = (GRID_W, LANES)
    qc = lax.broadcasted_iota(jnp.int32, shape, 0)
    lane = lax.broadcasted_iota(jnp.int32, shape, 1)
    left = lane < GRID_W
    kc = jnp.where(left, lane, lane - GRID_W)
    cs = jnp.clip(qc - NA_WIN_COLS // 2, 0, GRID_W - NA_WIN_COLS)
    col_ok = (kc >= cs) & (kc < cs + NA_WIN_COLS)

    for dr in range(n_dr):
        x = jnp.broadcast_to(rpb_ref[dr:dr + 1, :], shape)
        a = pltpu.roll(x, LANES - (NA_WIN_COLS - 1), 1, stride=1, stride_axis=0)
        b = pltpu.roll(x, GRID_W - (NA_WIN_COLS - 1), 1, stride=1, stride_axis=0)
        t_ref[dr] = jnp.where(col_ok, jnp.where(left, a, b) * LOG2E - shift_ref[0], NEG)

    neg = jnp.full(shape, NEG, F32)
    for c, r0 in enumerate((0, 2 * NA_QROWS, rows - NA_QROWS)):
        for qr in range(NA_QROWS):
            abs_qr = r0 + qr
            rs = min(max(abs_qr - NA_WIN_ROWS // 2, 0), rows - NA_WIN_ROWS)
            for a in range(NA_KROWS // 2):
                halves = []
                for kr in (r0 - NA_QROWS + 2 * a, r0 - NA_QROWS + 2 * a + 1):
                    in_window = rs <= kr < rs + NA_WIN_ROWS
                    halves.append(t_ref[kr - abs_qr + NA_WIN_ROWS - 1] if in_window else neg)
                o_ref[c, qr * GRID_W:(qr + 1) * GRID_W, a * LANES:(a + 1) * LANES] = (
                    jnp.where(left, halves[0], halves[1]))


def _na_bias(rpb, shift, rows):
    n_dr, n_dc = rpb.shape[1:]
    rpb = jnp.pad(rpb, ((0, 0), (0, -n_dr % 8), (0, LANES - n_dc)))
    return pl.pallas_call(
        functools.partial(_na_bias_body, rows=rows),
        grid=(HEADS,),
        in_specs=[pl.BlockSpec((None,) + rpb.shape[1:], lambda h: (h, 0, 0)),
                  pl.BlockSpec(memory_space=pltpu.SMEM)],
        out_specs=pl.BlockSpec((3, None, NA_TQ, NA_KROWS * GRID_W), lambda h: (0, h, 0, 0)),
        out_shape=jax.ShapeDtypeStruct((3, HEADS, NA_TQ, NA_KROWS * GRID_W), F32),
        scratch_shapes=[pltpu.VMEM((2 * NA_WIN_ROWS - 1, GRID_W, LANES), F32)],
        compiler_params=_cparams("parallel"),
        name="na_bias",
    )(rpb, shift.reshape(1))


def _na_body(q_ref, kp_ref, kc_ref, kn_ref, vp_ref, vc_ref, vn_ref, km_ref, vm_ref,
             bias_ref, mb_ref, g_ref, o_ref, out_ref, s_ref):
    for h in range(HEADS):
        sl = slice(h * HEAD_DIM, (h + 1) * HEAD_DIM)
        q = q_ref[:, sl]
        for c, k_ref in enumerate((kp_ref, kc_ref, kn_ref)):
            cols = slice(c * NA_TQ, (c + 1) * NA_TQ)
            s_ref[h, :, cols] = _dot(q, k_ref[sl, :]) + bias_ref[h, :, cols]
        s_ref[h, :, 3 * NA_TQ:] = _dot(q, km_ref[sl, :]) + mb_ref[h]
    def softmax_pv(running_max):
        l_min = None
        for h in range(HEADS):
            sl = slice(h * HEAD_DIM, (h + 1) * HEAD_DIM)
            s = s_ref[h]
            if running_max:
                s = s - jnp.max(s, axis=-1, keepdims=True)
            p = jnp.exp2(s)
            l = jnp.sum(p, axis=-1, keepdims=True)
            pb = p.astype(BF16)
            o = (_dot(pb[:, 3 * NA_TQ:], vm_ref[:, sl])
                 + _dot(pb[:, :NA_TQ], vp_ref[:, sl])
                 + _dot(pb[:, NA_TQ:2 * NA_TQ], vc_ref[:, sl])
                 + _dot(pb[:, 2 * NA_TQ:3 * NA_TQ], vn_ref[:, sl]))
            out_ref[:, sl] = o / l
            l_min = l if l_min is None else jnp.minimum(l_min, l)
        return l_min

    l_min = softmax_pv(False)

    @pl.when(jnp.min(l_min) < SOFTMAX_UNDERFLOW)
    def _():
        softmax_pv(True)

    o_ref[...] = (_rms(out_ref[...]) * g_ref[...]).astype(BF16)


def _na_attn(qa, kat, va, kmt, vm, bias, mb, g, nb):
    rows = qa.shape[0]
    n = rows // nb // NA_TQ
    cur = lambda b, i: (b * n + i, 0)
    prev = lambda b, i: (b * n + jnp.maximum(i - 1, 0), 0)
    nxt = lambda b, i: (b * n + jnp.minimum(i + 1, n - 1), 0)
    blk = lambda f: pl.BlockSpec((NA_TQ, NA_WIDTH), f)
    blk_t = lambda f: pl.BlockSpec((NA_WIDTH, NA_TQ), lambda b, i: f(b, i)[::-1])
    case = lambda b, i: (jnp.where(i == 0, 0, jnp.where(i == n - 1, 2, 1)), 0, 0, 0)
    return pl.pallas_call(
        _na_body,
        grid=(nb, n),
        in_specs=[
            blk(cur), blk_t(prev), blk_t(cur), blk_t(nxt), blk(prev), blk(cur), blk(nxt),
            pl.BlockSpec((NA_WIDTH, META_ROWS), lambda b, i: (0, 0)),
            pl.BlockSpec((META_ROWS, NA_WIDTH), lambda b, i: (0, 0)),
            pl.BlockSpec((None, HEADS, NA_TQ, NA_KROWS * GRID_W), case),
            pl.BlockSpec((HEADS, 1, META_ROWS), lambda b, i: (0, 0, 0)),
            pl.BlockSpec((1, NA_WIDTH), lambda b, i: (0, 0)),
        ],
        out_specs=blk(cur),
        out_shape=jax.ShapeDtypeStruct((rows, NA_WIDTH), BF16),
        scratch_shapes=[pltpu.VMEM((NA_TQ, NA_WIDTH), F32),
                        pltpu.VMEM((HEADS, NA_TQ, NA_KROWS * GRID_W + META_ROWS), F32)],
        compiler_params=_cparams("parallel", "arbitrary"),
        name="na_attn",
    )(qa, kat, kat, kat, va, va, va, kmt, vm, bias, mb, g)


def _na_meta_body(q_ref, kt_ref, v_ref, mb_ref, g_ref, o_ref, out_ref):
    for h in range(HEADS):
        sl = slice(h * HEAD_DIM, (h + 1) * HEAD_DIM)
        s = _dot(q_ref[:, sl], kt_ref[sl, :]) + mb_ref[h]
        m = jnp.max(s, axis=-1, keepdims=True)
        p = jnp.exp2(s - m)
        l = jnp.sum(p, axis=-1, keepdims=True)
        out_ref[:, sl] = _dot(p.astype(BF16), v_ref[:, sl]) / l
    o_ref[...] = (_rms(out_ref[...]) * g_ref[...]).astype(BF16)


def _na_meta(qm, kmt, vm, mb, g):
    full = lambda s: pl.BlockSpec(s, lambda i: (0,) * len(s))
    return pl.pallas_call(
        _na_meta_body,
        grid=(1,),
        in_specs=[full((META_ROWS, NA_WIDTH)), full((NA_WIDTH, META_ROWS)), full((META_ROWS, NA_WIDTH)),
                  full((HEADS, 1, META_ROWS)), full((1, NA_WIDTH))],
        out_specs=full((META_ROWS, NA_WIDTH)),
        out_shape=jax.ShapeDtypeStruct((META_ROWS, NA_WIDTH), BF16),
        scratch_shapes=[pltpu.VMEM((META_ROWS, NA_WIDTH), F32)],
        compiler_params=_cparams("arbitrary"),
        name="na_meta",
    )(qm, kmt, vm, mb, g)


def _outproj_body(x_ref, a_ref, bt_ref, bg_ref, w_ref, fg_ref, h_ref, xn_ref):
    bn = (_rms(bt_ref[...].T) * bg_ref[...]).astype(BF16)
    mix = jnp.concatenate([a_ref[...], bn], axis=-1)
    h = x_ref[...] + _dot(mix, w_ref[...])
    h_ref[...] = h
    xn_ref[...] = (_rms(h) * fg_ref[...]).astype(BF16)


def _outproj(x, a, bt, bg, w, fg, tm, x_map, a_map, rows):
    c = lambda i: (0, 0)
    tpb = bt.shape[2] // tm
    return pl.pallas_call(
        _outproj_body,
        grid=(rows // tm,),
        in_specs=[
            pl.BlockSpec((tm, D_MODEL), x_map),
            pl.BlockSpec((tm, NA_WIDTH), a_map),
            pl.BlockSpec((None, NA_WIDTH, tm), lambda i: (i // tpb, 0, i % tpb)),
            pl.BlockSpec((1, NA_WIDTH), c),
            pl.BlockSpec((2 * NA_WIDTH, D_MODEL), c),
            pl.BlockSpec((1, D_MODEL), c),
        ],
        out_specs=[pl.BlockSpec((tm, D_MODEL), lambda i: (i, 0)),
                   pl.BlockSpec((tm, D_MODEL), lambda i: (i, 0))],
        out_shape=(jax.ShapeDtypeStruct((rows, D_MODEL), F32),
                   jax.ShapeDtypeStruct((rows, D_MODEL), BF16)),
        compiler_params=_cparams("parallel"),
        name="outproj",
    )(x, a, bt, bg, w, fg)


def _ffn_body(xm_ref, xp_ref, xx_ref, xmeta_ref, h_ref, wg_ref, wu_ref, wd_ref, cw_ref, cb_ref,
              o_ref, xe_ref, *, tm, tpb):
    i = pl.program_id(0)
    j = pl.program_id(1)

    @pl.when(j == 0)
    def _():
        first = (i % tpb) == 0
        last = (i % tpb) == tpb - 1
        xe_ref[0:HALO, :] = jnp.where(first, xmeta_ref[...], xp_ref[...])
        xe_ref[HALO:HALO + tm, :] = xm_ref[...]
        xe_ref[HALO + tm:2 * HALO + tm, :] = jnp.where(last, jnp.zeros_like(xx_ref[...]), xx_ref[...])
        o_ref[...] = h_ref[...]

    g = _dot(xe_ref[...], wg_ref[...])
    u = _dot(xe_ref[HALO:HALO + tm, :], wu_ref[...])
    cw = cw_ref[...]
    gc = (cb_ref[...] + cw[0:1] * g[HALO - 1:HALO - 1 + tm]
          + cw[1:2] * g[HALO:HALO + tm] + cw[2:3] * g[HALO + 1:HALO + 1 + tm])
    act = gc * (1.0 / (1.0 + jnp.exp(-gc))) * u
    o_ref[...] += _dot(act.astype(BF16), wd_ref[...])


def _ffn(xn, xn_meta, h1, wg, wu, wd, cw, cb, nb, tm, tf):
    rows = xn.shape[0]
    nt = rows // tm
    tpb = nt // nb
    hb = tm // HALO
    last_hb = rows // HALO - 1
    return pl.pallas_call(
        functools.partial(_ffn_body, tm=tm, tpb=tpb),
        grid=(nt, D_FF // tf),
        in_specs=[
            pl.BlockSpec((tm, D_MODEL), lambda i, j: (i, 0)),
            pl.BlockSpec((HALO, D_MODEL), lambda i, j: (jnp.maximum(i * hb - 1, 0), 0)),
            pl.BlockSpec((HALO, D_MODEL), lambda i, j: (jnp.minimum((i + 1) * hb, last_hb), 0)),
            pl.BlockSpec((HALO, D_MODEL), lambda i, j: ((i // tpb) * (META_ROWS // HALO), 0)),
            pl.BlockSpec((tm, D_MODEL), lambda i, j: (i, 0)),
            pl.BlockSpec((D_MODEL, tf), lambda i, j: (0, j)),
            pl.BlockSpec((D_MODEL, tf), lambda i, j: (0, j)),
            pl.BlockSpec((tf, D_MODEL), lambda i, j: (j, 0)),
            pl.BlockSpec((3, tf), lambda i, j: (0, j)),
            pl.BlockSpec((1, tf), lambda i, j: (0, j)),
        ],
        out_specs=pl.BlockSpec((tm, D_MODEL), lambda i, j: (i, 0)),
        out_shape=jax.ShapeDtypeStruct((rows, D_MODEL), F32),
        scratch_shapes=[pltpu.VMEM((tm + 2 * HALO, D_MODEL), BF16)],
        compiler_params=_cparams("parallel", "arbitrary"),
        name="ffn",
    )(xn, xn, xn, xn_meta, h1, wg, wu, wd, cw, cb)


def _rope_tables(first, n):
    inv = ROPE_THETA ** (-np.arange(0, ROPE_DIM, 2, dtype=np.float64) / ROPE_DIM)
    ang = (first + np.arange(n, dtype=np.float64))[:, None] * inv[None, :]
    cos, sin = np.cos(ang), np.sin(ang)
    z = np.zeros_like(cos)
    tables = (np.concatenate([cos, cos, z, z], axis=-1), np.concatenate([-sin, sin, z, z], axis=-1), cos.T, sin.T)
    return tuple(jnp.asarray(t, F32) for t in tables)


def _softmax_shift(q_gain, k_gain):
    bound = LOG2E * QK_DIM ** 0.5 * jnp.max(jnp.abs(q_gain)) * jnp.max(jnp.abs(k_gain))
    return bound * (1.02 * (1.0 + 2.0 ** -7))


def _na_shift(q_gain, k_gain, rpb, meta_bias):
    qk = LOG2E * HEAD_DIM ** 0.5 * jnp.max(jnp.abs(q_gain)) * jnp.max(jnp.abs(k_gain))
    return 1.02 * qk + LOG2E * jnp.maximum(jnp.max(rpb), jnp.max(meta_bias))


def _pad_lanes(v, width):
    return jnp.pad(v, ((0, 0), (0, width - v.shape[-1])))


def kernel(x, meta_tokens, mix_norm_g, w_in, na_q_g, na_k_g, na_rpb, na_meta_bias, mla_cq_g, mla_ckv_g,
           w_q_up, w_kv_up, mla_q_g, mla_k_g, na_out_g, mla_out_g, w_out, ffn_norm_g, w_gate, w_up,
           conv_w, conv_b, w_down):
    nb, seq, d = x.shape
    assert d == D_MODEL and w_in.shape[0] == 1, "one layer of width 2048"
    rows_grid = seq // GRID_W
    assert seq % max(IN_TM, UP_TM, ATT_TQ, FFN_TM) == 0 and rows_grid >= 4 * NA_QROWS

    w_in_p = _pad_lanes(w_in[0], 4 * IN_TN).astype(BF16)
    w4 = w_in_p.reshape(D_MODEL, 4, IN_TN).transpose(1, 0, 2)
    wqt = w_q_up[0].reshape(Q_RANK, HEADS, QK_DIM).transpose(1, 2, 0).astype(BF16)
    wkv = w_kv_up[0].reshape(KV_RANK, HEADS, 2 * HEAD_DIM)
    wk = wkv[:, :, :HEAD_DIM].reshape(KV_RANK, HEADS // 2, 2 * HEAD_DIM).transpose(1, 0, 2).astype(BF16)
    wvt = wkv[:, :, HEAD_DIM:].transpose(1, 2, 0).astype(BF16)
    wo = w_out[0].astype(BF16)
    wg, wu, wd = w_gate[0].astype(BF16), w_up[0].astype(BF16), w_down[0].astype(BF16)
    row = lambda v: v.reshape(1, -1)
    kg_pad = _pad_lanes(mla_k_g, QK_PAD)
    na_shift = _na_shift(na_q_g, na_k_g, na_rpb, na_meta_bias)
    mb = jnp.pad(na_meta_bias[0] * LOG2E - na_shift, ((0, 0), (0, META_ROWS - N_META)),
                 constant_values=NEG).reshape(HEADS, 1, META_ROWS)

    xr = x.reshape(nb * seq, D_MODEL)
    xm = jnp.pad(meta_tokens.astype(x.dtype), ((0, META_ROWS - N_META), (0, 0)))
    rope_r = _rope_tables(N_META, seq)
    rope_m = _rope_tables(0, META_ROWS)

    inproj = functools.partial(_inproj, g=row(mix_norm_g), w4=w4, qg=row(na_q_g), kg=row(na_k_g),
                               cqg=row(mla_cq_g), ckvg=row(mla_ckv_g))
    qa, ka, va, cq, ckv, kpe = inproj(xr, tm=IN_TM)
    qa_m, ka_m, va_m, cq_m, ckv_m, kpe_m = inproj(xm, tm=META_ROWS)

    up = functools.partial(_mla_up, wqt=wqt, wk=wk, wvt=wvt, qg=mla_q_g, kg=kg_pad,
                           shift=_softmax_shift(mla_q_g, mla_k_g))
    qt, kk, vt = up(cq, ckv, kpe, *rope_r, nb=nb, tm=UP_TM)
    qt_m, kk_m, vt_m = up(cq_m, ckv_m, kpe_m, *rope_m, nb=1, tm=META_ROWS)

    bias = _na_bias(na_rpb[0], na_shift, rows_grid)
    a_n = _na_attn(qa, ka, va, ka_m, va_m, bias, mb, row(na_out_g), nb)
    a_n_m = _na_meta(qa_m, ka_m, va_m, mb, row(na_out_g))

    out_b = _mla_attn(qt, kk, vt, kk_m, vt_m, ATT_TQ, shared_q=False)
    out_b_m = _mla_attn(qt_m, kk, vt, kk_m, vt_m, META_ROWS, shared_q=True)

    op = functools.partial(_outproj, bg=row(mla_out_g), w=wo, fg=row(ffn_norm_g))
    ident = lambda i: (i, 0)
    zero = lambda i: (0, 0)
    h1, xn2 = op(xr, a_n, out_b, tm=OUT_TM, x_map=ident, a_map=ident, rows=nb * seq)
    _, xn2_m = op(xm, a_n_m, out_b_m, tm=META_ROWS, x_map=zero, a_map=zero, rows=nb * META_ROWS)

    out = _ffn(xn2, xn2_m, h1, wg, wu, wd, conv_w[0], row(conv_b), nb, FFN_TM, FFN_TF)
    return out.reshape(nb, seq, D_MODEL)
```

```python
import functools

import jax
import jax.numpy as jnp
import numpy as np
from jax import lax
from jax.experimental import pallas as pl
from jax.experimental.pallas import tpu as pltpu

F32 = jnp.float32
BF16 = jnp.bfloat16

LANES = 128
BF16_SUBLANES = 16
VMEM_LIMIT_BYTES = 56 * 1024 * 1024

D_MODEL = 2048
GRID_W = 64
N_META = 16
EPS = 1e-6
NEG = -1e30
LOG2E = 1.4426950408889634
HEADS = 8
HEAD_DIM = 128
NA_WIDTH = HEADS * HEAD_DIM
NA_WIN_ROWS = 8
NA_WIN_COLS = 16
Q_RANK = 512
KV_RANK = 256
ROPE_DIM = 64
QK_DIM = HEAD_DIM + ROPE_DIM
QK_PAD = 2 * LANES
ROPE_THETA = 10000.0
D_FF = 5632
META_ROWS = LANES
SOFTMAX_UNDERFLOW = 2.0 ** -100

IN_TM = 1024
IN_TN = 1024
UP_TM = 512
ATT_TQ = 1024
ATT_GROUP = 4
NA_QROWS = 4
NA_TQ = NA_QROWS * GRID_W
NA_KROWS = 3 * NA_QROWS
OUT_TM = 256
FFN_TM = 1024
FFN_TF = 512
HALO = BF16_SUBLANES


def _cparams(*sem):
    return pltpu.CompilerParams(dimension_semantics=sem, vmem_limit_bytes=VMEM_LIMIT_BYTES)


def _rms(v):
    return v * lax.rsqrt(jnp.mean(v * v, axis=-1, keepdims=True) + EPS)


def _dot(a, b):
    return jnp.dot(a, b, preferred_element_type=F32)


def _inproj_body(x_ref, g_ref, w_ref, qg_ref, kg_ref, cqg_ref, ckvg_ref,
                 qa_ref, ka_ref, va_ref, cq_ref, ckv_ref, kpe_ref, xn_ref, *, q_scale):
    j = pl.program_id(1)

    @pl.when(j == 0)
    def _():
        xn_ref[...] = (_rms(x_ref[...]) * g_ref[...]).astype(BF16)

    def project(lo, hi):
        return _dot(xn_ref[...], w_ref[:, lo:hi])

    def head_norm(gain_ref, out_ref, post, transposed):
        for t in range(HEADS // 2):
            y = project(2 * t * HEAD_DIM, 2 * (t + 1) * HEAD_DIM)
            for u in range(2):
                sl = slice((2 * t + u) * HEAD_DIM, (2 * t + u + 1) * HEAD_DIM)
                yh = _rms(y[:, u * HEAD_DIM:(u + 1) * HEAD_DIM]) * gain_ref[...] * post
                if transposed:
                    out_ref[sl, :] = yh.T.astype(BF16)
                else:
                    out_ref[:, sl] = yh.astype(BF16)

    @pl.when(j == 0)
    def _():
        head_norm(qg_ref, qa_ref, q_scale, False)

    @pl.when(j == 1)
    def _():
        head_norm(kg_ref, ka_ref, 1.0, True)

    @pl.when(j == 2)
    def _():
        va_ref[...] = project(0, IN_TN).astype(BF16)

    @pl.when(j == 3)
    def _():
        cq_ref[...] = (_rms(project(0, Q_RANK)) * cqg_ref[...]).astype(BF16)
        y = project(Q_RANK, IN_TN)
        ckv_ref[...] = (_rms(y[:, :KV_RANK]) * ckvg_ref[...]).astype(BF16)
        kpe_ref[...] = y[:, KV_RANK:KV_RANK + LANES]


def _inproj(x, g, w4, qg, kg, cqg, ckvg, tm):
    rows = x.shape[0]
    row = lambda i, j: (i, 0)
    const = lambda i, j: (0, 0)
    outs = (
        jax.ShapeDtypeStruct((rows, NA_WIDTH), BF16),
        jax.ShapeDtypeStruct((NA_WIDTH, rows), BF16),
        jax.ShapeDtypeStruct((rows, NA_WIDTH), BF16),
        jax.ShapeDtypeStruct((rows, Q_RANK), BF16),
        jax.ShapeDtypeStruct((rows, KV_RANK), BF16),
        jax.ShapeDtypeStruct((rows, LANES), F32),
    )
    return pl.pallas_call(
        functools.partial(_inproj_body, q_scale=HEAD_DIM ** -0.5 * LOG2E),
        grid=(rows // tm, 4),
        in_specs=[
            pl.BlockSpec((tm, D_MODEL), row),
            pl.BlockSpec((1, D_MODEL), const),
            pl.BlockSpec((None, D_MODEL, IN_TN), lambda i, j: (j, 0, 0)),
            pl.BlockSpec((1, HEAD_DIM), const),
            pl.BlockSpec((1, HEAD_DIM), const),
            pl.BlockSpec((1, Q_RANK), const),
            pl.BlockSpec((1, KV_RANK), const),
        ],
        out_specs=[
            pl.BlockSpec((tm, NA_WIDTH), row),
            pl.BlockSpec((NA_WIDTH, tm), lambda i, j: (0, i)),
            pl.BlockSpec((tm, NA_WIDTH), row),
            pl.BlockSpec((tm, Q_RANK), row),
            pl.BlockSpec((tm, KV_RANK), row),
            pl.BlockSpec((tm, LANES), row),
        ],
        out_shape=outs,
        scratch_shapes=[pltpu.VMEM((tm, D_MODEL), BF16)],
        compiler_params=_cparams("parallel", "arbitrary"),
        name="inproj",
    )(x, g, w4, qg, kg, cqg, ckvg)


def _mla_up_body(cq_ref, ckv_ref, kpe_ref, cos_ref, sin_ref, cost_ref, sint_ref,
                 wqt_ref, wk_ref, wvt_ref, qgt_ref, qpad_ref, kg_ref, qt_ref, k_ref, vt_ref, *, q_scale):
    half = ROPE_DIM // 2
    ckv = ckv_ref[...]
    cqt = cq_ref[...].astype(F32).T.astype(BF16)
    ckvt = ckv.astype(F32).T.astype(BF16)
    qgt = qgt_ref[...]
    cost = cost_ref[...]
    sint = sint_ref[...]
    cosf = cos_ref[...]
    sinf = sin_ref[...]
    kg = kg_ref[...]

    kpe = kpe_ref[...]
    kpe_ss = jnp.sum(kpe * kpe, axis=-1, keepdims=True)
    r = kpe * kg[:, LANES:]
    kpe_rot = r * cosf + (pltpu.roll(r, half, 1) + pltpu.roll(r, LANES - half, 1)) * sinf

    for h in range(HEADS):
        q = _dot(wqt_ref[h], cqt)
        ms = jnp.sum(q * q, axis=0, keepdims=True) * (1.0 / QK_DIM)
        qn = q * (lax.rsqrt(ms + EPS) * q_scale) * qgt
        x1 = qn[LANES:LANES + half]
        x2 = qn[LANES + half:LANES + 2 * half]
        qt_ref[h, 0:LANES, :] = qn[:LANES].astype(BF16)
        qt_ref[h, LANES:LANES + half, :] = (x1 * cost - x2 * sint).astype(BF16)
        qt_ref[h, LANES + half:LANES + 2 * half, :] = (x2 * cost + x1 * sint).astype(BF16)
        qt_ref[h, LANES + 2 * half:, :] = qpad_ref[...]
        vt_ref[h] = _dot(wvt_ref[h], ckvt).astype(BF16)

    one_lane = (lax.broadcasted_iota(jnp.int32, (1, LANES), 1) == ROPE_DIM).astype(F32)
    for t in range(HEADS // 2):
        kn2 = _dot(ckv, wk_ref[t])
        for u in range(2):
            kn = kn2[:, u * LANES:(u + 1) * LANES]
            ms = (jnp.sum(kn * kn, axis=-1, keepdims=True) + kpe_ss) * (1.0 / QK_DIM)
            rn = lax.rsqrt(ms + EPS)
            k_ref[2 * t + u] = jnp.concatenate([kn * rn * kg[:, :LANES], kpe_rot * rn + one_lane],
                                               axis=-1).astype(BF16)


def _mla_up(cq, ckv, kpe, cosf, sinf, cost, sint, wqt, wk, wvt, qg, kg, shift, nb, tm):
    rows = cq.shape[0]
    nt = rows // nb // tm
    row = lambda b, i: (b * nt + i, 0)
    pos = lambda b, i: (i, 0)
    post = lambda b, i: (0, i)
    c2 = lambda b, i: (0, 0)
    c3 = lambda b, i: (0, 0, 0)
    qgt = jnp.broadcast_to(qg.reshape(QK_DIM, 1), (QK_DIM, tm))
    qpad = jnp.zeros((QK_PAD - QK_DIM, tm), F32).at[0].set(-shift).astype(BF16)
    outs = (
        jax.ShapeDtypeStruct((nb, HEADS, QK_PAD, nt * tm), BF16),
        jax.ShapeDtypeStruct((nb, HEADS, nt, tm, QK_PAD), BF16),
        jax.ShapeDtypeStruct((nb, HEADS, nt, HEAD_DIM, tm), BF16),
    )
    return pl.pallas_call(
        functools.partial(_mla_up_body, q_scale=QK_DIM ** -0.5 * LOG2E),
        grid=(nb, nt),
        in_specs=[
            pl.BlockSpec((tm, Q_RANK), row),
            pl.BlockSpec((tm, KV_RANK), row),
            pl.BlockSpec((tm, LANES), row),
            pl.BlockSpec((tm, LANES), pos),
            pl.BlockSpec((tm, LANES), pos),
            pl.BlockSpec((ROPE_DIM // 2, tm), post),
            pl.BlockSpec((ROPE_DIM // 2, tm), post),
            pl.BlockSpec((HEADS, QK_DIM, Q_RANK), c3),
            pl.BlockSpec((HEADS // 2, KV_RANK, 2 * HEAD_DIM), c3),
            pl.BlockSpec((HEADS, HEAD_DIM, KV_RANK), c3),
            pl.BlockSpec((QK_DIM, tm), c2),
            pl.BlockSpec((QK_PAD - QK_DIM, tm), c2),
            pl.BlockSpec((1, QK_PAD), c2),
        ],
        out_specs=[
            pl.BlockSpec((None, HEADS, QK_PAD, tm), lambda b, i: (b, 0, 0, i)),
            pl.BlockSpec((None, HEADS, None, tm, QK_PAD), lambda b, i: (b, 0, i, 0, 0)),
            pl.BlockSpec((None, HEADS, None, HEAD_DIM, tm), lambda b, i: (b, 0, i, 0, 0)),
        ],
        out_shape=outs,
        compiler_params=_cparams("parallel", "parallel"),
        name="mla_up",
    )(cq, ckv, kpe, cosf, sinf, cost, sint, wqt, wk, wvt, qgt, qpad, kg)


def _mla_attn_body(qt_ref, qtn_ref, k_ref, vt_ref, km_ref, vmt_ref, o_ref, acc_ref, s_ref,
                   *, nk, carry_over):
    qt = qt_ref[...]
    ng = nk // ATT_GROUP

    def scores(g, half, q=None):
        q = qt if q is None else q
        for u in range(ATT_GROUP):
            s_ref[half * ATT_GROUP + u] = _dot(k_ref[g * ATT_GROUP + u], q)

    def attend(g, half, l):
        pv = None
        for u in range(ATT_GROUP):
            p = jnp.exp2(s_ref[half * ATT_GROUP + u])
            l = l + jnp.sum(p, axis=0, keepdims=True)
            d = _dot(vt_ref[g * ATT_GROUP + u], p.astype(BF16))
            pv = d if pv is None else pv + d
        acc_ref[...] += pv
        return l

    def meta_scores():
        s = _dot(km_ref[...], qt)
        key = lax.broadcasted_iota(jnp.int32, s.shape, 0)
        return jnp.where(key < N_META, s, NEG)

    def two_stages(i, l):
        scores(2 * i + 1, 1)
        l = attend(2 * i, 0, l)
        scores(2 * i + 2, 0)
        return attend(2 * i + 1, 1, l)

    if carry_over:
        @pl.when(pl.program_id(2) == 0)
        def _():
            scores(0, 0)
    else:
        scores(0, 0)

    p = jnp.exp2(meta_scores())
    acc_ref[...] = _dot(vmt_ref[...], p.astype(BF16))
    l = lax.fori_loop(0, ng // 2 - 1, two_stages, jnp.sum(p, axis=0, keepdims=True))
    scores(ng - 1, 1)
    l = attend(ng - 2, 0, l)
    if carry_over:
        scores(0, 0, qtn_ref[...])
    l = attend(ng - 1, 1, l)
    o_ref[...] = acc_ref[...] / l

    @pl.when(jnp.min(l) < SOFTMAX_UNDERFLOW)
    def _():
        s = meta_scores()
        m0 = jnp.max(s, axis=0, keepdims=True)
        p = jnp.exp2(s - m0)
        acc_ref[...] = _dot(vmt_ref[...], p.astype(BF16))

        def chunk(c, carry):
            m_prev, l_prev = carry
            s = _dot(k_ref[c], qt)
            m_new = jnp.maximum(m_prev, jnp.max(s, axis=0, keepdims=True))
            alpha = jnp.exp2(m_prev - m_new)
            p = jnp.exp2(s - m_new)
            acc_ref[...] = alpha * acc_ref[...] + _dot(vt_ref[c], p.astype(BF16))
            return m_new, alpha * l_prev + jnp.sum(p, axis=0, keepdims=True)

        _, l_exact = lax.fori_loop(0, nk, chunk, (m0, jnp.sum(p, axis=0, keepdims=True)))
        o_ref[...] = acc_ref[...] / l_exact


def _mla_attn(qt, k, vt, km, vmt, tq, shared_q):
    nb, _, nk, tk, _ = k.shape
    nq = qt.shape[3] // tq
    qb = (lambda b: 0) if shared_q else (lambda b: b)
    return pl.pallas_call(
        functools.partial(_mla_attn_body, nk=nk, carry_over=nq > 1),
        grid=(nb, HEADS, nq),
        in_specs=[
            pl.BlockSpec((None, None, QK_PAD, tq), lambda b, h, i: (qb(b), h, 0, i)),
            pl.BlockSpec((None, None, QK_PAD, tq), lambda b, h, i: (qb(b), h, 0, jnp.minimum(i + 1, nq - 1))),
            pl.BlockSpec((None, None, nk, tk, QK_PAD), lambda b, h, i: (b, h, 0, 0, 0),
                         pipeline_mode=pl.Buffered(1 if nq > 1 else 2)),
            pl.BlockSpec((None, None, nk, HEAD_DIM, tk), lambda b, h, i: (b, h, 0, 0, 0),
                         pipeline_mode=pl.Buffered(1 if nq > 1 else 2)),
            pl.BlockSpec((None, None, None, META_ROWS, QK_PAD), lambda b, h, i: (0, h, 0, 0, 0)),
            pl.BlockSpec((None, None, None, HEAD_DIM, META_ROWS), lambda b, h, i: (0, h, 0, 0, 0)),
        ],
        out_specs=pl.BlockSpec((None, HEAD_DIM, tq), lambda b, h, i: (b, h, i)),
        out_shape=jax.ShapeDtypeStruct((nb, NA_WIDTH, nq * tq), F32),
        scratch_shapes=[pltpu.VMEM((HEAD_DIM, tq), F32), pltpu.VMEM((2 * ATT_GROUP, tk, tq), F32)],
        compiler_params=_cparams("parallel", "parallel", "arbitrary"),
        name="mla_attn",
    )(qt, qt, k, vt, km, vmt)


def _na_bias_body(rpb_ref, shift_ref, o_ref, t_ref, *, rows):
    n_dr = 2 * NA_WIN_ROWS - 1
    shape = (GRID_W, LANES)
    qc = lax.broadcasted_iota(jnp.int32, shape, 0)
    lane = lax.broadcasted_iota(jnp.int32, shape, 1)
    left = lane < GRID_W
    kc = jnp.where(left, lane, lane - GRID_W)
    cs = jnp.clip(qc - NA_WIN_COLS // 2, 0, GRID_W - NA_WIN_COLS)
    col_ok = (kc >= cs) & (kc < cs + NA_WIN_COLS)

    for dr in range(n_dr):
        x = jnp.broadcast_to(rpb_ref[dr:dr + 1, :], shape)
        a = pltpu.roll(x, LANES - (NA_WIN_COLS - 1), 1, stride=1, stride_axis=0)
        b = pltpu.roll(x, GRID_W - (NA_WIN_COLS - 1), 1, stride=1, stride_axis=0)
        t_ref[dr] = jnp.where(col_ok, jnp.where(left, a, b) * LOG2E - shift_ref[0], NEG)

    neg = jnp.full(shape, NEG, F32)
    for c, r0 in enumerate((0, 2 * NA_QROWS, rows - NA_QROWS)):
        for qr in range(NA_QROWS):
            abs_qr = r0 + qr
            rs = min(max(abs_qr - NA_WIN_ROWS // 2, 0), rows - NA_WIN_ROWS)
            for a in range(NA_KROWS // 2):
                halves = []
                for kr in (r0 - NA_QROWS + 2 * a, r0 - NA_QROWS + 2 * a + 1):
                    in_window = rs <= kr < rs + NA_WIN_ROWS
                    halves.append(t_ref[kr - abs_qr + NA_WIN_ROWS - 1] if in_window else neg)
                o_ref[c, qr * GRID_W:(qr + 1) * GRID_W, a * LANES:(a + 1) * LANES] = (
                    jnp.where(left, halves[0], halves[1]))


def _na_bias(rpb, shift, rows):
    n_dr, n_dc = rpb.shape[1:]
    rpb = jnp.pad(rpb, ((0, 0), (0, -n_dr % 8), (0, LANES - n_dc)))
    return pl.pallas_call(
        functools.partial(_na_bias_body, rows=rows),
        grid=(HEADS,),
        in_specs=[pl.BlockSpec((None,) + rpb.shape[1:], lambda h: (h, 0, 0)),
                  pl.BlockSpec(memory_space=pltpu.SMEM)],
        out_specs=pl.BlockSpec((3, None, NA_TQ, NA_KROWS * GRID_W), lambda h: (0, h, 0, 0)),
        out_shape=jax.ShapeDtypeStruct((3, HEADS, NA_TQ, NA_KROWS * GRID_W), F32),
        scratch_shapes=[pltpu.VMEM((2 * NA_WIN_ROWS - 1, GRID_W, LANES), F32)],
        compiler_params=_cparams("parallel"),
        name="na_bias",
    )(rpb, shift.reshape(1))


def _na_body(q_ref, kp_ref, kc_ref, kn_ref, vp_ref, vc_ref, vn_ref, km_ref, vm_ref,
             bias_ref, mb_ref, g_ref, o_ref, out_ref, s_ref):
    for h in range(HEADS):
        sl = slice(h * HEAD_DIM, (h + 1) * HEAD_DIM)
        q = q_ref[:, sl]
        for c, k_ref in enumerate((kp_ref, kc_ref, kn_ref)):
            cols = slice(c * NA_TQ, (c + 1) * NA_TQ)
            s_ref[h, :, cols] = _dot(q, k_ref[sl, :]) + bias_ref[h, :, cols]
        s_ref[h, :, 3 * NA_TQ:] = _dot(q, km_ref[sl, :]) + mb_ref[h]
    def softmax_pv(running_max):
        l_min = None
        for h in range(HEADS):
            sl = slice(h * HEAD_DIM, (h + 1) * HEAD_DIM)
            s = s_ref[h]
            if running_max:
                s = s - jnp.max(s, axis=-1, keepdims=True)
            p = jnp.exp2(s)
            l = jnp.sum(p, axis=-1, keepdims=True)
            pb = p.astype(BF16)
            o = (_dot(pb[:, 3 * NA_TQ:], vm_ref[:, sl])
                 + _dot(pb[:, :NA_TQ], vp_ref[:, sl])
                 + _dot(pb[:, NA_TQ:2 * NA_TQ], vc_ref[:, sl])
                 + _dot(pb[:, 2 * NA_TQ:3 * NA_TQ], vn_ref[:, sl]))
            out_ref[:, sl] = o / l
            l_min = l if l_min is None else jnp.minimum(l_min, l)
        return l_min

    l_min = softmax_pv(False)

    @pl.when(jnp.min(l_min) < SOFTMAX_UNDERFLOW)
    def _():
        softmax_pv(True)

    o_ref[...] = (_rms(out_ref[...]) * g_ref[...]).astype(BF16)


def _na_attn(qa, kat, va, kmt, vm, bias, mb, g, nb):
    rows = qa.shape[0]
    n = rows // nb // NA_TQ
    cur = lambda b, i: (b * n + i, 0)
    prev = lambda b, i: (b * n + jnp.maximum(i - 1, 0), 0)
    nxt = lambda b, i: (b * n + jnp.minimum(i + 1, n - 1), 0)
    blk = lambda f: pl.BlockSpec((NA_TQ, NA_WIDTH), f)
    blk_t = lambda f: pl.BlockSpec((NA_WIDTH, NA_TQ), lambda b, i: f(b, i)[::-1])
    case = lambda b, i: (jnp.where(i == 0, 0, jnp.where(i == n - 1, 2, 1)), 0, 0, 0)
    return pl.pallas_call(
        _na_body,
        grid=(nb, n),
        in_specs=[
            blk(cur), blk_t(prev), blk_t(cur), blk_t(nxt), blk(prev), blk(cur), blk(nxt),
            pl.BlockSpec((NA_WIDTH, META_ROWS), lambda b, i: (0, 0)),
            pl.BlockSpec((META_ROWS, NA_WIDTH), lambda b, i: (0, 0)),
            pl.BlockSpec((None, HEADS, NA_TQ, NA_KROWS * GRID_W), case),
            pl.BlockSpec((HEADS, 1, META_ROWS), lambda b, i: (0, 0, 0)),
            pl.BlockSpec((1, NA_WIDTH), lambda b, i: (0, 0)),
        ],
        out_specs=blk(cur),
        out_shape=jax.ShapeDtypeStruct((rows, NA_WIDTH), BF16),
        scratch_shapes=[pltpu.VMEM((NA_TQ, NA_WIDTH), F32),
                        pltpu.VMEM((HEADS, NA_TQ, NA_KROWS * GRID_W + META_ROWS), F32)],
        compiler_params=_cparams("parallel", "arbitrary"),
        name="na_attn",
    )(qa, kat, kat, kat, va, va, va, kmt, vm, bias, mb, g)


def _na_meta_body(q_ref, kt_ref, v_ref, mb_ref, g_ref, o_ref, out_ref):
    for h in range(HEADS):
        sl = slice(h * HEAD_DIM, (h + 1) * HEAD_DIM)
        s = _dot(q_ref[:, sl], kt_ref[sl, :]) + mb_ref[h]
        m = jnp.max(s, axis=-1, keepdims=True)
        p = jnp.exp2(s - m)
        l = jnp.sum(p, axis=-1, keepdims=True)
        out_ref[:, sl] = _dot(p.astype(BF16), v_ref[:, sl]) / l
    o_ref[...] = (_rms(out_ref[...]) * g_ref[...]).astype(BF16)


def _na_meta(qm, kmt, vm, mb, g):
    full = lambda s: pl.BlockSpec(s, lambda i: (0,) * len(s))
    return pl.pallas_call(
        _na_meta_body,
        grid=(1,),
        in_specs=[full((META_ROWS, NA_WIDTH)), full((NA_WIDTH, META_ROWS)), full((META_ROWS, NA_WIDTH)),
                  full((HEADS, 1, META_ROWS)), full((1, NA_WIDTH))],
        out_specs=full((META_ROWS, NA_WIDTH)),
        out_shape=jax.ShapeDtypeStruct((META_ROWS, NA_WIDTH), BF16),
        scratch_shapes=[pltpu.VMEM((META_ROWS, NA_WIDTH), F32)],
        compiler_params=_cparams("arbitrary"),
        name="na_meta",
    )(qm, kmt, vm, mb, g)


def _outproj_body(x_ref, a_ref, bt_ref, bg_ref, w_ref, fg_ref, h_ref, xn_ref):
    bn = (_rms(bt_ref[...].T) * bg_ref[...]).astype(BF16)
    mix = jnp.concatenate([a_ref[...], bn], axis=-1)
    h = x_ref[...] + _dot(mix, w_ref[...])
    h_ref[...] = h
    xn_ref[...] = (_rms(h) * fg_ref[...]).astype(BF16)


def _outproj(x, a, bt, bg, w, fg, tm, x_map, a_map, rows):
    c = lambda i: (0, 0)
    tpb = bt.shape[2] // tm
    return pl.pallas_call(
        _outproj_body,
        grid=(rows // tm,),
        in_specs=[
            pl.BlockSpec((tm, D_MODEL), x_map),
            pl.BlockSpec((tm, NA_WIDTH), a_map),
            pl.BlockSpec((None, NA_WIDTH, tm), lambda i: (i // tpb, 0, i % tpb)),
            pl.BlockSpec((1, NA_WIDTH), c),
            pl.BlockSpec((2 * NA_WIDTH, D_MODEL), c),
            pl.BlockSpec((1, D_MODEL), c),
        ],
        out_specs=[pl.BlockSpec((tm, D_MODEL), lambda i: (i, 0)),
                   pl.BlockSpec((tm, D_MODEL), lambda i: (i, 0))],
        out_shape=(jax.ShapeDtypeStruct((rows, D_MODEL), F32),
                   jax.ShapeDtypeStruct((rows, D_MODEL), BF16)),
        compiler_params=_cparams("parallel"),
        name="outproj",
    )(x, a, bt, bg, w, fg)


def _ffn_body(xm_ref, xp_ref, xx_ref, xmeta_ref, h_ref, wg_ref, wu_ref, wd_ref, cw_ref, cb_ref,
              o_ref, xe_ref, *, tm, tpb, j_bottom):
    i = pl.program_id(0)
    j = pl.program_id(1)
    half = tm // 2

    @pl.when(j == 0)
    def _():
        first = (i % tpb) == 0
        last = (i % tpb) == tpb - 1
        xe_ref[0:HALO, :] = jnp.where(first, xmeta_ref[...], xp_ref[...])
        xe_ref[HALO:HALO + tm, :] = xm_ref[...]
        xe_ref[HALO + tm:2 * HALO + tm, :] = jnp.where(last, jnp.zeros_like(xx_ref[...]), xx_ref[...])
        o_ref[0:half, :] = h_ref[...]
        o_ref[half:, :] = jnp.zeros((tm - half, o_ref.shape[1]), F32)

    @pl.when(j == j_bottom)
    def _():
        o_ref[half:, :] += h_ref[...]

    g = _dot(xe_ref[...], wg_ref[...])
    u = _dot(xe_ref[HALO:HALO + tm, :], wu_ref[...])
    cw = cw_ref[...]
    gc = (cb_ref[...] + cw[0:1] * g[HALO - 1:HALO - 1 + tm]
          + cw[1:2] * g[HALO:HALO + tm] + cw[2:3] * g[HALO + 1:HALO + 1 + tm])
    act = gc * (1.0 / (1.0 + jnp.exp(-gc))) * u
    o_ref[...] += _dot(act.astype(BF16), wd_ref[...])


def _ffn(xn, xn_meta, h1, wg, wu, wd, cw, cb, nb, tm, tf):
    rows = xn.shape[0]
    nt = rows // tm
    tpb = nt // nb
    hb = tm // HALO
    last_hb = rows // HALO - 1
    n_ff = D_FF // tf
    j_bottom = n_ff // 2
    return pl.pallas_call(
        functools.partial(_ffn_body, tm=tm, tpb=tpb, j_bottom=j_bottom),
        grid=(nt, n_ff),
        in_specs=[
            pl.BlockSpec((tm, D_MODEL), lambda i, j: (i, 0)),
            pl.BlockSpec((HALO, D_MODEL), lambda i, j: (jnp.maximum(i * hb - 1, 0), 0)),
            pl.BlockSpec((HALO, D_MODEL), lambda i, j: (jnp.minimum((i + 1) * hb, last_hb), 0)),
            pl.BlockSpec((HALO, D_MODEL), lambda i, j: ((i // tpb) * (META_ROWS // HALO), 0)),
            pl.BlockSpec((tm // 2, D_MODEL), lambda i, j: (2 * i + jnp.where(j >= j_bottom, 1, 0), 0)),
            pl.BlockSpec((D_MODEL, tf), lambda i, j: (0, j)),
            pl.BlockSpec((D_MODEL, tf), lambda i, j: (0, j)),
            pl.BlockSpec((tf, D_MODEL), lambda i, j: (j, 0)),
            pl.BlockSpec((3, tf), lambda i, j: (0, j)),
            pl.BlockSpec((1, tf), lambda i, j: (0, j)),
        ],
        out_specs=pl.BlockSpec((tm, D_MODEL), lambda i, j: (i, 0)),
        out_shape=jax.ShapeDtypeStruct((rows, D_MODEL), F32),
        scratch_shapes=[pltpu.VMEM((tm + 2 * HALO, D_MODEL), BF16)],
        compiler_params=_cparams("parallel", "arbitrary"),
        name="ffn",
    )(xn, xn, xn, xn_meta, h1, wg, wu, wd, cw, cb)


def _rope_tables(first, n):
    inv = ROPE_THETA ** (-np.arange(0, ROPE_DIM, 2, dtype=np.float64) / ROPE_DIM)
    ang = (first + np.arange(n, dtype=np.float64))[:, None] * inv[None, :]
    cos, sin = np.cos(ang), np.sin(ang)
    z = np.zeros_like(cos)
    tables = (np.concatenate([cos, cos, z, z], axis=-1), np.concatenate([-sin, sin, z, z], axis=-1), cos.T, sin.T)
    return tuple(jnp.asarray(t, F32) for t in tables)


def _softmax_shift(q_gain, k_gain):
    bound = LOG2E * QK_DIM ** 0.5 * jnp.max(jnp.abs(q_gain)) * jnp.max(jnp.abs(k_gain))
    return bound * (1.02 * (1.0 + 2.0 ** -7))


def _na_shift(q_gain, k_gain, rpb, meta_bias):
    qk = LOG2E * HEAD_DIM ** 0.5 * jnp.max(jnp.abs(q_gain)) * jnp.max(jnp.abs(k_gain))
    return 1.02 * qk + LOG2E * jnp.maximum(jnp.max(rpb), jnp.max(meta_bias))


def _pad_lanes(v, width):
    return jnp.pad(v, ((0, 0), (0, width - v.shape[-1])))


def kernel(x, meta_tokens, mix_norm_g, w_in, na_q_g, na_k_g, na_rpb, na_meta_bias, mla_cq_g, mla_ckv_g,
           w_q_up, w_kv_up, mla_q_g, mla_k_g, na_out_g, mla_out_g, w_out, ffn_norm_g, w_gate, w_up,
           conv_w, conv_b, w_down):
    nb, seq, d = x.shape
    assert d == D_MODEL and w_in.shape[0] == 1, "one layer of width 2048"
    rows_grid = seq // GRID_W
    assert seq % max(IN_TM, UP_TM, ATT_TQ, FFN_TM) == 0 and rows_grid >= 4 * NA_QROWS

    w_in_p = _pad_lanes(w_in[0], 4 * IN_TN).astype(BF16)
    w4 = w_in_p.reshape(D_MODEL, 4, IN_TN).transpose(1, 0, 2)
    wqt = w_q_up[0].reshape(Q_RANK, HEADS, QK_DIM).transpose(1, 2, 0).astype(BF16)
    wkv = w_kv_up[0].reshape(KV_RANK, HEADS, 2 * HEAD_DIM)
    wk = wkv[:, :, :HEAD_DIM].reshape(KV_RANK, HEADS // 2, 2 * HEAD_DIM).transpose(1, 0, 2).astype(BF16)
    wvt = wkv[:, :, HEAD_DIM:].transpose(1, 2, 0).astype(BF16)
    wo = w_out[0].astype(BF16)
    wg, wu, wd = w_gate[0].astype(BF16), w_up[0].astype(BF16), w_down[0].astype(BF16)
    row = lambda v: v.reshape(1, -1)
    kg_pad = _pad_lanes(mla_k_g, QK_PAD)
    na_shift = _na_shift(na_q_g, na_k_g, na_rpb, na_meta_bias)
    mb = jnp.pad(na_meta_bias[0] * LOG2E - na_shift, ((0, 0), (0, META_ROWS - N_META)),
                 constant_values=NEG).reshape(HEADS, 1, META_ROWS)

    xr = x.reshape(nb * seq, D_MODEL)
    xm = jnp.pad(meta_tokens.astype(x.dtype), ((0, META_ROWS - N_META), (0, 0)))
    rope_r = _rope_tables(N_META, seq)
    rope_m = _rope_tables(0, META_ROWS)

    inproj = functools.partial(_inproj, g=row(mix_norm_g), w4=w4, qg=row(na_q_g), kg=row(na_k_g),
                               cqg=row(mla_cq_g), ckvg=row(mla_ckv_g))
    qa, ka, va, cq, ckv, kpe = inproj(xr, tm=IN_TM)
    qa_m, ka_m, va_m, cq_m, ckv_m, kpe_m = inproj(xm, tm=META_ROWS)

    up = functools.partial(_mla_up, wqt=wqt, wk=wk, wvt=wvt, qg=mla_q_g, kg=kg_pad,
                           shift=_softmax_shift(mla_q_g, mla_k_g))
    qt, kk, vt = up(cq, ckv, kpe, *rope_r, nb=nb, tm=UP_TM)
    qt_m, kk_m, vt_m = up(cq_m, ckv_m, kpe_m, *rope_m, nb=1, tm=META_ROWS)

    bias = _na_bias(na_rpb[0], na_shift, rows_grid)
    a_n = _na_attn(qa, ka, va, ka_m, va_m, bias, mb, row(na_out_g), nb)
    a_n_m = _na_meta(qa_m, ka_m, va_m, mb, row(na_out_g))

    out_b = _mla_attn(qt, kk, vt, kk_m, vt_m, ATT_TQ, shared_q=False)
    out_b_m = _mla_attn(qt_m, kk, vt, kk_m, vt_m, META_ROWS, shared_q=True)

    op = functools.partial(_outproj, bg=row(mla_out_g), w=wo, fg=row(ffn_norm_g))
    ident = lambda i: (i, 0)
    zero = lambda i: (0, 0)
    h1, xn2 = op(xr, a_n, out_b, tm=OUT_TM, x_map=ident, a_map=ident, rows=nb * seq)
    _, xn2_m = op(xm, a_n_m, out_b_m, tm=META_ROWS, x_map=zero, a_map=zero, rows=nb * META_ROWS)

    out = _ffn(xn2, xn2_m, h1, wg, wu, wd, conv_w[0], row(conv_b), nb, FFN_TM, FFN_TF)
    return out.reshape(nb, seq, D_MODEL)
```

```python
import functools

import jax
import jax.numpy as jnp
import numpy as np
from jax import lax
from jax.experimental import pallas as pl
from jax.experimental.pallas import tpu as pltpu

F32 = jnp.float32
BF16 = jnp.bfloat16

LANES = 128
BF16_SUBLANES = 16
VMEM_LIMIT_BYTES = 56 * 1024 * 1024

D_MODEL = 2048
GRID_W = 64
N_META = 16
EPS = 1e-6
NEG = -1e30
LOG2E = 1.4426950408889634
HEADS = 8
HEAD_DIM = 128
NA_WIDTH = HEADS * HEAD_DIM
NA_WIN_ROWS = 8
NA_WIN_COLS = 16
Q_RANK = 512
KV_RANK = 256
ROPE_DIM = 64
QK_DIM = HEAD_DIM + ROPE_DIM
QK_PAD = 2 * LANES
ROPE_THETA = 10000.0
D_FF = 5632
META_ROWS = LANES
SOFTMAX_UNDERFLOW = 2.0 ** -100

IN_TM = 1024
IN_TN = 1024
UP_TM = 512
ATT_TQ = 1024
ATT_GROUP = 4
NA_QROWS = 4
NA_TQ = NA_QROWS * GRID_W
NA_KROWS = 3 * NA_QROWS
OUT_TM = 256
FFN_TM = 1024
FFN_TF = 512
HALO = BF16_SUBLANES


def _cparams(*sem):
    return pltpu.CompilerParams(dimension_semantics=sem, vmem_limit_bytes=VMEM_LIMIT_BYTES)


def _rms(v):
    return v * lax.rsqrt(jnp.mean(v * v, axis=-1, keepdims=True) + EPS)


def _dot(a, b):
    return jnp.dot(a, b, preferred_element_type=F32)


def _inproj_body(x_ref, g_ref, w_ref, qg_ref, kg_ref, cqg_ref, ckvg_ref,
                 qa_ref, ka_ref, va_ref, cq_ref, ckv_ref, kpe_ref, xn_ref, *, q_scale):
    j = pl.program_id(1)

    @pl.when(j == 0)
    def _():
        xn_ref[...] = (_rms(x_ref[...]) * g_ref[...]).astype(BF16)

    def project(lo, hi):
        return _dot(xn_ref[...], w_ref[:, lo:hi])

    def head_norm(gain_ref, out_ref, post, transposed):
        for t in range(HEADS // 2):
            y = project(2 * t * HEAD_DIM, 2 * (t + 1) * HEAD_DIM)
            for u in range(2):
                sl = slice((2 * t + u) * HEAD_DIM, (2 * t + u + 1) * HEAD_DIM)
                yh = _rms(y[:, u * HEAD_DIM:(u + 1) * HEAD_DIM]) * gain_ref[...] * post
                if transposed:
                    out_ref[sl, :] = yh.T.astype(BF16)
                else:
                    out_ref[:, sl] = yh.astype(BF16)

    @pl.when(j == 0)
    def _():
        head_norm(qg_ref, qa_ref, q_scale, False)

    @pl.when(j == 1)
    def _():
        head_norm(kg_ref, ka_ref, 1.0, True)

    @pl.when(j == 2)
    def _():
        va_ref[...] = project(0, IN_TN).astype(BF16)

    @pl.when(j == 3)
    def _():
        cq_ref[...] = (_rms(project(0, Q_RANK)) * cqg_ref[...]).astype(BF16)
        y = project(Q_RANK, IN_TN)
        ckv_ref[...] = (_rms(y[:, :KV_RANK]) * ckvg_ref[...]).astype(BF16)
        kpe_ref[...] = y[:, KV_RANK:KV_RANK + LANES]


def _inproj(x, g, w4, qg, kg, cqg, ckvg, tm):
    rows = x.shape[0]
    row = lambda i, j: (i, 0)
    const = lambda i, j: (0, 0)
    outs = (
        jax.ShapeDtypeStruct((rows, NA_WIDTH), BF16),
        jax.ShapeDtypeStruct((NA_WIDTH, rows), BF16),
        jax.ShapeDtypeStruct((rows, NA_WIDTH), BF16),
        jax.ShapeDtypeStruct((rows, Q_RANK), BF16),
        jax.ShapeDtypeStruct((rows, KV_RANK), BF16),
        jax.ShapeDtypeStruct((rows, LANES), F32),
    )
    return pl.pallas_call(
        functools.partial(_inproj_body, q_scale=HEAD_DIM ** -0.5 * LOG2E),
        grid=(rows // tm, 4),
        in_specs=[
            pl.BlockSpec((tm, D_MODEL), row),
            pl.BlockSpec((1, D_MODEL), const),
            pl.BlockSpec((None, D_MODEL, IN_TN), lambda i, j: (j, 0, 0)),
            pl.BlockSpec((1, HEAD_DIM), const),
            pl.BlockSpec((1, HEAD_DIM), const),
            pl.BlockSpec((1, Q_RANK), const),
            pl.BlockSpec((1, KV_RANK), const),
        ],
        out_specs=[
            pl.BlockSpec((tm, NA_WIDTH), row),
            pl.BlockSpec((NA_WIDTH, tm), lambda i, j: (0, i)),
            pl.BlockSpec((tm, NA_WIDTH), row),
            pl.BlockSpec((tm, Q_RANK), row),
            pl.BlockSpec((tm, KV_RANK), row),
            pl.BlockSpec((tm, LANES), row),
        ],
        out_shape=outs,
        scratch_shapes=[pltpu.VMEM((tm, D_MODEL), BF16)],
        compiler_params=_cparams("parallel", "arbitrary"),
        name="inproj",
    )(x, g, w4, qg, kg, cqg, ckvg)


def _mla_up_body(cq_ref, ckv_ref, kpe_ref, cos_ref, sin_ref, cost_ref, sint_ref,
                 wqt_ref, wk_ref, wvt_ref, qgt_ref, qpad_ref, kg_ref, qt_ref, k_ref, vt_ref, *, q_scale):
    half = ROPE_DIM // 2
    ckv = ckv_ref[...]
    cqt = cq_ref[...].astype(F32).T.astype(BF16)
    ckvt = ckv.astype(F32).T.astype(BF16)
    qgt = qgt_ref[...]
    cost = cost_ref[...]
    sint = sint_ref[...]
    cosf = cos_ref[...]
    sinf = sin_ref[...]
    kg = kg_ref[...]

    kpe = kpe_ref[...]
    kpe_ss = jnp.sum(kpe * kpe, axis=-1, keepdims=True)
    r = kpe * kg[:, LANES:]
    kpe_rot = r * cosf + (pltpu.roll(r, half, 1) + pltpu.roll(r, LANES - half, 1)) * sinf

    for h in range(HEADS):
        q = _dot(wqt_ref[h], cqt)
        ms = jnp.sum(q * q, axis=0, keepdims=True) * (1.0 / QK_DIM)
        qn = q * (lax.rsqrt(ms + EPS) * q_scale) * qgt
        x1 = qn[LANES:LANES + half]
        x2 = qn[LANES + half:LANES + 2 * half]
        qt_ref[h, 0:LANES, :] = qn[:LANES].astype(BF16)
        qt_ref[h, LANES:LANES + half, :] = (x1 * cost - x2 * sint).astype(BF16)
        qt_ref[h, LANES + half:LANES + 2 * half, :] = (x2 * cost + x1 * sint).astype(BF16)
        qt_ref[h, LANES + 2 * half:, :] = qpad_ref[...]
        vt_ref[h] = _dot(wvt_ref[h], ckvt).astype(BF16)

    one_lane = (lax.broadcasted_iota(jnp.int32, (1, LANES), 1) == ROPE_DIM).astype(F32)
    for t in range(HEADS // 2):
        kn2 = _dot(ckv, wk_ref[t])
        for u in range(2):
            kn = kn2[:, u * LANES:(u + 1) * LANES]
            ms = (jnp.sum(kn * kn, axis=-1, keepdims=True) + kpe_ss) * (1.0 / QK_DIM)
            rn = lax.rsqrt(ms + EPS)
            k_ref[2 * t + u] = jnp.concatenate([kn * rn * kg[:, :LANES], kpe_rot * rn + one_lane],
                                               axis=-1).astype(BF16)


def _mla_up(cq, ckv, kpe, cosf, sinf, cost, sint, wqt, wk, wvt, qg, kg, shift, nb, tm):
    rows = cq.shape[0]
    nt = rows // nb // tm
    row = lambda b, i: (b * nt + i, 0)
    pos = lambda b, i: (i, 0)
    post = lambda b, i: (0, i)
    c2 = lambda b, i: (0, 0)
    c3 = lambda b, i: (0, 0, 0)
    qgt = jnp.broadcast_to(qg.reshape(QK_DIM, 1), (QK_DIM, tm))
    qpad = jnp.zeros((QK_PAD - QK_DIM, tm), F32).at[0].set(-shift).astype(BF16)
    outs = (
        jax.ShapeDtypeStruct((nb, HEADS, QK_PAD, nt * tm), BF16),
        jax.ShapeDtypeStruct((nb, HEADS, nt, tm, QK_PAD), BF16),
        jax.ShapeDtypeStruct((nb, HEADS, nt, HEAD_DIM, tm), BF16),
    )
    return pl.pallas_call(
        functools.partial(_mla_up_body, q_scale=QK_DIM ** -0.5 * LOG2E),
        grid=(nb, nt),
        in_specs=[
            pl.BlockSpec((tm, Q_RANK), row),
            pl.BlockSpec((tm, KV_RANK), row),
            pl.BlockSpec((tm, LANES), row),
            pl.BlockSpec((tm, LANES), pos),
            pl.BlockSpec((tm, LANES), pos),
            pl.BlockSpec((ROPE_DIM // 2, tm), post),
            pl.BlockSpec((ROPE_DIM // 2, tm), post),
            pl.BlockSpec((HEADS, QK_DIM, Q_RANK), c3),
            pl.BlockSpec((HEADS // 2, KV_RANK, 2 * HEAD_DIM), c3),
            pl.BlockSpec((HEADS, HEAD_DIM, KV_RANK), c3),
            pl.BlockSpec((QK_DIM, tm), c2),
            pl.BlockSpec((QK_PAD - QK_DIM, tm), c2),
            pl.BlockSpec((1, QK_PAD), c2),
        ],
        out_specs=[
            pl.BlockSpec((None, HEADS, QK_PAD, tm), lambda b, i: (b, 0, 0, i)),
            pl.BlockSpec((None, HEADS, None, tm, QK_PAD), lambda b, i: (b, 0, i, 0, 0)),
            pl.BlockSpec((None, HEADS, None, HEAD_DIM, tm), lambda b, i: (b, 0, i, 0, 0)),
        ],
        out_shape=outs,
        compiler_params=_cparams("parallel", "parallel"),
        name="mla_up",
    )(cq, ckv, kpe, cosf, sinf, cost, sint, wqt, wk, wvt, qgt, qpad, kg)


def _mla_attn_body(qt_ref, qtn_ref, k_ref, vt_ref, km_ref, vmt_ref, o_ref, acc_ref, p_ref, lc_ref,
                   *, nk, carry_over):
    qt = qt_ref[...]
    ng = nk // ATT_GROUP

    def probs(g, half, q=None):
        q = qt if q is None else q
        ls = None
        for u in range(ATT_GROUP):
            p = jnp.exp2(_dot(k_ref[g * ATT_GROUP + u], q))
            s1 = jnp.sum(p, axis=0, keepdims=True)
            ls = s1 if ls is None else ls + s1
            p_ref[half * ATT_GROUP + u] = p.astype(BF16)
        return ls

    def values(g, half):
        pv = None
        for u in range(ATT_GROUP):
            d = _dot(vt_ref[g * ATT_GROUP + u], p_ref[half * ATT_GROUP + u])
            pv = d if pv is None else pv + d
        acc_ref[...] += pv

    def meta_scores():
        s = _dot(km_ref[...], qt)
        key = lax.broadcasted_iota(jnp.int32, s.shape, 0)
        return jnp.where(key < N_META, s, NEG)

    def two_stages(i, l):
        l = l + probs(2 * i + 1, 1)
        values(2 * i, 0)
        l = l + probs(2 * i + 2, 0)
        values(2 * i + 1, 1)
        return l

    if carry_over:
        @pl.when(pl.program_id(2) == 0)
        def _():
            lc_ref[...] = probs(0, 0)
        l = lc_ref[...]
    else:
        l = probs(0, 0)

    p = jnp.exp2(meta_scores())
    acc_ref[...] = _dot(vmt_ref[...], p.astype(BF16))
    l = lax.fori_loop(0, ng // 2 - 1, two_stages, l + jnp.sum(p, axis=0, keepdims=True))
    l = l + probs(ng - 1, 1)
    values(ng - 2, 0)
    if carry_over:
        lc_ref[...] = probs(0, 0, qtn_ref[...])
    values(ng - 1, 1)
    o_ref[...] = acc_ref[...] / l

    @pl.when(jnp.min(l) < SOFTMAX_UNDERFLOW)
    def _():
        s = meta_scores()
        m0 = jnp.max(s, axis=0, keepdims=True)
        p = jnp.exp2(s - m0)
        acc_ref[...] = _dot(vmt_ref[...], p.astype(BF16))

        def chunk(c, carry):
            m_prev, l_prev = carry
            s = _dot(k_ref[c], qt)
            m_new = jnp.maximum(m_prev, jnp.max(s, axis=0, keepdims=True))
            alpha = jnp.exp2(m_prev - m_new)
            p = jnp.exp2(s - m_new)
            acc_ref[...] = alpha * acc_ref[...] + _dot(vt_ref[c], p.astype(BF16))
            return m_new, alpha * l_prev + jnp.sum(p, axis=0, keepdims=True)

        _, l_exact = lax.fori_loop(0, nk, chunk, (m0, jnp.sum(p, axis=0, keepdims=True)))
        o_ref[...] = acc_ref[...] / l_exact


def _mla_attn(qt, k, vt, km, vmt, tq, shared_q):
    nb, _, nk, tk, _ = k.shape
    nq = qt.shape[3] // tq
    qb = (lambda b: 0) if shared_q else (lambda b: b)
    return pl.pallas_call(
        functools.partial(_mla_attn_body, nk=nk, carry_over=nq > 1),
        grid=(nb, HEADS, nq),
        in_specs=[
            pl.BlockSpec((None, None, QK_PAD, tq), lambda b, h, i: (qb(b), h, 0, i)),
            pl.BlockSpec((None, None, QK_PAD, tq), lambda b, h, i: (qb(b), h, 0, jnp.minimum(i + 1, nq - 1))),
            pl.BlockSpec((None, None, nk, tk, QK_PAD), lambda b, h, i: (b, h, 0, 0, 0),
                         pipeline_mode=pl.Buffered(1 if nq > 1 else 2)),
            pl.BlockSpec((None, None, nk, HEAD_DIM, tk), lambda b, h, i: (b, h, 0, 0, 0),
                         pipeline_mode=pl.Buffered(1 if nq > 1 else 2)),
            pl.BlockSpec((None, None, None, META_ROWS, QK_PAD), lambda b, h, i: (0, h, 0, 0, 0)),
            pl.BlockSpec((None, None, None, HEAD_DIM, META_ROWS), lambda b, h, i: (0, h, 0, 0, 0)),
        ],
        out_specs=pl.BlockSpec((None, HEAD_DIM, tq), lambda b, h, i: (b, h, i)),
        out_shape=jax.ShapeDtypeStruct((nb, NA_WIDTH, nq * tq), F32),
        scratch_shapes=[pltpu.VMEM((HEAD_DIM, tq), F32), pltpu.VMEM((2 * ATT_GROUP, tk, tq), BF16),
                        pltpu.VMEM((1, tq), F32)],
        compiler_params=_cparams("parallel", "parallel", "arbitrary"),
        name="mla_attn",
    )(qt, qt, k, vt, km, vmt)


def _na_bias_body(rpb_ref, shift_ref, o_ref, t_ref, *, rows):
    n_dr = 2 * NA_WIN_ROWS - 1
    shape = (GRID_W, LANES)
    qc = lax.broadcasted_iota(jnp.int32, shape, 0)
    lane = lax.broadcasted_iota(jnp.int32, shape, 1)
    left = lane < GRID_W
    kc = jnp.where(left, lane, lane - GRID_W)
    cs = jnp.clip(qc - NA_WIN_COLS // 2, 0, GRID_W - NA_WIN_COLS)
    col_ok = (kc >= cs) & (kc < cs + NA_WIN_COLS)

    for dr in range(n_dr):
        x = jnp.broadcast_to(rpb_ref[dr:dr + 1, :], shape)
        a = pltpu.roll(x, LANES - (NA_WIN_COLS - 1), 1, stride=1, stride_axis=0)
        b = pltpu.roll(x, GRID_W - (NA_WIN_COLS - 1), 1, stride=1, stride_axis=0)
        t_ref[dr] = jnp.where(col_ok, jnp.where(left, a, b) * LOG2E - shift_ref[0], NEG)

    neg = jnp.full(shape, NEG, F32)
    for c, r0 in enumerate((0, 2 * NA_QROWS, rows - NA_QROWS)):
        for qr in range(NA_QROWS):
            abs_qr = r0 + qr
            rs = min(max(abs_qr - NA_WIN_ROWS // 2, 0), rows - NA_WIN_ROWS)
            for a in range(NA_KROWS // 2):
                halves = []
                for kr in (r0 - NA_QROWS + 2 * a, r0 - NA_QROWS + 2 * a + 1):
                    in_window = rs <= kr < rs + NA_WIN_ROWS
                    halves.append(t_ref[kr - abs_qr + NA_WIN_ROWS - 1] if in_window else neg)
                o_ref[c, qr * GRID_W:(qr + 1) * GRID_W, a * LANES:(a + 1) * LANES] = (
                    jnp.where(left, halves[0], halves[1]))


def _na_bias(rpb, shift, rows):
    n_dr, n_dc = rpb.shape[1:]
    rpb = jnp.pad(rpb, ((0, 0), (0, -n_dr % 8), (0, LANES - n_dc)))
    return pl.pallas_call(
        functools.partial(_na_bias_body, rows=rows),
        grid=(HEADS,),
        in_specs=[pl.BlockSpec((None,) + rpb.shape[1:], lambda h: (h, 0, 0)),
                  pl.BlockSpec(memory_space=pltpu.SMEM)],
        out_specs=pl.BlockSpec((3, None, NA_TQ, NA_KROWS * GRID_W), lambda h: (0, h, 0, 0)),
        out_shape=jax.ShapeDtypeStruct((3, HEADS, NA_TQ, NA_KROWS * GRID_W), F32),
        scratch_shapes=[pltpu.VMEM((2 * NA_WIN_ROWS - 1, GRID_W, LANES), F32)],
        compiler_params=_cparams("parallel"),
        name="na_bias",
    )(rpb, shift.reshape(1))


def _na_body(q_ref, kp_ref, kc_ref, kn_ref, vp_ref, vc_ref, vn_ref, km_ref, vm_ref,
             bias_ref, mb_ref, g_ref, o_ref, out_ref, s_ref):
    for h in range(HEADS):
        sl = slice(h * HEAD_DIM, (h + 1) * HEAD_DIM)
        q = q_ref[:, sl]
        for c, k_ref in enumerate((kp_ref, kc_ref, kn_ref)):
            cols = slice(c * NA_TQ, (c + 1) * NA_TQ)
            s_ref[h, :, cols] = _dot(q, k_ref[sl, :]) + bias_ref[h, :, cols]
        s_ref[h, :, 3 * NA_TQ:] = _dot(q, km_ref[sl, :]) + mb_ref[h]
    def softmax_pv(running_max):
        l_min = None
        for h in range(HEADS):
            sl = slice(h * HEAD_DIM, (h + 1) * HEAD_DIM)
            s = s_ref[h]
            if running_max:
                s = s - jnp.max(s, axis=-1, keepdims=True)
            p = jnp.exp2(s)
            l = jnp.sum(p, axis=-1, keepdims=True)
            pb = p.astype(BF16)
            o = (_dot(pb[:, 3 * NA_TQ:], vm_ref[:, sl])
                 + _dot(pb[:, :NA_TQ], vp_ref[:, sl])
                 + _dot(pb[:, NA_TQ:2 * NA_TQ], vc_ref[:, sl])
                 + _dot(pb[:, 2 * NA_TQ:3 * NA_TQ], vn_ref[:, sl]))
            out_ref[:, sl] = o / l
            l_min = l if l_min is None else jnp.minimum(l_min, l)
        return l_min

    l_min = softmax_pv(False)

    @pl.when(jnp.min(l_min) < SOFTMAX_UNDERFLOW)
    def _():
        softmax_pv(True)

    o_ref[...] = (_rms(out_ref[...]) * g_ref[...]).astype(BF16)


def _na_attn(qa, kat, va, kmt, vm, bias, mb, g, nb):
    rows = qa.shape[0]
    n = rows // nb // NA_TQ
    cur = lambda b, i: (b * n + i, 0)
    prev = lambda b, i: (b * n + jnp.maximum(i - 1, 0), 0)
    nxt = lambda b, i: (b * n + jnp.minimum(i + 1, n - 1), 0)
    blk = lambda f: pl.BlockSpec((NA_TQ, NA_WIDTH), f)
    blk_t = lambda f: pl.BlockSpec((NA_WIDTH, NA_TQ), lambda b, i: f(b, i)[::-1])
    case = lambda b, i: (jnp.where(i == 0, 0, jnp.where(i == n - 1, 2, 1)), 0, 0, 0)
    return pl.pallas_call(
        _na_body,
        grid=(nb, n),
        in_specs=[
            blk(cur), blk_t(prev), blk_t(cur), blk_t(nxt), blk(prev), blk(cur), blk(nxt),
            pl.BlockSpec((NA_WIDTH, META_ROWS), lambda b, i: (0, 0)),
            pl.BlockSpec((META_ROWS, NA_WIDTH), lambda b, i: (0, 0)),
            pl.BlockSpec((None, HEADS, NA_TQ, NA_KROWS * GRID_W), case),
            pl.BlockSpec((HEADS, 1, META_ROWS), lambda b, i: (0, 0, 0)),
            pl.BlockSpec((1, NA_WIDTH), lambda b, i: (0, 0)),
        ],
        out_specs=blk(cur),
        out_shape=jax.ShapeDtypeStruct((rows, NA_WIDTH), BF16),
        scratch_shapes=[pltpu.VMEM((NA_TQ, NA_WIDTH), F32),
                        pltpu.VMEM((HEADS, NA_TQ, NA_KROWS * GRID_W + META_ROWS), F32)],
        compiler_params=_cparams("parallel", "arbitrary"),
        name="na_attn",
    )(qa, kat, kat, kat, va, va, va, kmt, vm, bias, mb, g)


def _na_meta_body(q_ref, kt_ref, v_ref, mb_ref, g_ref, o_ref, out_ref):
    for h in range(HEADS):
        sl = slice(h * HEAD_DIM, (h + 1) * HEAD_DIM)
        s = _dot(q_ref[:, sl], kt_ref[sl, :]) + mb_ref[h]
        m = jnp.max(s, axis=-1, keepdims=True)
        p = jnp.exp2(s - m)
        l = jnp.sum(p, axis=-1, keepdims=True)
        out_ref[:, sl] = _dot(p.astype(BF16), v_ref[:, sl]) / l
    o_ref[...] = (_rms(out_ref[...]) * g_ref[...]).astype(BF16)


def _na_meta(qm, kmt, vm, mb, g):
    full = lambda s: pl.BlockSpec(s, lambda i: (0,) * len(s))
    return pl.pallas_call(
        _na_meta_body,
        grid=(1,),
        in_specs=[full((META_ROWS, NA_WIDTH)), full((NA_WIDTH, META_ROWS)), full((META_ROWS, NA_WIDTH)),
                  full((HEADS, 1, META_ROWS)), full((1, NA_WIDTH))],
        out_specs=full((META_ROWS, NA_WIDTH)),
        out_shape=jax.ShapeDtypeStruct((META_ROWS, NA_WIDTH), BF16),
        scratch_shapes=[pltpu.VMEM((META_ROWS, NA_WIDTH), F32)],
        compiler_params=_cparams("arbitrary"),
        name="na_meta",
    )(qm, kmt, vm, mb, g)


def _outproj_body(x_ref, a_ref, bt_ref, bg_ref, w_ref, fg_ref, h_ref, xn_ref):
    bn = (_rms(bt_ref[...].T) * bg_ref[...]).astype(BF16)
    mix = jnp.concatenate([a_ref[...], bn], axis=-1)
    h = x_ref[...] + _dot(mix, w_ref[...])
    h_ref[...] = h
    xn_ref[...] = (_rms(h) * fg_ref[...]).astype(BF16)


def _outproj(x, a, bt, bg, w, fg, tm, x_map, a_map, rows):
    c = lambda i: (0, 0)
    tpb = bt.shape[2] // tm
    return pl.pallas_call(
        _outproj_body,
        grid=(rows // tm,),
        in_specs=[
            pl.BlockSpec((tm, D_MODEL), x_map),
            pl.BlockSpec((tm, NA_WIDTH), a_map),
            pl.BlockSpec((None, NA_WIDTH, tm), lambda i: (i // tpb, 0, i % tpb)),
            pl.BlockSpec((1, NA_WIDTH), c),
            pl.BlockSpec((2 * NA_WIDTH, D_MODEL), c),
            pl.BlockSpec((1, D_MODEL), c),
        ],
        out_specs=[pl.BlockSpec((tm, D_MODEL), lambda i: (i, 0)),
                   pl.BlockSpec((tm, D_MODEL), lambda i: (i, 0))],
        out_shape=(jax.ShapeDtypeStruct((rows, D_MODEL), F32),
                   jax.ShapeDtypeStruct((rows, D_MODEL), BF16)),
        compiler_params=_cparams("parallel"),
        name="outproj",
    )(x, a, bt, bg, w, fg)


def _ffn_body(xm_ref, xp_ref, xx_ref, xmeta_ref, h_ref, wg_ref, wu_ref, wd_ref, cw_ref, cb_ref,
              o_ref, xe_ref, *, tm, tpb, j_bottom):
    i = pl.program_id(0)
    j = pl.program_id(1)
    half = tm // 2

    @pl.when(j == 0)
    def _():
        first = (i % tpb) == 0
        last = (i % tpb) == tpb - 1
        xe_ref[0:HALO, :] = jnp.where(first, xmeta_ref[...], xp_ref[...])
        xe_ref[HALO:HALO + tm, :] = xm_ref[...]
        xe_ref[HALO + tm:2 * HALO + tm, :] = jnp.where(last, jnp.zeros_like(xx_ref[...]), xx_ref[...])
        o_ref[0:half, :] = h_ref[...]
        o_ref[half:, :] = jnp.zeros((tm - half, o_ref.shape[1]), F32)

    @pl.when(j == j_bottom)
    def _():
        o_ref[half:, :] += h_ref[...]

    g = _dot(xe_ref[...], wg_ref[...])
    u = _dot(xe_ref[HALO:HALO + tm, :], wu_ref[...])
    cw = cw_ref[...]
    gc = (cb_ref[...] + cw[0:1] * g[HALO - 1:HALO - 1 + tm]
          + cw[1:2] * g[HALO:HALO + tm] + cw[2:3] * g[HALO + 1:HALO + 1 + tm])
    act = gc * (1.0 / (1.0 + jnp.exp(-gc))) * u
    o_ref[...] += _dot(act.astype(BF16), wd_ref[...])


def _ffn(xn, xn_meta, h1, wg, wu, wd, cw, cb, nb, tm, tf):
    rows = xn.shape[0]
    nt = rows // tm
    tpb = nt // nb
    hb = tm // HALO
    last_hb = rows // HALO - 1
    n_ff = D_FF // tf
    j_bottom = n_ff // 2
    return pl.pallas_call(
        functools.partial(_ffn_body, tm=tm, tpb=tpb, j_bottom=j_bottom),
        grid=(nt, n_ff),
        in_specs=[
            pl.BlockSpec((tm, D_MODEL), lambda i, j: (i, 0)),
            pl.BlockSpec((HALO, D_MODEL), lambda i, j: (jnp.maximum(i * hb - 1, 0), 0)),
            pl.BlockSpec((HALO, D_MODEL), lambda i, j: (jnp.minimum((i + 1) * hb, last_hb), 0)),
            pl.BlockSpec((HALO, D_MODEL), lambda i, j: ((i // tpb) * (META_ROWS // HALO), 0)),
            pl.BlockSpec((tm // 2, D_MODEL), lambda i, j: (2 * i + jnp.where(j >= j_bottom, 1, 0), 0)),
            pl.BlockSpec((D_MODEL, tf), lambda i, j: (0, j)),
            pl.BlockSpec((D_MODEL, tf), lambda i, j: (0, j)),
            pl.BlockSpec((tf, D_MODEL), lambda i, j: (j, 0)),
            pl.BlockSpec((3, tf), lambda i, j: (0, j)),
            pl.BlockSpec((1, tf), lambda i, j: (0, j)),
        ],
        out_specs=pl.BlockSpec((tm, D_MODEL), lambda i, j: (i, 0)),
        out_shape=jax.ShapeDtypeStruct((rows, D_MODEL), F32),
        scratch_shapes=[pltpu.VMEM((tm + 2 * HALO, D_MODEL), BF16)],
        compiler_params=_cparams("parallel", "arbitrary"),
        name="ffn",
    )(xn, xn, xn, xn_meta, h1, wg, wu, wd, cw, cb)


def _rope_tables(first, n):
    inv = ROPE_THETA ** (-np.arange(0, ROPE_DIM, 2, dtype=np.float64) / ROPE_DIM)
    ang = (first + np.arange(n, dtype=np.float64))[:, None] * inv[None, :]
    cos, sin = np.cos(ang), np.sin(ang)
    z = np.zeros_like(cos)
    tables = (np.concatenate([cos, cos, z, z], axis=-1), np.concatenate([-sin, sin, z, z], axis=-1), cos.T, sin.T)
    return tuple(jnp.asarray(t, F32) for t in tables)


def _softmax_shift(q_gain, k_gain):
    bound = LOG2E * QK_DIM ** 0.5 * jnp.max(jnp.abs(q_gain)) * jnp.max(jnp.abs(k_gain))
    return bound * (1.02 * (1.0 + 2.0 ** -7))


def _na_shift(q_gain, k_gain, rpb, meta_bias):
    qk = LOG2E * HEAD_DIM ** 0.5 * jnp.max(jnp.abs(q_gain)) * jnp.max(jnp.abs(k_gain))
    return 1.02 * qk + LOG2E * jnp.maximum(jnp.max(rpb), jnp.max(meta_bias))


def _pad_lanes(v, width):
    return jnp.pad(v, ((0, 0), (0, width - v.shape[-1])))


def kernel(x, meta_tokens, mix_norm_g, w_in, na_q_g, na_k_g, na_rpb, na_meta_bias, mla_cq_g, mla_ckv_g,
           w_q_up, w_kv_up, mla_q_g, mla_k_g, na_out_g, mla_out_g, w_out, ffn_norm_g, w_gate, w_up,
           conv_w, conv_b, w_down):
    nb, seq, d = x.shape
    assert d == D_MODEL and w_in.shape[0] == 1, "one layer of width 2048"
    rows_grid = seq // GRID_W
    assert seq % max(IN_TM, UP_TM, ATT_TQ, FFN_TM) == 0 and rows_grid >= 4 * NA_QROWS

    w_in_p = _pad_lanes(w_in[0], 4 * IN_TN).astype(BF16)
    w4 = w_in_p.reshape(D_MODEL, 4, IN_TN).transpose(1, 0, 2)
    wqt = w_q_up[0].reshape(Q_RANK, HEADS, QK_DIM).transpose(1, 2, 0).astype(BF16)
    wkv = w_kv_up[0].reshape(KV_RANK, HEADS, 2 * HEAD_DIM)
    wk = wkv[:, :, :HEAD_DIM].reshape(KV_RANK, HEADS // 2, 2 * HEAD_DIM).transpose(1, 0, 2).astype(BF16)
    wvt = wkv[:, :, HEAD_DIM:].transpose(1, 2, 0).astype(BF16)
    wo = w_out[0].astype(BF16)
    wg, wu, wd = w_gate[0].astype(BF16), w_up[0].astype(BF16), w_down[0].astype(BF16)
    row = lambda v: v.reshape(1, -1)
    kg_pad = _pad_lanes(mla_k_g, QK_PAD)
    na_shift = _na_shift(na_q_g, na_k_g, na_rpb, na_meta_bias)
    mb = jnp.pad(na_meta_bias[0] * LOG2E - na_shift, ((0, 0), (0, META_ROWS - N_META)),
                 constant_values=NEG).reshape(HEADS, 1, META_ROWS)

    xr = x.reshape(nb * seq, D_MODEL)
    xm = jnp.pad(meta_tokens.astype(x.dtype), ((0, META_ROWS - N_META), (0, 0)))
    rope_r = _rope_tables(N_META, seq)
    rope_m = _rope_tables(0, META_ROWS)

    inproj = functools.partial(_inproj, g=row(mix_norm_g), w4=w4, qg=row(na_q_g), kg=row(na_k_g),
                               cqg=row(mla_cq_g), ckvg=row(mla_ckv_g))
    qa, ka, va, cq, ckv, kpe = inproj(xr, tm=IN_TM)
    qa_m, ka_m, va_m, cq_m, ckv_m, kpe_m = inproj(xm, tm=META_ROWS)

    up = functools.partial(_mla_up, wqt=wqt, wk=wk, wvt=wvt, qg=mla_q_g, kg=kg_pad,
                           shift=_softmax_shift(mla_q_g, mla_k_g))
    qt, kk, vt = up(cq, ckv, kpe, *rope_r, nb=nb, tm=UP_TM)
    qt_m, kk_m, vt_m = up(cq_m, ckv_m, kpe_m, *rope_m, nb=1, tm=META_ROWS)

    bias = _na_bias(na_rpb[0], na_shift, rows_grid)
    a_n = _na_attn(qa, ka, va, ka_m, va_m, bias, mb, row(na_out_g), nb)
    a_n_m = _na_meta(qa_m, ka_m, va_m, mb, row(na_out_g))

    out_b = _mla_attn(qt, kk, vt, kk_m, vt_m, ATT_TQ, shared_q=False)
    out_b_m = _mla_attn(qt_m, kk, vt, kk_m, vt_m, META_ROWS, shared_q=True)

    op = functools.partial(_outproj, bg=row(mla_out_g), w=wo, fg=row(ffn_norm_g))
    ident = lambda i: (i, 0)
    zero = lambda i: (0, 0)
    h1, xn2 = op(xr, a_n, out_b, tm=OUT_TM, x_map=ident, a_map=ident, rows=nb * seq)
    _, xn2_m = op(xm, a_n_m, out_b_m, tm=META_ROWS, x_map=zero, a_map=zero, rows=nb * META_ROWS)

    out = _ffn(xn2, xn2_m, h1, wg, wu, wd, conv_w[0], row(conv_b), nb, FFN_TM, FFN_TF)
    return out.reshape(nb, seq, D_MODEL)
```

```python
import functools

import jax
import jax.numpy as jnp
import numpy as np
from jax import lax
from jax.experimental import pallas as pl
from jax.experimental.pallas import tpu as pltpu

F32 = jnp.float32
BF16 = jnp.bfloat16

LANES = 128
BF16_SUBLANES = 16
VMEM_LIMIT_BYTES = 56 * 1024 * 1024

D_MODEL = 2048
GRID_W = 64
N_META = 16
EPS = 1e-6
NEG = -1e30
LOG2E = 1.4426950408889634
HEADS = 8
HEAD_DIM = 128
NA_WIDTH = HEADS * HEAD_DIM
NA_WIN_ROWS = 8
NA_WIN_COLS = 16
Q_RANK = 512
KV_RANK = 256
ROPE_DIM = 64
QK_DIM = HEAD_DIM + ROPE_DIM
QK_PAD = 2 * LANES
ROPE_THETA = 10000.0
D_FF = 5632
META_ROWS = LANES
SOFTMAX_UNDERFLOW = 2.0 ** -100

IN_TM = 1024
IN_TN = 1024
UP_TM = 512
ATT_TQ = 1024
ATT_GROUP = 8
NA_QROWS = 4
NA_TQ = NA_QROWS * GRID_W
NA_KROWS = 3 * NA_QROWS
OUT_TM = 256
FFN_TM = 1024
FFN_TF = 512
HALO = BF16_SUBLANES


def _cparams(*sem):
    return pltpu.CompilerParams(dimension_semantics=sem, vmem_limit_bytes=VMEM_LIMIT_BYTES)


def _rms(v):
    return v * lax.rsqrt(jnp.mean(v * v, axis=-1, keepdims=True) + EPS)


def _dot(a, b):
    return jnp.dot(a, b, preferred_element_type=F32)


def _inproj_body(x_ref, g_ref, w_ref, qg_ref, kg_ref, cqg_ref, ckvg_ref,
                 qa_ref, ka_ref, va_ref, cq_ref, ckv_ref, kpe_ref, xn_ref, *, q_scale):
    j = pl.program_id(1)

    @pl.when(j == 0)
    def _():
        xn_ref[...] = (_rms(x_ref[...]) * g_ref[...]).astype(BF16)

    def project(lo, hi):
        return _dot(xn_ref[...], w_ref[:, lo:hi])

    def head_norm(gain_ref, out_ref, post, transposed):
        for t in range(HEADS // 2):
            y = project(2 * t * HEAD_DIM, 2 * (t + 1) * HEAD_DIM)
            for u in range(2):
                sl = slice((2 * t + u) * HEAD_DIM, (2 * t + u + 1) * HEAD_DIM)
                yh = _rms(y[:, u * HEAD_DIM:(u + 1) * HEAD_DIM]) * gain_ref[...] * post
                if transposed:
                    out_ref[sl, :] = yh.T.astype(BF16)
                else:
                    out_ref[:, sl] = yh.astype(BF16)

    @pl.when(j == 0)
    def _():
        head_norm(qg_ref, qa_ref, q_scale, False)

    @pl.when(j == 1)
    def _():
        head_norm(kg_ref, ka_ref, 1.0, True)

    @pl.when(j == 2)
    def _():
        va_ref[...] = project(0, IN_TN).astype(BF16)

    @pl.when(j == 3)
    def _():
        cq_ref[...] = (_rms(project(0, Q_RANK)) * cqg_ref[...]).astype(BF16)
        y = project(Q_RANK, IN_TN)
        ckv_ref[...] = (_rms(y[:, :KV_RANK]) * ckvg_ref[...]).astype(BF16)
        kpe_ref[...] = y[:, KV_RANK:KV_RANK + LANES]


def _inproj(x, g, w4, qg, kg, cqg, ckvg, tm):
    rows = x.shape[0]
    row = lambda i, j: (i, 0)
    const = lambda i, j: (0, 0)
    outs = (
        jax.ShapeDtypeStruct((rows, NA_WIDTH), BF16),
        jax.ShapeDtypeStruct((NA_WIDTH, rows), BF16),
        jax.ShapeDtypeStruct((rows, NA_WIDTH), BF16),
        jax.ShapeDtypeStruct((rows, Q_RANK), BF16),
        jax.ShapeDtypeStruct((rows, KV_RANK), BF16),
        jax.ShapeDtypeStruct((rows, LANES), F32),
    )
    return pl.pallas_call(
        functools.partial(_inproj_body, q_scale=HEAD_DIM ** -0.5 * LOG2E),
        grid=(rows // tm, 4),
        in_specs=[
            pl.BlockSpec((tm, D_MODEL), row),
            pl.BlockSpec((1, D_MODEL), const),
            pl.BlockSpec((None, D_MODEL, IN_TN), lambda i, j: (j, 0, 0)),
            pl.BlockSpec((1, HEAD_DIM), const),
            pl.BlockSpec((1, HEAD_DIM), const),
            pl.BlockSpec((1, Q_RANK), const),
            pl.BlockSpec((1, KV_RANK), const),
        ],
        out_specs=[
            pl.BlockSpec((tm, NA_WIDTH), row),
            pl.BlockSpec((NA_WIDTH, tm), lambda i, j: (0, i)),
            pl.BlockSpec((tm, NA_WIDTH), row),
            pl.BlockSpec((tm, Q_RANK), row),
            pl.BlockSpec((tm, KV_RANK), row),
            pl.BlockSpec((tm, LANES), row),
        ],
        out_shape=outs,
        scratch_shapes=[pltpu.VMEM((tm, D_MODEL), BF16)],
        compiler_params=_cparams("parallel", "arbitrary"),
        name="inproj",
    )(x, g, w4, qg, kg, cqg, ckvg)


def _mla_up_body(cq_ref, ckv_ref, kpe_ref, cos_ref, sin_ref, cost_ref, sint_ref,
                 wqt_ref, wk_ref, wvt_ref, qgt_ref, qpad_ref, kg_ref, qt_ref, k_ref, vt_ref, *, q_scale):
    half = ROPE_DIM // 2
    ckv = ckv_ref[...]
    cqt = cq_ref[...].astype(F32).T.astype(BF16)
    ckvt = ckv.astype(F32).T.astype(BF16)
    qgt = qgt_ref[...]
    cost = cost_ref[...]
    sint = sint_ref[...]
    cosf = cos_ref[...]
    sinf = sin_ref[...]
    kg = kg_ref[...]

    kpe = kpe_ref[...]
    kpe_ss = jnp.sum(kpe * kpe, axis=-1, keepdims=True)
    r = kpe * kg[:, LANES:]
    kpe_rot = r * cosf + (pltpu.roll(r, half, 1) + pltpu.roll(r, LANES - half, 1)) * sinf

    for h in range(HEADS):
        q = _dot(wqt_ref[h], cqt)
        ms = jnp.sum(q * q, axis=0, keepdims=True) * (1.0 / QK_DIM)
        qn = q * (lax.rsqrt(ms + EPS) * q_scale) * qgt
        x1 = qn[LANES:LANES + half]
        x2 = qn[LANES + half:LANES + 2 * half]
        qt_ref[h, 0:LANES, :] = qn[:LANES].astype(BF16)
        qt_ref[h, LANES:LANES + half, :] = (x1 * cost - x2 * sint).astype(BF16)
        qt_ref[h, LANES + half:LANES + 2 * half, :] = (x2 * cost + x1 * sint).astype(BF16)
        qt_ref[h, LANES + 2 * half:, :] = qpad_ref[...]
        vt_ref[h] = _dot(wvt_ref[h], ckvt).astype(BF16)

    one_lane = (lax.broadcasted_iota(jnp.int32, (1, LANES), 1) == ROPE_DIM).astype(F32)
    for t in range(HEADS // 2):
        kn2 = _dot(ckv, wk_ref[t])
        for u in range(2):
            kn = kn2[:, u * LANES:(u + 1) * LANES]
            ms = (jnp.sum(kn * kn, axis=-1, keepdims=True) + kpe_ss) * (1.0 / QK_DIM)
            rn = lax.rsqrt(ms + EPS)
            k_ref[2 * t + u] = jnp.concatenate([kn * rn * kg[:, :LANES], kpe_rot * rn + one_lane],
                                               axis=-1).astype(BF16)


def _mla_up(cq, ckv, kpe, cosf, sinf, cost, sint, wqt, wk, wvt, qg, kg, shift, nb, tm):
    rows = cq.shape[0]
    nt = rows // nb // tm
    row = lambda b, i: (b * nt + i, 0)
    pos = lambda b, i: (i, 0)
    post = lambda b, i: (0, i)
    c2 = lambda b, i: (0, 0)
    c3 = lambda b, i: (0, 0, 0)
    qgt = jnp.broadcast_to(qg.reshape(QK_DIM, 1), (QK_DIM, tm))
    qpad = jnp.zeros((QK_PAD - QK_DIM, tm), F32).at[0].set(-shift).astype(BF16)
    outs = (
        jax.ShapeDtypeStruct((nb, HEADS, QK_PAD, nt * tm), BF16),
        jax.ShapeDtypeStruct((nb, HEADS, nt, tm, QK_PAD), BF16),
        jax.ShapeDtypeStruct((nb, HEADS, nt, HEAD_DIM, tm), BF16),
    )
    return pl.pallas_call(
        functools.partial(_mla_up_body, q_scale=QK_DIM ** -0.5 * LOG2E),
        grid=(nb, nt),
        in_specs=[
            pl.BlockSpec((tm, Q_RANK), row),
            pl.BlockSpec((tm, KV_RANK), row),
            pl.BlockSpec((tm, LANES), row),
            pl.BlockSpec((tm, LANES), pos),
            pl.BlockSpec((tm, LANES), pos),
            pl.BlockSpec((ROPE_DIM // 2, tm), post),
            pl.BlockSpec((ROPE_DIM // 2, tm), post),
            pl.BlockSpec((HEADS, QK_DIM, Q_RANK), c3),
            pl.BlockSpec((HEADS // 2, KV_RANK, 2 * HEAD_DIM), c3),
            pl.BlockSpec((HEADS, HEAD_DIM, KV_RANK), c3),
            pl.BlockSpec((QK_DIM, tm), c2),
            pl.BlockSpec((QK_PAD - QK_DIM, tm), c2),
            pl.BlockSpec((1, QK_PAD), c2),
        ],
        out_specs=[
            pl.BlockSpec((None, HEADS, QK_PAD, tm), lambda b, i: (b, 0, 0, i)),
            pl.BlockSpec((None, HEADS, None, tm, QK_PAD), lambda b, i: (b, 0, i, 0, 0)),
            pl.BlockSpec((None, HEADS, None, HEAD_DIM, tm), lambda b, i: (b, 0, i, 0, 0)),
        ],
        out_shape=outs,
        compiler_params=_cparams("parallel", "parallel"),
        name="mla_up",
    )(cq, ckv, kpe, cosf, sinf, cost, sint, wqt, wk, wvt, qgt, qpad, kg)


def _mla_attn_body(qt_ref, qtn_ref, k_ref, vt_ref, km_ref, vmt_ref, o_ref, acc_ref, p_ref, lc_ref,
                   *, nk, carry_over):
    qt = qt_ref[...]
    ng = nk // ATT_GROUP

    def probs(g, half, q=None):
        q = qt if q is None else q
        ls = None
        for u in range(ATT_GROUP):
            p = jnp.exp2(_dot(k_ref[g * ATT_GROUP + u], q))
            s1 = jnp.sum(p, axis=0, keepdims=True)
            ls = s1 if ls is None else ls + s1
            p_ref[half * ATT_GROUP + u] = p.astype(BF16)
        return ls

    def values(g, half):
        pv = None
        for u in range(ATT_GROUP):
            d = _dot(vt_ref[g * ATT_GROUP + u], p_ref[half * ATT_GROUP + u])
            pv = d if pv is None else pv + d
        acc_ref[...] += pv

    def meta_scores():
        s = _dot(km_ref[...], qt)
        key = lax.broadcasted_iota(jnp.int32, s.shape, 0)
        return jnp.where(key < N_META, s, NEG)

    def two_stages(i, l):
        l = l + probs(2 * i + 1, 1)
        values(2 * i, 0)
        l = l + probs(2 * i + 2, 0)
        values(2 * i + 1, 1)
        return l

    if carry_over:
        @pl.when(pl.program_id(2) == 0)
        def _():
            lc_ref[...] = probs(0, 0)
        l = lc_ref[...]
    else:
        l = probs(0, 0)

    p = jnp.exp2(meta_scores())
    acc_ref[...] = _dot(vmt_ref[...], p.astype(BF16))
    l = lax.fori_loop(0, ng // 2 - 1, two_stages, l + jnp.sum(p, axis=0, keepdims=True))
    l = l + probs(ng - 1, 1)
    values(ng - 2, 0)
    if carry_over:
        lc_ref[...] = probs(0, 0, qtn_ref[...])
    values(ng - 1, 1)
    o_ref[...] = acc_ref[...] / l

    @pl.when(jnp.min(l) < SOFTMAX_UNDERFLOW)
    def _():
        s = meta_scores()
        m0 = jnp.max(s, axis=0, keepdims=True)
        p = jnp.exp2(s - m0)
        acc_ref[...] = _dot(vmt_ref[...], p.astype(BF16))

        def chunk(c, carry):
            m_prev, l_prev = carry
            s = _dot(k_ref[c], qt)
            m_new = jnp.maximum(m_prev, jnp.max(s, axis=0, keepdims=True))
            alpha = jnp.exp2(m_prev - m_new)
            p = jnp.exp2(s - m_new)
            acc_ref[...] = alpha * acc_ref[...] + _dot(vt_ref[c], p.astype(BF16))
            return m_new, alpha * l_prev + jnp.sum(p, axis=0, keepdims=True)

        _, l_exact = lax.fori_loop(0, nk, chunk, (m0, jnp.sum(p, axis=0, keepdims=True)))
        o_ref[...] = acc_ref[...] / l_exact


def _mla_attn(qt, k, vt, km, vmt, tq, shared_q):
    nb, _, nk, tk, _ = k.shape
    nq = qt.shape[3] // tq
    qb = (lambda b: 0) if shared_q else (lambda b: b)
    return pl.pallas_call(
        functools.partial(_mla_attn_body, nk=nk, carry_over=nq > 1),
        grid=(nb, HEADS, nq),
        in_specs=[
            pl.BlockSpec((None, None, QK_PAD, tq), lambda b, h, i: (qb(b), h, 0, i)),
            pl.BlockSpec((None, None, QK_PAD, tq), lambda b, h, i: (qb(b), h, 0, jnp.minimum(i + 1, nq - 1))),
            pl.BlockSpec((None, None, nk, tk, QK_PAD), lambda b, h, i: (b, h, 0, 0, 0)),
            pl.BlockSpec((None, None, nk, HEAD_DIM, tk), lambda b, h, i: (b, h, 0, 0, 0)),
            pl.BlockSpec((None, None, None, META_ROWS, QK_PAD), lambda b, h, i: (0, h, 0, 0, 0)),
            pl.BlockSpec((None, None, None, HEAD_DIM, META_ROWS), lambda b, h, i: (0, h, 0, 0, 0)),
        ],
        out_specs=pl.BlockSpec((None, HEAD_DIM, tq), lambda b, h, i: (b, h, i)),
        out_shape=jax.ShapeDtypeStruct((nb, NA_WIDTH, nq * tq), F32),
        scratch_shapes=[pltpu.VMEM((HEAD_DIM, tq), F32), pltpu.VMEM((2 * ATT_GROUP, tk, tq), BF16),
                        pltpu.VMEM((1, tq), F32)],
        compiler_params=_cparams("parallel", "parallel", "arbitrary"),
        name="mla_attn",
    )(qt, qt, k, vt, km, vmt)


def _na_bias_body(rpb_ref, shift_ref, o_ref, t_ref, *, rows):
    n_dr = 2 * NA_WIN_ROWS - 1
    shape = (GRID_W, LANES)
    qc = lax.broadcasted_iota(jnp.int32, shape, 0)
    lane = lax.broadcasted_iota(jnp.int32, shape, 1)
    left = lane < GRID_W
    kc = jnp.where(left, lane, lane - GRID_W)
    cs = jnp.clip(qc - NA_WIN_COLS // 2, 0, GRID_W - NA_WIN_COLS)
    col_ok = (kc >= cs) & (kc < cs + NA_WIN_COLS)

    for dr in range(n_dr):
        x = jnp.broadcast_to(rpb_ref[dr:dr + 1, :], shape)
        a = pltpu.roll(x, LANES - (NA_WIN_COLS - 1), 1, stride=1, stride_axis=0)
        b = pltpu.roll(x, GRID_W - (NA_WIN_COLS - 1), 1, stride=1, stride_axis=0)
        t_ref[dr] = jnp.where(col_ok, jnp.where(left, a, b) * LOG2E - shift_ref[0], NEG)

    neg = jnp.full(shape, NEG, F32)
    for c, r0 in enumerate((0, 2 * NA_QROWS, rows - NA_QROWS)):
        for qr in range(NA_QROWS):
            abs_qr = r0 + qr
            rs = min(max(abs_qr - NA_WIN_ROWS // 2, 0), rows - NA_WIN_ROWS)
            for a in range(NA_KROWS // 2):
                halves = []
                for kr in (r0 - NA_QROWS + 2 * a, r0 - NA_QROWS + 2 * a + 1):
                    in_window = rs <= kr < rs + NA_WIN_ROWS
                    halves.append(t_ref[kr - abs_qr + NA_WIN_ROWS - 1] if in_window else neg)
                o_ref[c, qr * GRID_W:(qr + 1) * GRID_W, a * LANES:(a + 1) * LANES] = (
                    jnp.where(left, halves[0], halves[1]))


def _na_bias(rpb, shift, rows):
    n_dr, n_dc = rpb.shape[1:]
    rpb = jnp.pad(rpb, ((0, 0), (0, -n_dr % 8), (0, LANES - n_dc)))
    return pl.pallas_call(
        functools.partial(_na_bias_body, rows=rows),
        grid=(HEADS,),
        in_specs=[pl.BlockSpec((None,) + rpb.shape[1:], lambda h: (h, 0, 0)),
                  pl.BlockSpec(memory_space=pltpu.SMEM)],
        out_specs=pl.BlockSpec((3, None, NA_TQ, NA_KROWS * GRID_W), lambda h: (0, h, 0, 0)),
        out_shape=jax.ShapeDtypeStruct((3, HEADS, NA_TQ, NA_KROWS * GRID_W), F32),
        scratch_shapes=[pltpu.VMEM((2 * NA_WIN_ROWS - 1, GRID_W, LANES), F32)],
        compiler_params=_cparams("parallel"),
        name="na_bias",
    )(rpb, shift.reshape(1))


def _na_body(q_ref, kp_ref, kc_ref, kn_ref, vp_ref, vc_ref, vn_ref, km_ref, vm_ref,
             bias_ref, mb_ref, g_ref, o_ref, out_ref, s_ref):
    for h in range(HEADS):
        sl = slice(h * HEAD_DIM, (h + 1) * HEAD_DIM)
        q = q_ref[:, sl]
        for c, k_ref in enumerate((kp_ref, kc_ref, kn_ref)):
            cols = slice(c * NA_TQ, (c + 1) * NA_TQ)
            s_ref[h, :, cols] = _dot(q, k_ref[sl, :]) + bias_ref[h, :, cols]
        s_ref[h, :, 3 * NA_TQ:] = _dot(q, km_ref[sl, :]) + mb_ref[h]
    def softmax_pv(running_max):
        l_min = None
        for h in range(HEADS):
            sl = slice(h * HEAD_DIM, (h + 1) * HEAD_DIM)
            s = s_ref[h]
            if running_max:
                s = s - jnp.max(s, axis=-1, keepdims=True)
            p = jnp.exp2(s)
            l = jnp.sum(p, axis=-1, keepdims=True)
            pb = p.astype(BF16)
            o = (_dot(pb[:, 3 * NA_TQ:], vm_ref[:, sl])
                 + _dot(pb[:, :NA_TQ], vp_ref[:, sl])
                 + _dot(pb[:, NA_TQ:2 * NA_TQ], vc_ref[:, sl])
                 + _dot(pb[:, 2 * NA_TQ:3 * NA_TQ], vn_ref[:, sl]))
            out_ref[:, sl] = o / l
            l_min = l if l_min is None else jnp.minimum(l_min, l)
        return l_min

    l_min = softmax_pv(False)

    @pl.when(jnp.min(l_min) < SOFTMAX_UNDERFLOW)
    def _():
        softmax_pv(True)

    o_ref[...] = (_rms(out_ref[...]) * g_ref[...]).astype(BF16)


def _na_attn(qa, kat, va, kmt, vm, bias, mb, g, nb):
    rows = qa.shape[0]
    n = rows // nb // NA_TQ
    cur = lambda b, i: (b * n + i, 0)
    prev = lambda b, i: (b * n + jnp.maximum(i - 1, 0), 0)
    nxt = lambda b, i: (b * n + jnp.minimum(i + 1, n - 1), 0)
    blk = lambda f: pl.BlockSpec((NA_TQ, NA_WIDTH), f)
    blk_t = lambda f: pl.BlockSpec((NA_WIDTH, NA_TQ), lambda b, i: f(b, i)[::-1])
    case = lambda b, i: (jnp.where(i == 0, 0, jnp.where(i == n - 1, 2, 1)), 0, 0, 0)
    return pl.pallas_call(
        _na_body,
        grid=(nb, n),
        in_specs=[
            blk(cur), blk_t(prev), blk_t(cur), blk_t(nxt), blk(prev), blk(cur), blk(nxt),
            pl.BlockSpec((NA_WIDTH, META_ROWS), lambda b, i: (0, 0)),
            pl.BlockSpec((META_ROWS, NA_WIDTH), lambda b, i: (0, 0)),
            pl.BlockSpec((None, HEADS, NA_TQ, NA_KROWS * GRID_W), case),
            pl.BlockSpec((HEADS, 1, META_ROWS), lambda b, i: (0, 0, 0)),
            pl.BlockSpec((1, NA_WIDTH), lambda b, i: (0, 0)),
        ],
        out_specs=blk(cur),
        out_shape=jax.ShapeDtypeStruct((rows, NA_WIDTH), BF16),
        scratch_shapes=[pltpu.VMEM((NA_TQ, NA_WIDTH), F32),
                        pltpu.VMEM((HEADS, NA_TQ, NA_KROWS * GRID_W + META_ROWS), F32)],
        compiler_params=_cparams("parallel", "arbitrary"),
        name="na_attn",
    )(qa, kat, kat, kat, va, va, va, kmt, vm, bias, mb, g)


def _na_meta_body(q_ref, kt_ref, v_ref, mb_ref, g_ref, o_ref, out_ref):
    for h in range(HEADS):
        sl = slice(h * HEAD_DIM, (h + 1) * HEAD_DIM)
        s = _dot(q_ref[:, sl], kt_ref[sl, :]) + mb_ref[h]
        m = jnp.max(s, axis=-1, keepdims=True)
        p = jnp.exp2(s - m)
        l = jnp.sum(p, axis=-1, keepdims=True)
        out_ref[:, sl] = _dot(p.astype(BF16), v_ref[:, sl]) / l
    o_ref[...] = (_rms(out_ref[...]) * g_ref[...]).astype(BF16)


def _na_meta(qm, kmt, vm, mb, g):
    full = lambda s: pl.BlockSpec(s, lambda i: (0,) * len(s))
    return pl.pallas_call(
        _na_meta_body,
        grid=(1,),
        in_specs=[full((META_ROWS, NA_WIDTH)), full((NA_WIDTH, META_ROWS)), full((META_ROWS, NA_WIDTH)),
                  full((HEADS, 1, META_ROWS)), full((1, NA_WIDTH))],
        out_specs=full((META_ROWS, NA_WIDTH)),
        out_shape=jax.ShapeDtypeStruct((META_ROWS, NA_WIDTH), BF16),
        scratch_shapes=[pltpu.VMEM((META_ROWS, NA_WIDTH), F32)],
        compiler_params=_cparams("arbitrary"),
        name="na_meta",
    )(qm, kmt, vm, mb, g)


def _outproj_body(x_ref, a_ref, bt_ref, bg_ref, w_ref, fg_ref, h_ref, xn_ref):
    bn = (_rms(bt_ref[...].T) * bg_ref[...]).astype(BF16)
    mix = jnp.concatenate([a_ref[...], bn], axis=-1)
    h = x_ref[...] + _dot(mix, w_ref[...])
    h_ref[...] = h
    xn_ref[...] = (_rms(h) * fg_ref[...]).astype(BF16)


def _outproj(x, a, bt, bg, w, fg, tm, x_map, a_map, rows):
    c = lambda i: (0, 0)
    tpb = bt.shape[2] // tm
    return pl.pallas_call(
        _outproj_body,
        grid=(rows // tm,),
        in_specs=[
            pl.BlockSpec((tm, D_MODEL), x_map),
            pl.BlockSpec((tm, NA_WIDTH), a_map),
            pl.BlockSpec((None, NA_WIDTH, tm), lambda i: (i // tpb, 0, i % tpb)),
            pl.BlockSpec((1, NA_WIDTH), c),
            pl.BlockSpec((2 * NA_WIDTH, D_MODEL), c),
            pl.BlockSpec((1, D_MODEL), c),
        ],
        out_specs=[pl.BlockSpec((tm, D_MODEL), lambda i: (i, 0)),
                   pl.BlockSpec((tm, D_MODEL), lambda i: (i, 0))],
        out_shape=(jax.ShapeDtypeStruct((rows, D_MODEL), F32),
                   jax.ShapeDtypeStruct((rows, D_MODEL), BF16)),
        compiler_params=_cparams("parallel"),
        name="outproj",
    )(x, a, bt, bg, w, fg)


def _ffn_body(xm_ref, xp_ref, xx_ref, xmeta_ref, h_ref, wg_ref, wu_ref, wd_ref, cw_ref, cb_ref,
              o_ref, xe_ref, *, tm, tpb, j_bottom):
    i = pl.program_id(0)
    j = pl.program_id(1)
    half = tm // 2

    @pl.when(j == 0)
    def _():
        first = (i % tpb) == 0
        last = (i % tpb) == tpb - 1
        xe_ref[0:HALO, :] = jnp.where(first, xmeta_ref[...], xp_ref[...])
        xe_ref[HALO:HALO + tm, :] = xm_ref[...]
        xe_ref[HALO + tm:2 * HALO + tm, :] = jnp.where(last, jnp.zeros_like(xx_ref[...]), xx_ref[...])
        o_ref[0:half, :] = h_ref[...]
        o_ref[half:, :] = jnp.zeros((tm - half, o_ref.shape[1]), F32)

    @pl.when(j == j_bottom)
    def _():
        o_ref[half:, :] += h_ref[...]

    g = _dot(xe_ref[...], wg_ref[...])
    u = _dot(xe_ref[HALO:HALO + tm, :], wu_ref[...])
    cw = cw_ref[...]
    gc = (cb_ref[...] + cw[0:1] * g[HALO - 1:HALO - 1 + tm]
          + cw[1:2] * g[HALO:HALO + tm] + cw[2:3] * g[HALO + 1:HALO + 1 + tm])
    act = gc * (1.0 / (1.0 + jnp.exp(-gc))) * u
    o_ref[...] += _dot(act.astype(BF16), wd_ref[...])


def _ffn(xn, xn_meta, h1, wg, wu, wd, cw, cb, nb, tm, tf):
    rows = xn.shape[0]
    nt = rows // tm
    tpb = nt // nb
    hb = tm // HALO
    last_hb = rows // HALO - 1
    n_ff = D_FF // tf
    j_bottom = n_ff // 2
    return pl.pallas_call(
        functools.partial(_ffn_body, tm=tm, tpb=tpb, j_bottom=j_bottom),
        grid=(nt, n_ff),
        in_specs=[
            pl.BlockSpec((tm, D_MODEL), lambda i, j: (i, 0)),
            pl.BlockSpec((HALO, D_MODEL), lambda i, j: (jnp.maximum(i * hb - 1, 0), 0)),
            pl.BlockSpec((HALO, D_MODEL), lambda i, j: (jnp.minimum((i + 1) * hb, last_hb), 0)),
            pl.BlockSpec((HALO, D_MODEL), lambda i, j: ((i // tpb) * (META_ROWS // HALO), 0)),
            pl.BlockSpec((tm // 2, D_MODEL), lambda i, j: (2 * i + jnp.where(j >= j_bottom, 1, 0), 0)),
            pl.BlockSpec((D_MODEL, tf), lambda i, j: (0, j)),
            pl.BlockSpec((D_MODEL, tf), lambda i, j: (0, j)),
            pl.BlockSpec((tf, D_MODEL), lambda i, j: (j, 0)),
            pl.BlockSpec((3, tf), lambda i, j: (0, j)),
            pl.BlockSpec((1, tf), lambda i, j: (0, j)),
        ],
        out_specs=pl.BlockSpec((tm, D_MODEL), lambda i, j: (i, 0)),
        out_shape=jax.ShapeDtypeStruct((rows, D_MODEL), F32),
        scratch_shapes=[pltpu.VMEM((tm + 2 * HALO, D_MODEL), BF16)],
        compiler_params=_cparams("parallel", "arbitrary"),
        name="ffn",
    )(xn, xn, xn, xn_meta, h1, wg, wu, wd, cw, cb)


def _rope_tables(first, n):
    inv = ROPE_THETA ** (-np.arange(0, ROPE_DIM, 2, dtype=np.float64) / ROPE_DIM)
    ang = (first + np.arange(n, dtype=np.float64))[:, None] * inv[None, :]
    cos, sin = np.cos(ang), np.sin(ang)
    z = np.zeros_like(cos)
    tables = (np.concatenate([cos, cos, z, z], axis=-1), np.concatenate([-sin, sin, z, z], axis=-1), cos.T, sin.T)
    return tuple(jnp.asarray(t, F32) for t in tables)


def _softmax_shift(q_gain, k_gain):
    bound = LOG2E * QK_DIM ** 0.5 * jnp.max(jnp.abs(q_gain)) * jnp.max(jnp.abs(k_gain))
    return bound * (1.02 * (1.0 + 2.0 ** -7))


def _na_shift(q_gain, k_gain, rpb, meta_bias):
    qk = LOG2E * HEAD_DIM ** 0.5 * jnp.max(jnp.abs(q_gain)) * jnp.max(jnp.abs(k_gain))
    return 1.02 * qk + LOG2E * jnp.maximum(jnp.max(rpb), jnp.max(meta_bias))


def _pad_lanes(v, width):
    return jnp.pad(v, ((0, 0), (0, width - v.shape[-1])))


def kernel(x, meta_tokens, mix_norm_g, w_in, na_q_g, na_k_g, na_rpb, na_meta_bias, mla_cq_g, mla_ckv_g,
           w_q_up, w_kv_up, mla_q_g, mla_k_g, na_out_g, mla_out_g, w_out, ffn_norm_g, w_gate, w_up,
           conv_w, conv_b, w_down):
    nb, seq, d = x.shape
    assert d == D_MODEL and w_in.shape[0] == 1, "one layer of width 2048"
    rows_grid = seq // GRID_W
    assert seq % max(IN_TM, UP_TM, ATT_TQ, FFN_TM) == 0 and rows_grid >= 4 * NA_QROWS
    assert seq % (2 * ATT_GROUP * UP_TM) == 0, "the latent-attention pipeline runs two key groups per trip"

    w_in_p = _pad_lanes(w_in[0], 4 * IN_TN).astype(BF16)
    w4 = w_in_p.reshape(D_MODEL, 4, IN_TN).transpose(1, 0, 2)
    wqt = w_q_up[0].reshape(Q_RANK, HEADS, QK_DIM).transpose(1, 2, 0).astype(BF16)
    wkv = w_kv_up[0].reshape(KV_RANK, HEADS, 2 * HEAD_DIM)
    wk = wkv[:, :, :HEAD_DIM].reshape(KV_RANK, HEADS // 2, 2 * HEAD_DIM).transpose(1, 0, 2).astype(BF16)
    wvt = wkv[:, :, HEAD_DIM:].transpose(1, 2, 0).astype(BF16)
    wo = w_out[0].astype(BF16)
    wg, wu, wd = w_gate[0].astype(BF16), w_up[0].astype(BF16), w_down[0].astype(BF16)
    row = lambda v: v.reshape(1, -1)
    kg_pad = _pad_lanes(mla_k_g, QK_PAD)
    na_shift = _na_shift(na_q_g, na_k_g, na_rpb, na_meta_bias)
    mb = jnp.pad(na_meta_bias[0] * LOG2E - na_shift, ((0, 0), (0, META_ROWS - N_META)),
                 constant_values=NEG).reshape(HEADS, 1, META_ROWS)

    xr = x.reshape(nb * seq, D_MODEL)
    xm = jnp.pad(meta_tokens.astype(x.dtype), ((0, META_ROWS - N_META), (0, 0)))
    rope_r = _rope_tables(N_META, seq)
    rope_m = _rope_tables(0, META_ROWS)

    inproj = functools.partial(_inproj, g=row(mix_norm_g), w4=w4, qg=row(na_q_g), kg=row(na_k_g),
                               cqg=row(mla_cq_g), ckvg=row(mla_ckv_g))
    qa, ka, va, cq, ckv, kpe = inproj(xr, tm=IN_TM)
    qa_m, ka_m, va_m, cq_m, ckv_m, kpe_m = inproj(xm, tm=META_ROWS)

    up = functools.partial(_mla_up, wqt=wqt, wk=wk, wvt=wvt, qg=mla_q_g, kg=kg_pad,
                           shift=_softmax_shift(mla_q_g, mla_k_g))
    qt, kk, vt = up(cq, ckv, kpe, *rope_r, nb=nb, tm=UP_TM)
    qt_m, kk_m, vt_m = up(cq_m, ckv_m, kpe_m, *rope_m, nb=1, tm=META_ROWS)

    bias = _na_bias(na_rpb[0], na_shift, rows_grid)
    a_n = _na_attn(qa, ka, va, ka_m, va_m, bias, mb, row(na_out_g), nb)
    a_n_m = _na_meta(qa_m, ka_m, va_m, mb, row(na_out_g))

    out_b = _mla_attn(qt, kk, vt, kk_m, vt_m, ATT_TQ, shared_q=False)
    out_b_m = _mla_attn(qt_m, kk, vt, kk_m, vt_m, META_ROWS, shared_q=True)

    op = functools.partial(_outproj, bg=row(mla_out_g), w=wo, fg=row(ffn_norm_g))
    ident = lambda i: (i, 0)
    zero = lambda i: (0, 0)
    h1, xn2 = op(xr, a_n, out_b, tm=OUT_TM, x_map=ident, a_map=ident, rows=nb * seq)
    _, xn2_m = op(xm, a_n_m, out_b_m, tm=META_ROWS, x_map=zero, a_map=zero, rows=nb * META_ROWS)

    out = _ffn(xn2, xn2_m, h1, wg, wu, wd, conv_w[0], row(conv_b), nb, FFN_TM, FFN_TF)
    return out.reshape(nb, seq, D_MODEL)
```

```python
import functools

import jax
import jax.numpy as jnp
import numpy as np
from jax import lax
from jax.experimental import pallas as pl
from jax.experimental.pallas import tpu as pltpu

F32 = jnp.float32
BF16 = jnp.bfloat16

LANES = 128
BF16_SUBLANES = 16
VMEM_LIMIT_BYTES = 56 * 1024 * 1024

D_MODEL = 2048
GRID_W = 64
N_META = 16
EPS = 1e-6
NEG = -1e30
LOG2E = 1.4426950408889634
HEADS = 8
HEAD_DIM = 128
NA_WIDTH = HEADS * HEAD_DIM
NA_WIN_ROWS = 8
NA_WIN_COLS = 16
Q_RANK = 512
KV_RANK = 256
ROPE_DIM = 64
QK_DIM = HEAD_DIM + ROPE_DIM
QK_PAD = 2 * LANES
ROPE_THETA = 10000.0
D_FF = 5632
META_ROWS = LANES
SOFTMAX_UNDERFLOW = 2.0 ** -100

IN_TM = 1024
IN_TN = 1024
UP_TM = 512
ATT_TQ = 1024
ATT_GROUP = 8
NA_QROWS = 4
NA_TQ = NA_QROWS * GRID_W
NA_KROWS = 3 * NA_QROWS
OUT_TM = 512
FFN_TM = 1024
FFN_TF = 512
HALO = BF16_SUBLANES


def _cparams(*sem):
    return pltpu.CompilerParams(dimension_semantics=sem, vmem_limit_bytes=VMEM_LIMIT_BYTES)


def _rms(v):
    return v * lax.rsqrt(jnp.mean(v * v, axis=-1, keepdims=True) + EPS)


def _dot(a, b):
    return jnp.dot(a, b, preferred_element_type=F32)


def _inproj_body(x_ref, g_ref, w_ref, qg_ref, kg_ref, cqg_ref, ckvg_ref,
                 qa_ref, ka_ref, va_ref, cq_ref, ckv_ref, kpe_ref, xn_ref, *, q_scale):
    j = pl.program_id(1)

    @pl.when(j == 0)
    def _():
        xn_ref[...] = (_rms(x_ref[...]) * g_ref[...]).astype(BF16)

    def project(lo, hi):
        return _dot(xn_ref[...], w_ref[:, lo:hi])

    def head_norm(gain_ref, out_ref, post, transposed):
        for t in range(HEADS // 2):
            y = project(2 * t * HEAD_DIM, 2 * (t + 1) * HEAD_DIM)
            for u in range(2):
                sl = slice((2 * t + u) * HEAD_DIM, (2 * t + u + 1) * HEAD_DIM)
                yh = _rms(y[:, u * HEAD_DIM:(u + 1) * HEAD_DIM]) * gain_ref[...] * post
                if transposed:
                    out_ref[sl, :] = yh.T.astype(BF16)
                else:
                    out_ref[:, sl] = yh.astype(BF16)

    @pl.when(j == 0)
    def _():
        head_norm(qg_ref, qa_ref, q_scale, False)

    @pl.when(j == 1)
    def _():
        head_norm(kg_ref, ka_ref, 1.0, True)

    @pl.when(j == 2)
    def _():
        va_ref[...] = project(0, IN_TN).astype(BF16)

    @pl.when(j == 3)
    def _():
        cq_ref[...] = (_rms(project(0, Q_RANK)) * cqg_ref[...]).astype(BF16)
        y = project(Q_RANK, IN_TN)
        ckv_ref[...] = (_rms(y[:, :KV_RANK]) * ckvg_ref[...]).astype(BF16)
        kpe_ref[...] = y[:, KV_RANK:KV_RANK + LANES]


def _inproj(x, g, w4, qg, kg, cqg, ckvg, tm):
    rows = x.shape[0]
    row = lambda i, j: (i, 0)
    const = lambda i, j: (0, 0)
    outs = (
        jax.ShapeDtypeStruct((rows, NA_WIDTH), BF16),
        jax.ShapeDtypeStruct((NA_WIDTH, rows), BF16),
        jax.ShapeDtypeStruct((rows, NA_WIDTH), BF16),
        jax.ShapeDtypeStruct((rows, Q_RANK), BF16),
        jax.ShapeDtypeStruct((rows, KV_RANK), BF16),
        jax.ShapeDtypeStruct((rows, LANES), F32),
    )
    return pl.pallas_call(
        functools.partial(_inproj_body, q_scale=HEAD_DIM ** -0.5 * LOG2E),
        grid=(rows // tm, 4),
        in_specs=[
            pl.BlockSpec((tm, D_MODEL), row),
            pl.BlockSpec((1, D_MODEL), const),
            pl.BlockSpec((None, D_MODEL, IN_TN), lambda i, j: (j, 0, 0)),
            pl.BlockSpec((1, HEAD_DIM), const),
            pl.BlockSpec((1, HEAD_DIM), const),
            pl.BlockSpec((1, Q_RANK), const),
            pl.BlockSpec((1, KV_RANK), const),
        ],
        out_specs=[
            pl.BlockSpec((tm, NA_WIDTH), row),
            pl.BlockSpec((NA_WIDTH, tm), lambda i, j: (0, i)),
            pl.BlockSpec((tm, NA_WIDTH), row),
            pl.BlockSpec((tm, Q_RANK), row),
            pl.BlockSpec((tm, KV_RANK), row),
            pl.BlockSpec((tm, LANES), row),
        ],
        out_shape=outs,
        scratch_shapes=[pltpu.VMEM((tm, D_MODEL), BF16)],
        compiler_params=_cparams("parallel", "arbitrary"),
        name="inproj",
    )(x, g, w4, qg, kg, cqg, ckvg)


def _mla_up_body(cq_ref, ckv_ref, kpe_ref, cos_ref, sin_ref, cost_ref, sint_ref,
                 wqt_ref, wk_ref, wvt_ref, qgt_ref, qpad_ref, kg_ref, qt_ref, k_ref, vt_ref, *, q_scale):
    half = ROPE_DIM // 2
    ckv = ckv_ref[...]
    cqt = cq_ref[...].astype(F32).T.astype(BF16)
    ckvt = ckv.astype(F32).T.astype(BF16)
    qgt = qgt_ref[...]
    cost = cost_ref[...]
    sint = sint_ref[...]
    cosf = cos_ref[...]
    sinf = sin_ref[...]
    kg = kg_ref[...]

    kpe = kpe_ref[...]
    kpe_ss = jnp.sum(kpe * kpe, axis=-1, keepdims=True)
    r = kpe * kg[:, LANES:]
    kpe_rot = r * cosf + (pltpu.roll(r, half, 1) + pltpu.roll(r, LANES - half, 1)) * sinf

    for h in range(HEADS):
        q = _dot(wqt_ref[h], cqt)
        ms = jnp.sum(q * q, axis=0, keepdims=True) * (1.0 / QK_DIM)
        qn = q * (lax.rsqrt(ms + EPS) * q_scale) * qgt
        x1 = qn[LANES:LANES + half]
        x2 = qn[LANES + half:LANES + 2 * half]
        qt_ref[h, 0:LANES, :] = qn[:LANES].astype(BF16)
        qt_ref[h, LANES:LANES + half, :] = (x1 * cost - x2 * sint).astype(BF16)
        qt_ref[h, LANES + half:LANES + 2 * half, :] = (x2 * cost + x1 * sint).astype(BF16)
        qt_ref[h, LANES + 2 * half:, :] = qpad_ref[...]
        vt_ref[h] = _dot(wvt_ref[h], ckvt).astype(BF16)

    one_lane = (lax.broadcasted_iota(jnp.int32, (1, LANES), 1) == ROPE_DIM).astype(F32)
    for t in range(HEADS // 2):
        kn2 = _dot(ckv, wk_ref[t])
        for u in range(2):
            kn = kn2[:, u * LANES:(u + 1) * LANES]
            ms = (jnp.sum(kn * kn, axis=-1, keepdims=True) + kpe_ss) * (1.0 / QK_DIM)
            rn = lax.rsqrt(ms + EPS)
            k_ref[2 * t + u] = jnp.concatenate([kn * rn * kg[:, :LANES], kpe_rot * rn + one_lane],
                                               axis=-1).astype(BF16)


def _mla_up(cq, ckv, kpe, cosf, sinf, cost, sint, wqt, wk, wvt, qg, kg, shift, nb, tm):
    rows = cq.shape[0]
    nt = rows // nb // tm
    row = lambda b, i: (b * nt + i, 0)
    pos = lambda b, i: (i, 0)
    post = lambda b, i: (0, i)
    c2 = lambda b, i: (0, 0)
    c3 = lambda b, i: (0, 0, 0)
    qgt = jnp.broadcast_to(qg.reshape(QK_DIM, 1), (QK_DIM, tm))
    qpad = jnp.zeros((QK_PAD - QK_DIM, tm), F32).at[0].set(-shift).astype(BF16)
    outs = (
        jax.ShapeDtypeStruct((nb, HEADS, QK_PAD, nt * tm), BF16),
        jax.ShapeDtypeStruct((nb, HEADS, nt, tm, QK_PAD), BF16),
        jax.ShapeDtypeStruct((nb, HEADS, nt, HEAD_DIM, tm), BF16),
    )
    return pl.pallas_call(
        functools.partial(_mla_up_body, q_scale=QK_DIM ** -0.5 * LOG2E),
        grid=(nb, nt),
        in_specs=[
            pl.BlockSpec((tm, Q_RANK), row),
            pl.BlockSpec((tm, KV_RANK), row),
            pl.BlockSpec((tm, LANES), row),
            pl.BlockSpec((tm, LANES), pos),
            pl.BlockSpec((tm, LANES), pos),
            pl.BlockSpec((ROPE_DIM // 2, tm), post),
            pl.BlockSpec((ROPE_DIM // 2, tm), post),
            pl.BlockSpec((HEADS, QK_DIM, Q_RANK), c3),
            pl.BlockSpec((HEADS // 2, KV_RANK, 2 * HEAD_DIM), c3),
            pl.BlockSpec((HEADS, HEAD_DIM, KV_RANK), c3),
            pl.BlockSpec((QK_DIM, tm), c2),
            pl.BlockSpec((QK_PAD - QK_DIM, tm), c2),
            pl.BlockSpec((1, QK_PAD), c2),
        ],
        out_specs=[
            pl.BlockSpec((None, HEADS, QK_PAD, tm), lambda b, i: (b, 0, 0, i)),
            pl.BlockSpec((None, HEADS, None, tm, QK_PAD), lambda b, i: (b, 0, i, 0, 0)),
            pl.BlockSpec((None, HEADS, None, HEAD_DIM, tm), lambda b, i: (b, 0, i, 0, 0)),
        ],
        out_shape=outs,
        compiler_params=_cparams("parallel", "parallel"),
        name="mla_up",
    )(cq, ckv, kpe, cosf, sinf, cost, sint, wqt, wk, wvt, qgt, qpad, kg)


def _mla_attn_body(qt_ref, qtn_ref, k_ref, vt_ref, km_ref, vmt_ref, o_ref, acc_ref, p_ref, lc_ref,
                   *, nk, carry_over):
    qt = qt_ref[...]
    ng = nk // ATT_GROUP

    def probs(g, half, q=None):
        q = qt if q is None else q
        ls = None
        for u in range(ATT_GROUP):
            p = jnp.exp2(_dot(k_ref[g * ATT_GROUP + u], q))
            s1 = jnp.sum(p, axis=0, keepdims=True)
            ls = s1 if ls is None else ls + s1
            p_ref[half * ATT_GROUP + u] = p.astype(BF16)
        return ls

    def values(g, half):
        pv = None
        for u in range(ATT_GROUP):
            d = _dot(vt_ref[g * ATT_GROUP + u], p_ref[half * ATT_GROUP + u])
            pv = d if pv is None else pv + d
        acc_ref[...] += pv

    def meta_scores():
        s = _dot(km_ref[...], qt)
        key = lax.broadcasted_iota(jnp.int32, s.shape, 0)
        return jnp.where(key < N_META, s, NEG)

    def two_stages(i, l):
        l = l + probs(2 * i + 1, 1)
        values(2 * i, 0)
        l = l + probs(2 * i + 2, 0)
        values(2 * i + 1, 1)
        return l

    if carry_over:
        @pl.when(pl.program_id(2) == 0)
        def _():
            lc_ref[...] = probs(0, 0)
        l = lc_ref[...]
    else:
        l = probs(0, 0)

    p = jnp.exp2(meta_scores())
    acc_ref[...] = _dot(vmt_ref[...], p.astype(BF16))
    l = lax.fori_loop(0, ng // 2 - 1, two_stages, l + jnp.sum(p, axis=0, keepdims=True))
    l = l + probs(ng - 1, 1)
    values(ng - 2, 0)
    if carry_over:
        lc_ref[...] = probs(0, 0, qtn_ref[...])
    values(ng - 1, 1)
    o_ref[...] = acc_ref[...] / l

    @pl.when(jnp.min(l) < SOFTMAX_UNDERFLOW)
    def _():
        s = meta_scores()
        m0 = jnp.max(s, axis=0, keepdims=True)
        p = jnp.exp2(s - m0)
        acc_ref[...] = _dot(vmt_ref[...], p.astype(BF16))

        def chunk(c, carry):
            m_prev, l_prev = carry
            s = _dot(k_ref[c], qt)
            m_new = jnp.maximum(m_prev, jnp.max(s, axis=0, keepdims=True))
            alpha = jnp.exp2(m_prev - m_new)
            p = jnp.exp2(s - m_new)
            acc_ref[...] = alpha * acc_ref[...] + _dot(vt_ref[c], p.astype(BF16))
            return m_new, alpha * l_prev + jnp.sum(p, axis=0, keepdims=True)

        _, l_exact = lax.fori_loop(0, nk, chunk, (m0, jnp.sum(p, axis=0, keepdims=True)))
        o_ref[...] = acc_ref[...] / l_exact


def _mla_attn(qt, k, vt, km, vmt, tq, shared_q):
    nb, _, nk, tk, _ = k.shape
    nq = qt.shape[3] // tq
    qb = (lambda b: 0) if shared_q else (lambda b: b)
    return pl.pallas_call(
        functools.partial(_mla_attn_body, nk=nk, carry_over=nq > 1),
        grid=(nb, HEADS, nq),
        in_specs=[
            pl.BlockSpec((None, None, QK_PAD, tq), lambda b, h, i: (qb(b), h, 0, i)),
            pl.BlockSpec((None, None, QK_PAD, tq), lambda b, h, i: (qb(b), h, 0, jnp.minimum(i + 1, nq - 1))),
            pl.BlockSpec((None, None, nk, tk, QK_PAD), lambda b, h, i: (b, h, 0, 0, 0)),
            pl.BlockSpec((None, None, nk, HEAD_DIM, tk), lambda b, h, i: (b, h, 0, 0, 0)),
            pl.BlockSpec((None, None, None, META_ROWS, QK_PAD), lambda b, h, i: (0, h, 0, 0, 0)),
            pl.BlockSpec((None, None, None, HEAD_DIM, META_ROWS), lambda b, h, i: (0, h, 0, 0, 0)),
        ],
        out_specs=pl.BlockSpec((None, HEAD_DIM, tq), lambda b, h, i: (b, h, i)),
        out_shape=jax.ShapeDtypeStruct((nb, NA_WIDTH, nq * tq), F32),
        scratch_shapes=[pltpu.VMEM((HEAD_DIM, tq), F32), pltpu.VMEM((2 * ATT_GROUP, tk, tq), BF16),
                        pltpu.VMEM((1, tq), F32)],
        compiler_params=_cparams("parallel", "parallel", "arbitrary"),
        name="mla_attn",
    )(qt, qt, k, vt, km, vmt)


def _na_bias_body(rpb_ref, shift_ref, o_ref, t_ref, *, rows):
    n_dr = 2 * NA_WIN_ROWS - 1
    shape = (GRID_W, LANES)
    qc = lax.broadcasted_iota(jnp.int32, shape, 0)
    lane = lax.broadcasted_iota(jnp.int32, shape, 1)
    left = lane < GRID_W
    kc = jnp.where(left, lane, lane - GRID_W)
    cs = jnp.clip(qc - NA_WIN_COLS // 2, 0, GRID_W - NA_WIN_COLS)
    col_ok = (kc >= cs) & (kc < cs + NA_WIN_COLS)

    for dr in range(n_dr):
        x = jnp.broadcast_to(rpb_ref[dr:dr + 1, :], shape)
        a = pltpu.roll(x, LANES - (NA_WIN_COLS - 1), 1, stride=1, stride_axis=0)
        b = pltpu.roll(x, GRID_W - (NA_WIN_COLS - 1), 1, stride=1, stride_axis=0)
        t_ref[dr] = jnp.where(col_ok, jnp.where(left, a, b) * LOG2E - shift_ref[0], NEG)

    neg = jnp.full(shape, NEG, F32)
    for c, r0 in enumerate((0, 2 * NA_QROWS, rows - NA_QROWS)):
        for qr in range(NA_QROWS):
            abs_qr = r0 + qr
            rs = min(max(abs_qr - NA_WIN_ROWS // 2, 0), rows - NA_WIN_ROWS)
            for a in range(NA_KROWS // 2):
                halves = []
                for kr in (r0 - NA_QROWS + 2 * a, r0 - NA_QROWS + 2 * a + 1):
                    in_window = rs <= kr < rs + NA_WIN_ROWS
                    halves.append(t_ref[kr - abs_qr + NA_WIN_ROWS - 1] if in_window else neg)
                o_ref[c, qr * GRID_W:(qr + 1) * GRID_W, a * LANES:(a + 1) * LANES] = (
                    jnp.where(left, halves[0], halves[1]))


def _na_bias(rpb, shift, rows):
    n_dr, n_dc = rpb.shape[1:]
    rpb = jnp.pad(rpb, ((0, 0), (0, -n_dr % 8), (0, LANES - n_dc)))
    return pl.pallas_call(
        functools.partial(_na_bias_body, rows=rows),
        grid=(HEADS,),
        in_specs=[pl.BlockSpec((None,) + rpb.shape[1:], lambda h: (h, 0, 0)),
                  pl.BlockSpec(memory_space=pltpu.SMEM)],
        out_specs=pl.BlockSpec((3, None, NA_TQ, NA_KROWS * GRID_W), lambda h: (0, h, 0, 0)),
        out_shape=jax.ShapeDtypeStruct((3, HEADS, NA_TQ, NA_KROWS * GRID_W), F32),
        scratch_shapes=[pltpu.VMEM((2 * NA_WIN_ROWS - 1, GRID_W, LANES), F32)],
        compiler_params=_cparams("parallel"),
        name="na_bias",
    )(rpb, shift.reshape(1))


def _na_body(q_ref, kp_ref, kc_ref, kn_ref, vp_ref, vc_ref, vn_ref, km_ref, vm_ref,
             bias_ref, mb_ref, g_ref, o_ref, out_ref, s_ref):
    for h in range(HEADS):
        sl = slice(h * HEAD_DIM, (h + 1) * HEAD_DIM)
        q = q_ref[:, sl]
        for c, k_ref in enumerate((kp_ref, kc_ref, kn_ref)):
            cols = slice(c * NA_TQ, (c + 1) * NA_TQ)
            s_ref[h, :, cols] = _dot(q, k_ref[sl, :]) + bias_ref[h, :, cols]
        s_ref[h, :, 3 * NA_TQ:] = _dot(q, km_ref[sl, :]) + mb_ref[h]
    def softmax_pv(running_max):
        l_min = None
        for h in range(HEADS):
            sl = slice(h * HEAD_DIM, (h + 1) * HEAD_DIM)
            s = s_ref[h]
            if running_max:
                s = s - jnp.max(s, axis=-1, keepdims=True)
            p = jnp.exp2(s)
            l = jnp.sum(p, axis=-1, keepdims=True)
            pb = p.astype(BF16)
            o = (_dot(pb[:, 3 * NA_TQ:], vm_ref[:, sl])
                 + _dot(pb[:, :NA_TQ], vp_ref[:, sl])
                 + _dot(pb[:, NA_TQ:2 * NA_TQ], vc_ref[:, sl])
                 + _dot(pb[:, 2 * NA_TQ:3 * NA_TQ], vn_ref[:, sl]))
            out_ref[:, sl] = o / l
            l_min = l if l_min is None else jnp.minimum(l_min, l)
        return l_min

    l_min = softmax_pv(False)

    @pl.when(jnp.min(l_min) < SOFTMAX_UNDERFLOW)
    def _():
        softmax_pv(True)

    o_ref[...] = (_rms(out_ref[...]) * g_ref[...]).astype(BF16)


def _na_attn(qa, kat, va, kmt, vm, bias, mb, g, nb):
    rows = qa.shape[0]
    n = rows // nb // NA_TQ
    cur = lambda b, i: (b * n + i, 0)
    prev = lambda b, i: (b * n + jnp.maximum(i - 1, 0), 0)
    nxt = lambda b, i: (b * n + jnp.minimum(i + 1, n - 1), 0)
    blk = lambda f: pl.BlockSpec((NA_TQ, NA_WIDTH), f)
    blk_t = lambda f: pl.BlockSpec((NA_WIDTH, NA_TQ), lambda b, i: f(b, i)[::-1])
    case = lambda b, i: (jnp.where(i == 0, 0, jnp.where(i == n - 1, 2, 1)), 0, 0, 0)
    return pl.pallas_call(
        _na_body,
        grid=(nb, n),
        in_specs=[
            blk(cur), blk_t(prev), blk_t(cur), blk_t(nxt), blk(prev), blk(cur), blk(nxt),
            pl.BlockSpec((NA_WIDTH, META_ROWS), lambda b, i: (0, 0)),
            pl.BlockSpec((META_ROWS, NA_WIDTH), lambda b, i: (0, 0)),
            pl.BlockSpec((None, HEADS, NA_TQ, NA_KROWS * GRID_W), case),
            pl.BlockSpec((HEADS, 1, META_ROWS), lambda b, i: (0, 0, 0)),
            pl.BlockSpec((1, NA_WIDTH), lambda b, i: (0, 0)),
        ],
        out_specs=blk(cur),
        out_shape=jax.ShapeDtypeStruct((rows, NA_WIDTH), BF16),
        scratch_shapes=[pltpu.VMEM((NA_TQ, NA_WIDTH), F32),
                        pltpu.VMEM((HEADS, NA_TQ, NA_KROWS * GRID_W + META_ROWS), F32)],
        compiler_params=_cparams("parallel", "arbitrary"),
        name="na_attn",
    )(qa, kat, kat, kat, va, va, va, kmt, vm, bias, mb, g)


def _na_meta_body(q_ref, kt_ref, v_ref, mb_ref, g_ref, o_ref, out_ref):
    for h in range(HEADS):
        sl = slice(h * HEAD_DIM, (h + 1) * HEAD_DIM)
        s = _dot(q_ref[:, sl], kt_ref[sl, :]) + mb_ref[h]
        m = jnp.max(s, axis=-1, keepdims=True)
        p = jnp.exp2(s - m)
        l = jnp.sum(p, axis=-1, keepdims=True)
        out_ref[:, sl] = _dot(p.astype(BF16), v_ref[:, sl]) / l
    o_ref[...] = (_rms(out_ref[...]) * g_ref[...]).astype(BF16)


def _na_meta(qm, kmt, vm, mb, g):
    full = lambda s: pl.BlockSpec(s, lambda i: (0,) * len(s))
    return pl.pallas_call(
        _na_meta_body,
        grid=(1,),
        in_specs=[full((META_ROWS, NA_WIDTH)), full((NA_WIDTH, META_ROWS)), full((META_ROWS, NA_WIDTH)),
                  full((HEADS, 1, META_ROWS)), full((1, NA_WIDTH))],
        out_specs=full((META_ROWS, NA_WIDTH)),
        out_shape=jax.ShapeDtypeStruct((META_ROWS, NA_WIDTH), BF16),
        scratch_shapes=[pltpu.VMEM((META_ROWS, NA_WIDTH), F32)],
        compiler_params=_cparams("arbitrary"),
        name="na_meta",
    )(qm, kmt, vm, mb, g)


def _outproj_body(x_ref, a_ref, bt_ref, bg_ref, w_ref, fg_ref, h_ref, xn_ref):
    bn = (_rms(bt_ref[...].T) * bg_ref[...]).astype(BF16)
    mix = jnp.concatenate([a_ref[...], bn], axis=-1)
    h = x_ref[...] + _dot(mix, w_ref[...])
    h_ref[...] = h
    xn_ref[...] = (_rms(h) * fg_ref[...]).astype(BF16)


def _outproj(x, a, bt, bg, w, fg, tm, x_map, a_map, rows):
    c = lambda i: (0, 0)
    tpb = bt.shape[2] // tm
    return pl.pallas_call(
        _outproj_body,
        grid=(rows // tm,),
        in_specs=[
            pl.BlockSpec((tm, D_MODEL), x_map),
            pl.BlockSpec((tm, NA_WIDTH), a_map),
            pl.BlockSpec((None, NA_WIDTH, tm), lambda i: (i // tpb, 0, i % tpb)),
            pl.BlockSpec((1, NA_WIDTH), c),
            pl.BlockSpec((2 * NA_WIDTH, D_MODEL), c),
            pl.BlockSpec((1, D_MODEL), c),
        ],
        out_specs=[pl.BlockSpec((tm, D_MODEL), lambda i: (i, 0)),
                   pl.BlockSpec((tm, D_MODEL), lambda i: (i, 0))],
        out_shape=(jax.ShapeDtypeStruct((rows, D_MODEL), F32),
                   jax.ShapeDtypeStruct((rows, D_MODEL), BF16)),
        compiler_params=_cparams("parallel"),
        name="outproj",
    )(x, a, bt, bg, w, fg)


def _ffn_body(xm_ref, xp_ref, xx_ref, xmeta_ref, h_ref, wg_ref, wu_ref, wd_ref, cw_ref, cb_ref,
              o_ref, xe_ref, *, tm, tpb, j_bottom):
    i = pl.program_id(0)
    j = pl.program_id(1)
    half = tm // 2

    @pl.when(j == 0)
    def _():
        first = (i % tpb) == 0
        last = (i % tpb) == tpb - 1
        xe_ref[0:HALO, :] = jnp.where(first, xmeta_ref[...], xp_ref[...])
        xe_ref[HALO:HALO + tm, :] = xm_ref[...]
        xe_ref[HALO + tm:2 * HALO + tm, :] = jnp.where(last, jnp.zeros_like(xx_ref[...]), xx_ref[...])
        o_ref[0:half, :] = h_ref[...]
        o_ref[half:, :] = jnp.zeros((tm - half, o_ref.shape[1]), F32)

    @pl.when(j == j_bottom)
    def _():
        o_ref[half:, :] += h_ref[...]

    g = _dot(xe_ref[...], wg_ref[...])
    u = _dot(xe_ref[HALO:HALO + tm, :], wu_ref[...])
    cw = cw_ref[...]
    gc = (cb_ref[...] + cw[0:1] * g[HALO - 1:HALO - 1 + tm]
          + cw[1:2] * g[HALO:HALO + tm] + cw[2:3] * g[HALO + 1:HALO + 1 + tm])
    act = gc * (1.0 / (1.0 + jnp.exp(-gc))) * u
    o_ref[...] += _dot(act.astype(BF16), wd_ref[...])


def _ffn(xn, xn_meta, h1, wg, wu, wd, cw, cb, nb, tm, tf):
    rows = xn.shape[0]
    nt = rows // tm
    tpb = nt // nb
    hb = tm // HALO
    last_hb = rows // HALO - 1
    n_ff = D_FF // tf
    j_bottom = n_ff // 2
    return pl.pallas_call(
        functools.partial(_ffn_body, tm=tm, tpb=tpb, j_bottom=j_bottom),
        grid=(nt, n_ff),
        in_specs=[
            pl.BlockSpec((tm, D_MODEL), lambda i, j: (i, 0)),
            pl.BlockSpec((HALO, D_MODEL), lambda i, j: (jnp.maximum(i * hb - 1, 0), 0)),
            pl.BlockSpec((HALO, D_MODEL), lambda i, j: (jnp.minimum((i + 1) * hb, last_hb), 0)),
            pl.BlockSpec((HALO, D_MODEL), lambda i, j: ((i // tpb) * (META_ROWS // HALO), 0)),
            pl.BlockSpec((tm // 2, D_MODEL), lambda i, j: (2 * i + jnp.where(j >= j_bottom, 1, 0), 0)),
            pl.BlockSpec((D_MODEL, tf), lambda i, j: (0, j)),
            pl.BlockSpec((D_MODEL, tf), lambda i, j: (0, j)),
            pl.BlockSpec((tf, D_MODEL), lambda i, j: (j, 0)),
            pl.BlockSpec((3, tf), lambda i, j: (0, j)),
            pl.BlockSpec((1, tf), lambda i, j: (0, j)),
        ],
        out_specs=pl.BlockSpec((tm, D_MODEL), lambda i, j: (i, 0)),
        out_shape=jax.ShapeDtypeStruct((rows, D_MODEL), F32),
        scratch_shapes=[pltpu.VMEM((tm + 2 * HALO, D_MODEL), BF16)],
        compiler_params=_cparams("parallel", "arbitrary"),
        name="ffn",
    )(xn, xn, xn, xn_meta, h1, wg, wu, wd, cw, cb)


def _rope_tables(first, n):
    inv = ROPE_THETA ** (-np.arange(0, ROPE_DIM, 2, dtype=np.float64) / ROPE_DIM)
    ang = (first + np.arange(n, dtype=np.float64))[:, None] * inv[None, :]
    cos, sin = np.cos(ang), np.sin(ang)
    z = np.zeros_like(cos)
    tables = (np.concatenate([cos, cos, z, z], axis=-1), np.concatenate([-sin, sin, z, z], axis=-1), cos.T, sin.T)
    return tuple(jnp.asarray(t, F32) for t in tables)


def _softmax_shift(q_gain, k_gain):
    bound = LOG2E * QK_DIM ** 0.5 * jnp.max(jnp.abs(q_gain)) * jnp.max(jnp.abs(k_gain))
    return bound * (1.02 * (1.0 + 2.0 ** -7))


def _na_shift(q_gain, k_gain, rpb, meta_bias):
    qk = LOG2E * HEAD_DIM ** 0.5 * jnp.max(jnp.abs(q_gain)) * jnp.max(jnp.abs(k_gain))
    return 1.02 * qk + LOG2E * jnp.maximum(jnp.max(rpb), jnp.max(meta_bias))


def _pad_lanes(v, width):
    return jnp.pad(v, ((0, 0), (0, width - v.shape[-1])))


def kernel(x, meta_tokens, mix_norm_g, w_in, na_q_g, na_k_g, na_rpb, na_meta_bias, mla_cq_g, mla_ckv_g,
           w_q_up, w_kv_up, mla_q_g, mla_k_g, na_out_g, mla_out_g, w_out, ffn_norm_g, w_gate, w_up,
           conv_w, conv_b, w_down):
    nb, seq, d = x.shape
    assert d == D_MODEL and w_in.shape[0] == 1, "one layer of width 2048"
    rows_grid = seq // GRID_W
    assert seq % max(IN_TM, UP_TM, ATT_TQ, FFN_TM) == 0 and rows_grid >= 4 * NA_QROWS
    assert seq % (2 * ATT_GROUP * UP_TM) == 0, "the latent-attention pipeline runs two key groups per trip"

    w_in_p = _pad_lanes(w_in[0], 4 * IN_TN).astype(BF16)
    w4 = w_in_p.reshape(D_MODEL, 4, IN_TN).transpose(1, 0, 2)
    wqt = w_q_up[0].reshape(Q_RANK, HEADS, QK_DIM).transpose(1, 2, 0).astype(BF16)
    wkv = w_kv_up[0].reshape(KV_RANK, HEADS, 2 * HEAD_DIM)
    wk = wkv[:, :, :HEAD_DIM].reshape(KV_RANK, HEADS // 2, 2 * HEAD_DIM).transpose(1, 0, 2).astype(BF16)
    wvt = wkv[:, :, HEAD_DIM:].transpose(1, 2, 0).astype(BF16)
    wo = w_out[0].astype(BF16)
    wg, wu, wd = w_gate[0].astype(BF16), w_up[0].astype(BF16), w_down[0].astype(BF16)
    row = lambda v: v.reshape(1, -1)
    kg_pad = _pad_lanes(mla_k_g, QK_PAD)
    na_shift = _na_shift(na_q_g, na_k_g, na_rpb, na_meta_bias)
    mb = jnp.pad(na_meta_bias[0] * LOG2E - na_shift, ((0, 0), (0, META_ROWS - N_META)),
                 constant_values=NEG).reshape(HEADS, 1, META_ROWS)

    xr = x.reshape(nb * seq, D_MODEL)
    xm = jnp.pad(meta_tokens.astype(x.dtype), ((0, META_ROWS - N_META), (0, 0)))
    rope_r = _rope_tables(N_META, seq)
    rope_m = _rope_tables(0, META_ROWS)

    inproj = functools.partial(_inproj, g=row(mix_norm_g), w4=w4, qg=row(na_q_g), kg=row(na_k_g),
                               cqg=row(mla_cq_g), ckvg=row(mla_ckv_g))
    qa, ka, va, cq, ckv, kpe = inproj(xr, tm=IN_TM)
    qa_m, ka_m, va_m, cq_m, ckv_m, kpe_m = inproj(xm, tm=META_ROWS)

    up = functools.partial(_mla_up, wqt=wqt, wk=wk, wvt=wvt, qg=mla_q_g, kg=kg_pad,
                           shift=_softmax_shift(mla_q_g, mla_k_g))
    qt, kk, vt = up(cq, ckv, kpe, *rope_r, nb=nb, tm=UP_TM)
    qt_m, kk_m, vt_m = up(cq_m, ckv_m, kpe_m, *rope_m, nb=1, tm=META_ROWS)

    bias = _na_bias(na_rpb[0], na_shift, rows_grid)
    a_n = _na_attn(qa, ka, va, ka_m, va_m, bias, mb, row(na_out_g), nb)
    a_n_m = _na_meta(qa_m, ka_m, va_m, mb, row(na_out_g))

    out_b = _mla_attn(qt, kk, vt, kk_m, vt_m, ATT_TQ, shared_q=False)
    out_b_m = _mla_attn(qt_m, kk, vt, kk_m, vt_m, META_ROWS, shared_q=True)

    op = functools.partial(_outproj, bg=row(mla_out_g), w=wo, fg=row(ffn_norm_g))
    ident = lambda i: (i, 0)
    zero = lambda i: (0, 0)
    h1, xn2 = op(xr, a_n, out_b, tm=OUT_TM, x_map=ident, a_map=ident, rows=nb * seq)
    _, xn2_m = op(xm, a_n_m, out_b_m, tm=META_ROWS, x_map=zero, a_map=zero, rows=nb * META_ROWS)

    out = _ffn(xn2, xn2_m, h1, wg, wu, wd, conv_w[0], row(conv_b), nb, FFN_TM, FFN_TF)
    return out.reshape(nb, seq, D_MODEL)
```

```python
import functools

import jax
import jax.numpy as jnp
import numpy as np
from jax import lax
from jax.experimental import pallas as pl
from jax.experimental.pallas import tpu as pltpu

F32 = jnp.float32
BF16 = jnp.bfloat16

LANES = 128
BF16_SUBLANES = 16
VMEM_LIMIT_BYTES = 56 * 1024 * 1024

D_MODEL = 2048
GRID_W = 64
N_META = 16
EPS = 1e-6
NEG = -1e30
LOG2E = 1.4426950408889634
HEADS = 8
HEAD_DIM = 128
NA_WIDTH = HEADS * HEAD_DIM
NA_WIN_ROWS = 8
NA_WIN_COLS = 16
Q_RANK = 512
KV_RANK = 256
ROPE_DIM = 64
QK_DIM = HEAD_DIM + ROPE_DIM
QK_PAD = 2 * LANES
ROPE_THETA = 10000.0
D_FF = 5632
META_ROWS = LANES
SOFTMAX_UNDERFLOW = 2.0 ** -100

IN_TM = 1024
IN_TN = 1024
UP_TM = 512
ATT_TQ = 1024
ATT_GROUP = 8
NA_QROWS = 4
NA_TQ = NA_QROWS * GRID_W
NA_KROWS = 3 * NA_QROWS
OUT_TM = 512
FFN_TM = 1024
FFN_TF = 512
HALO = BF16_SUBLANES


def _cparams(*sem):
    return pltpu.CompilerParams(dimension_semantics=sem, vmem_limit_bytes=VMEM_LIMIT_BYTES)


def _rms(v):
    return v * lax.rsqrt(jnp.mean(v * v, axis=-1, keepdims=True) + EPS)


def _dot(a, b):
    return jnp.dot(a, b, preferred_element_type=F32)


def _inproj_body(x_ref, g_ref, w_ref, qg_ref, kg_ref, cqg_ref, ckvg_ref,
                 qa_ref, ka_ref, va_ref, cq_ref, ckv_ref, kpe_ref, xn_ref, *, q_scale):
    j = pl.program_id(1)

    @pl.when(j == 0)
    def _():
        xn_ref[...] = (_rms(x_ref[...]) * g_ref[...]).astype(BF16)

    def project(lo, hi):
        return _dot(xn_ref[...], w_ref[:, lo:hi])

    def head_norm(gain_ref, out_ref, post, transposed):
        for t in range(HEADS // 2):
            y = project(2 * t * HEAD_DIM, 2 * (t + 1) * HEAD_DIM)
            for u in range(2):
                sl = slice((2 * t + u) * HEAD_DIM, (2 * t + u + 1) * HEAD_DIM)
                yh = _rms(y[:, u * HEAD_DIM:(u + 1) * HEAD_DIM]) * gain_ref[...] * post
                if transposed:
                    out_ref[sl, :] = yh.T.astype(BF16)
                else:
                    out_ref[:, sl] = yh.astype(BF16)

    @pl.when(j == 0)
    def _():
        head_norm(qg_ref, qa_ref, q_scale, False)

    @pl.when(j == 1)
    def _():
        head_norm(kg_ref, ka_ref, 1.0, True)

    @pl.when(j == 2)
    def _():
        va_ref[...] = project(0, IN_TN).astype(BF16)

    @pl.when(j == 3)
    def _():
        cq_ref[...] = (_rms(project(0, Q_RANK)) * cqg_ref[...]).astype(BF16)
        y = project(Q_RANK, IN_TN)
        ckv_ref[...] = (_rms(y[:, :KV_RANK]) * ckvg_ref[...]).astype(BF16)
        kpe_ref[...] = y[:, KV_RANK:KV_RANK + LANES]


def _inproj(x, g, w4, qg, kg, cqg, ckvg, tm):
    rows = x.shape[0]
    row = lambda i, j: (i, 0)
    const = lambda i, j: (0, 0)
    outs = (
        jax.ShapeDtypeStruct((rows, NA_WIDTH), BF16),
        jax.ShapeDtypeStruct((NA_WIDTH, rows), BF16),
        jax.ShapeDtypeStruct((rows, NA_WIDTH), BF16),
        jax.ShapeDtypeStruct((rows, Q_RANK), BF16),
        jax.ShapeDtypeStruct((rows, KV_RANK), BF16),
        jax.ShapeDtypeStruct((rows, LANES), F32),
    )
    return pl.pallas_call(
        functools.partial(_inproj_body, q_scale=HEAD_DIM ** -0.5 * LOG2E),
        grid=(rows // tm, 4),
        in_specs=[
            pl.BlockSpec((tm, D_MODEL), row),
            pl.BlockSpec((1, D_MODEL), const),
            pl.BlockSpec((None, D_MODEL, IN_TN), lambda i, j: (j, 0, 0)),
            pl.BlockSpec((1, HEAD_DIM), const),
            pl.BlockSpec((1, HEAD_DIM), const),
            pl.BlockSpec((1, Q_RANK), const),
            pl.BlockSpec((1, KV_RANK), const),
        ],
        out_specs=[
            pl.BlockSpec((tm, NA_WIDTH), row),
            pl.BlockSpec((NA_WIDTH, tm), lambda i, j: (0, i)),
            pl.BlockSpec((tm, NA_WIDTH), row),
            pl.BlockSpec((tm, Q_RANK), row),
            pl.BlockSpec((tm, KV_RANK), row),
            pl.BlockSpec((tm, LANES), row),
        ],
        out_shape=outs,
        scratch_shapes=[pltpu.VMEM((tm, D_MODEL), BF16)],
        compiler_params=_cparams("parallel", "arbitrary"),
        name="inproj",
    )(x, g, w4, qg, kg, cqg, ckvg)


def _mla_up_body(cq_ref, ckv_ref, kpe_ref, cos_ref, sin_ref, cost_ref, sint_ref,
                 wqt_ref, wk_ref, wvt_ref, qgt_ref, qpad_ref, kg_ref, qt_ref, k_ref, vt_ref, *, q_scale):
    half = ROPE_DIM // 2
    ckv = ckv_ref[...]
    cqt = cq_ref[...].astype(F32).T.astype(BF16)
    ckvt = ckv.astype(F32).T.astype(BF16)
    qgt = qgt_ref[...]
    cost = cost_ref[...]
    sint = sint_ref[...]
    cosf = cos_ref[...]
    sinf = sin_ref[...]
    kg = kg_ref[...]

    kpe = kpe_ref[...]
    kpe_ss = jnp.sum(kpe * kpe, axis=-1, keepdims=True)
    r = kpe * kg[:, LANES:]
    kpe_rot = r * cosf + (pltpu.roll(r, half, 1) + pltpu.roll(r, LANES - half, 1)) * sinf

    for h in range(HEADS):
        q = _dot(wqt_ref[h], cqt)
        ms = jnp.sum(q * q, axis=0, keepdims=True) * (1.0 / QK_DIM)
        qn = q * (lax.rsqrt(ms + EPS) * q_scale) * qgt
        x1 = qn[LANES:LANES + half]
        x2 = qn[LANES + half:LANES + 2 * half]
        qt_ref[h, 0:LANES, :] = qn[:LANES].astype(BF16)
        qt_ref[h, LANES:LANES + half, :] = (x1 * cost - x2 * sint).astype(BF16)
        qt_ref[h, LANES + half:LANES + 2 * half, :] = (x2 * cost + x1 * sint).astype(BF16)
        qt_ref[h, LANES + 2 * half:, :] = qpad_ref[...]
        vt_ref[h] = _dot(wvt_ref[h], ckvt).astype(BF16)

    one_lane = (lax.broadcasted_iota(jnp.int32, (1, LANES), 1) == ROPE_DIM).astype(F32)
    for t in range(HEADS // 2):
        kn2 = _dot(ckv, wk_ref[t])
        for u in range(2):
            kn = kn2[:, u * LANES:(u + 1) * LANES]
            ms = (jnp.sum(kn * kn, axis=-1, keepdims=True) + kpe_ss) * (1.0 / QK_DIM)
            rn = lax.rsqrt(ms + EPS)
            k_ref[2 * t + u] = jnp.concatenate([kn * rn * kg[:, :LANES], kpe_rot * rn + one_lane],
                                               axis=-1).astype(BF16)


def _mla_up(cq, ckv, kpe, cosf, sinf, cost, sint, wqt, wk, wvt, qg, kg, shift, nb, tm):
    rows = cq.shape[0]
    nt = rows // nb // tm
    row = lambda b, i: (b * nt + i, 0)
    pos = lambda b, i: (i, 0)
    post = lambda b, i: (0, i)
    c2 = lambda b, i: (0, 0)
    c3 = lambda b, i: (0, 0, 0)
    qgt = jnp.broadcast_to(qg.reshape(QK_DIM, 1), (QK_DIM, tm))
    qpad = jnp.zeros((QK_PAD - QK_DIM, tm), F32).at[0].set(-shift).astype(BF16)
    outs = (
        jax.ShapeDtypeStruct((nb, HEADS, QK_PAD, nt * tm), BF16),
        jax.ShapeDtypeStruct((nb, HEADS, nt, tm, QK_PAD), BF16),
        jax.ShapeDtypeStruct((nb, HEADS, nt, HEAD_DIM, tm), BF16),
    )
    return pl.pallas_call(
        functools.partial(_mla_up_body, q_scale=QK_DIM ** -0.5 * LOG2E),
        grid=(nb, nt),
        in_specs=[
            pl.BlockSpec((tm, Q_RANK), row),
            pl.BlockSpec((tm, KV_RANK), row),
            pl.BlockSpec((tm, LANES), row),
            pl.BlockSpec((tm, LANES), pos),
            pl.BlockSpec((tm, LANES), pos),
            pl.BlockSpec((ROPE_DIM // 2, tm), post),
            pl.BlockSpec((ROPE_DIM // 2, tm), post),
            pl.BlockSpec((HEADS, QK_DIM, Q_RANK), c3),
            pl.BlockSpec((HEADS // 2, KV_RANK, 2 * HEAD_DIM), c3),
            pl.BlockSpec((HEADS, HEAD_DIM, KV_RANK), c3),
            pl.BlockSpec((QK_DIM, tm), c2),
            pl.BlockSpec((QK_PAD - QK_DIM, tm), c2),
            pl.BlockSpec((1, QK_PAD), c2),
        ],
        out_specs=[
            pl.BlockSpec((None, HEADS, QK_PAD, tm), lambda b, i: (b, 0, 0, i)),
            pl.BlockSpec((None, HEADS, None, tm, QK_PAD), lambda b, i: (b, 0, i, 0, 0)),
            pl.BlockSpec((None, HEADS, None, HEAD_DIM, tm), lambda b, i: (b, 0, i, 0, 0)),
        ],
        out_shape=outs,
        compiler_params=_cparams("parallel", "parallel"),
        name="mla_up",
    )(cq, ckv, kpe, cosf, sinf, cost, sint, wqt, wk, wvt, qgt, qpad, kg)


def _mla_attn_body(qt_ref, qtn_ref, k_ref, vt_ref, km_ref, vmt_ref, o_ref, acc_ref, p_ref, lc_ref,
                   *, nk, carry_over):
    qt = qt_ref[...]
    ng = nk // ATT_GROUP

    def probs(g, half, q=None):
        q = qt if q is None else q
        ls = None
        for u in range(ATT_GROUP):
            p = jnp.exp2(_dot(k_ref[g * ATT_GROUP + u], q))
            s1 = jnp.sum(p, axis=0, keepdims=True)
            ls = s1 if ls is None else ls + s1
            p_ref[half * ATT_GROUP + u] = p.astype(BF16)
        return ls

    def values(g, half):
        pv = None
        for u in range(ATT_GROUP):
            d = _dot(vt_ref[g * ATT_GROUP + u], p_ref[half * ATT_GROUP + u])
            pv = d if pv is None else pv + d
        acc_ref[...] += pv

    def meta_scores():
        s = _dot(km_ref[...], qt)
        key = lax.broadcasted_iota(jnp.int32, s.shape, 0)
        return jnp.where(key < N_META, s, NEG)

    def two_stages(i, l):
        l = l + probs(2 * i + 1, 1)
        values(2 * i, 0)
        l = l + probs(2 * i + 2, 0)
        values(2 * i + 1, 1)
        return l

    if carry_over:
        @pl.when(pl.program_id(2) == 0)
        def _():
            lc_ref[...] = probs(0, 0)
        l = lc_ref[...]
    else:
        l = probs(0, 0)

    p = jnp.exp2(meta_scores())
    acc_ref[...] = _dot(vmt_ref[...], p.astype(BF16))
    l = lax.fori_loop(0, ng // 2 - 1, two_stages, l + jnp.sum(p, axis=0, keepdims=True))
    l = l + probs(ng - 1, 1)
    values(ng - 2, 0)
    if carry_over:
        lc_ref[...] = probs(0, 0, qtn_ref[...])
    values(ng - 1, 1)
    o_ref[...] = acc_ref[...] / l

    @pl.when(jnp.min(l) < SOFTMAX_UNDERFLOW)
    def _():
        s = meta_scores()
        m0 = jnp.max(s, axis=0, keepdims=True)
        p = jnp.exp2(s - m0)
        acc_ref[...] = _dot(vmt_ref[...], p.astype(BF16))

        def chunk(c, carry):
            m_prev, l_prev = carry
            s = _dot(k_ref[c], qt)
            m_new = jnp.maximum(m_prev, jnp.max(s, axis=0, keepdims=True))
            alpha = jnp.exp2(m_prev - m_new)
            p = jnp.exp2(s - m_new)
            acc_ref[...] = alpha * acc_ref[...] + _dot(vt_ref[c], p.astype(BF16))
            return m_new, alpha * l_prev + jnp.sum(p, axis=0, keepdims=True)

        _, l_exact = lax.fori_loop(0, nk, chunk, (m0, jnp.sum(p, axis=0, keepdims=True)))
        o_ref[...] = acc_ref[...] / l_exact


def _mla_attn(qt, k, vt, km, vmt, tq, shared_q):
    nb, _, nk, tk, _ = k.shape
    nq = qt.shape[3] // tq
    qb = (lambda b: 0) if shared_q else (lambda b: b)
    return pl.pallas_call(
        functools.partial(_mla_attn_body, nk=nk, carry_over=nq > 1),
        grid=(nb, HEADS, nq),
        in_specs=[
            pl.BlockSpec((None, None, QK_PAD, tq), lambda b, h, i: (qb(b), h, 0, i)),
            pl.BlockSpec((None, None, QK_PAD, tq), lambda b, h, i: (qb(b), h, 0, jnp.minimum(i + 1, nq - 1))),
            pl.BlockSpec((None, None, nk, tk, QK_PAD), lambda b, h, i: (b, h, 0, 0, 0)),
            pl.BlockSpec((None, None, nk, HEAD_DIM, tk), lambda b, h, i: (b, h, 0, 0, 0)),
            pl.BlockSpec((None, None, None, META_ROWS, QK_PAD), lambda b, h, i: (0, h, 0, 0, 0)),
            pl.BlockSpec((None, None, None, HEAD_DIM, META_ROWS), lambda b, h, i: (0, h, 0, 0, 0)),
        ],
        out_specs=pl.BlockSpec((None, HEAD_DIM, tq), lambda b, h, i: (b, h, i)),
        out_shape=jax.ShapeDtypeStruct((nb, NA_WIDTH, nq * tq), F32),
        scratch_shapes=[pltpu.VMEM((HEAD_DIM, tq), F32), pltpu.VMEM((2 * ATT_GROUP, tk, tq), BF16),
                        pltpu.VMEM((1, tq), F32)],
        compiler_params=_cparams("parallel", "parallel", "arbitrary"),
        name="mla_attn",
    )(qt, qt, k, vt, km, vmt)


def _na_bias_body(rpb_ref, shift_ref, o_ref, t_ref, *, rows):
    n_dr = 2 * NA_WIN_ROWS - 1
    shape = (GRID_W, LANES)
    qc = lax.broadcasted_iota(jnp.int32, shape, 0)
    lane = lax.broadcasted_iota(jnp.int32, shape, 1)
    left = lane < GRID_W
    kc = jnp.where(left, lane, lane - GRID_W)
    cs = jnp.clip(qc - NA_WIN_COLS // 2, 0, GRID_W - NA_WIN_COLS)
    col_ok = (kc >= cs) & (kc < cs + NA_WIN_COLS)

    for dr in range(n_dr):
        x = jnp.broadcast_to(rpb_ref[dr:dr + 1, :], shape)
        a = pltpu.roll(x, LANES - (NA_WIN_COLS - 1), 1, stride=1, stride_axis=0)
        b = pltpu.roll(x, GRID_W - (NA_WIN_COLS - 1), 1, stride=1, stride_axis=0)
        t_ref[dr] = jnp.where(col_ok, jnp.where(left, a, b) * LOG2E - shift_ref[0], NEG)

    neg = jnp.full(shape, NEG, F32)
    for c, r0 in enumerate((0, 2 * NA_QROWS, rows - NA_QROWS)):
        for qr in range(NA_QROWS):
            abs_qr = r0 + qr
            rs = min(max(abs_qr - NA_WIN_ROWS // 2, 0), rows - NA_WIN_ROWS)
            for a in range(NA_KROWS // 2):
                halves = []
                for kr in (r0 - NA_QROWS + 2 * a, r0 - NA_QROWS + 2 * a + 1):
                    in_window = rs <= kr < rs + NA_WIN_ROWS
                    halves.append(t_ref[kr - abs_qr + NA_WIN_ROWS - 1] if in_window else neg)
                o_ref[c, qr * GRID_W:(qr + 1) * GRID_W, a * LANES:(a + 1) * LANES] = (
                    jnp.where(left, halves[0], halves[1]))


def _na_bias(rpb, shift, rows):
    n_dr, n_dc = rpb.shape[1:]
    rpb = jnp.pad(rpb, ((0, 0), (0, -n_dr % 8), (0, LANES - n_dc)))
    return pl.pallas_call(
        functools.partial(_na_bias_body, rows=rows),
        grid=(HEADS,),
        in_specs=[pl.BlockSpec((None,) + rpb.shape[1:], lambda h: (h, 0, 0)),
                  pl.BlockSpec(memory_space=pltpu.SMEM)],
        out_specs=pl.BlockSpec((3, None, NA_TQ, NA_KROWS * GRID_W), lambda h: (0, h, 0, 0)),
        out_shape=jax.ShapeDtypeStruct((3, HEADS, NA_TQ, NA_KROWS * GRID_W), F32),
        scratch_shapes=[pltpu.VMEM((2 * NA_WIN_ROWS - 1, GRID_W, LANES), F32)],
        compiler_params=_cparams("parallel"),
        name="na_bias",
    )(rpb, shift.reshape(1))


def _na_body(q_ref, kp_ref, kc_ref, kn_ref, vp_ref, vc_ref, vn_ref, km_ref, vm_ref,
             bias_ref, mb_ref, g_ref, o_ref, out_ref, p_ref, l_ref):
    k_blocks = (kp_ref, kc_ref, kn_ref, km_ref)
    v_blocks = (vp_ref, vc_ref, vn_ref, vm_ref)
    cols = [slice(c * NA_TQ, (c + 1) * NA_TQ) for c in range(3)] + [slice(3 * NA_TQ, 3 * NA_TQ + META_ROWS)]

    def scores(h, c):
        sl = slice(h * HEAD_DIM, (h + 1) * HEAD_DIM)
        bias = mb_ref[h] if c == 3 else bias_ref[h, :, cols[c]]
        return _dot(q_ref[:, sl], k_blocks[c][sl, :]) + bias

    l_min = None
    for h in range(HEADS):
        l = None
        for c in range(4):
            p = jnp.exp2(scores(h, c))
            s1 = jnp.sum(p, axis=-1, keepdims=True)
            l = s1 if l is None else l + s1
            p_ref[h, :, cols[c]] = p.astype(BF16)
        l_ref[h] = jnp.broadcast_to(l, (NA_TQ, LANES))
        l_min = l if l_min is None else jnp.minimum(l_min, l)

    for h in range(HEADS):
        sl = slice(h * HEAD_DIM, (h + 1) * HEAD_DIM)
        o = None
        for c in (3, 0, 1, 2):
            d = _dot(p_ref[h, :, cols[c]], v_blocks[c][:, sl])
            o = d if o is None else o + d
        out_ref[:, sl] = o / l_ref[h]

    @pl.when(jnp.min(l_min) < SOFTMAX_UNDERFLOW)
    def _():
        for h in range(HEADS):
            sl = slice(h * HEAD_DIM, (h + 1) * HEAD_DIM)
            s = jnp.concatenate([scores(h, c) for c in range(4)], axis=-1)
            p = jnp.exp2(s - jnp.max(s, axis=-1, keepdims=True))
            pb = p.astype(BF16)
            o = None
            for c in (3, 0, 1, 2):
                d = _dot(pb[:, cols[c]], v_blocks[c][:, sl])
                o = d if o is None else o + d
            out_ref[:, sl] = o / jnp.sum(p, axis=-1, keepdims=True)

    o_ref[...] = (_rms(out_ref[...]) * g_ref[...]).astype(BF16)


def _na_attn(qa, kat, va, kmt, vm, bias, mb, g, nb):
    rows = qa.shape[0]
    n = rows // nb // NA_TQ
    cur = lambda b, i: (b * n + i, 0)
    prev = lambda b, i: (b * n + jnp.maximum(i - 1, 0), 0)
    nxt = lambda b, i: (b * n + jnp.minimum(i + 1, n - 1), 0)
    blk = lambda f: pl.BlockSpec((NA_TQ, NA_WIDTH), f)
    blk_t = lambda f: pl.BlockSpec((NA_WIDTH, NA_TQ), lambda b, i: f(b, i)[::-1])
    case = lambda b, i: (jnp.where(i == 0, 0, jnp.where(i == n - 1, 2, 1)), 0, 0, 0)
    return pl.pallas_call(
        _na_body,
        grid=(nb, n),
        in_specs=[
            blk(cur), blk_t(prev), blk_t(cur), blk_t(nxt), blk(prev), blk(cur), blk(nxt),
            pl.BlockSpec((NA_WIDTH, META_ROWS), lambda b, i: (0, 0)),
            pl.BlockSpec((META_ROWS, NA_WIDTH), lambda b, i: (0, 0)),
            pl.BlockSpec((None, HEADS, NA_TQ, NA_KROWS * GRID_W), case),
            pl.BlockSpec((HEADS, 1, META_ROWS), lambda b, i: (0, 0, 0)),
            pl.BlockSpec((1, NA_WIDTH), lambda b, i: (0, 0)),
        ],
        out_specs=blk(cur),
        out_shape=jax.ShapeDtypeStruct((rows, NA_WIDTH), BF16),
        scratch_shapes=[pltpu.VMEM((NA_TQ, NA_WIDTH), F32),
                        pltpu.VMEM((HEADS, NA_TQ, NA_KROWS * GRID_W + META_ROWS), BF16),
                        pltpu.VMEM((HEADS, NA_TQ, LANES), F32)],
        compiler_params=_cparams("parallel", "arbitrary"),
        name="na_attn",
    )(qa, kat, kat, kat, va, va, va, kmt, vm, bias, mb, g)


def _na_meta_body(q_ref, kt_ref, v_ref, mb_ref, g_ref, o_ref, out_ref):
    for h in range(HEADS):
        sl = slice(h * HEAD_DIM, (h + 1) * HEAD_DIM)
        s = _dot(q_ref[:, sl], kt_ref[sl, :]) + mb_ref[h]
        m = jnp.max(s, axis=-1, keepdims=True)
        p = jnp.exp2(s - m)
        l = jnp.sum(p, axis=-1, keepdims=True)
        out_ref[:, sl] = _dot(p.astype(BF16), v_ref[:, sl]) / l
    o_ref[...] = (_rms(out_ref[...]) * g_ref[...]).astype(BF16)


def _na_meta(qm, kmt, vm, mb, g):
    full = lambda s: pl.BlockSpec(s, lambda i: (0,) * len(s))
    return pl.pallas_call(
        _na_meta_body,
        grid=(1,),
        in_specs=[full((META_ROWS, NA_WIDTH)), full((NA_WIDTH, META_ROWS)), full((META_ROWS, NA_WIDTH)),
                  full((HEADS, 1, META_ROWS)), full((1, NA_WIDTH))],
        out_specs=full((META_ROWS, NA_WIDTH)),
        out_shape=jax.ShapeDtypeStruct((META_ROWS, NA_WIDTH), BF16),
        scratch_shapes=[pltpu.VMEM((META_ROWS, NA_WIDTH), F32)],
        compiler_params=_cparams("arbitrary"),
        name="na_meta",
    )(qm, kmt, vm, mb, g)


def _outproj_body(x_ref, a_ref, bt_ref, bg_ref, w_ref, fg_ref, h_ref, xn_ref):
    bn = (_rms(bt_ref[...].T) * bg_ref[...]).astype(BF16)
    mix = jnp.concatenate([a_ref[...], bn], axis=-1)
    h = x_ref[...] + _dot(mix, w_ref[...])
    h_ref[...] = h
    xn_ref[...] = (_rms(h) * fg_ref[...]).astype(BF16)


def _outproj(x, a, bt, bg, w, fg, tm, x_map, a_map, rows):
    c = lambda i: (0, 0)
    tpb = bt.shape[2] // tm
    return pl.pallas_call(
        _outproj_body,
        grid=(rows // tm,),
        in_specs=[
            pl.BlockSpec((tm, D_MODEL), x_map),
            pl.BlockSpec((tm, NA_WIDTH), a_map),
            pl.BlockSpec((None, NA_WIDTH, tm), lambda i: (i // tpb, 0, i % tpb)),
            pl.BlockSpec((1, NA_WIDTH), c),
            pl.BlockSpec((2 * NA_WIDTH, D_MODEL), c),
            pl.BlockSpec((1, D_MODEL), c),
        ],
        out_specs=[pl.BlockSpec((tm, D_MODEL), lambda i: (i, 0)),
                   pl.BlockSpec((tm, D_MODEL), lambda i: (i, 0))],
        out_shape=(jax.ShapeDtypeStruct((rows, D_MODEL), F32),
                   jax.ShapeDtypeStruct((rows, D_MODEL), BF16)),
        compiler_params=_cparams("parallel"),
        name="outproj",
    )(x, a, bt, bg, w, fg)


def _ffn_body(xm_ref, xp_ref, xx_ref, xmeta_ref, h_ref, wg_ref, wu_ref, wd_ref, cw_ref, cb_ref,
              o_ref, xe_ref, *, tm, tpb, j_bottom):
    i = pl.program_id(0)
    j = pl.program_id(1)
    half = tm // 2

    @pl.when(j == 0)
    def _():
        first = (i % tpb) == 0
        last = (i % tpb) == tpb - 1
        xe_ref[0:HALO, :] = jnp.where(first, xmeta_ref[...], xp_ref[...])
        xe_ref[HALO:HALO + tm, :] = xm_ref[...]
        xe_ref[HALO + tm:2 * HALO + tm, :] = jnp.where(last, jnp.zeros_like(xx_ref[...]), xx_ref[...])
        o_ref[0:half, :] = h_ref[...]
        o_ref[half:, :] = jnp.zeros((tm - half, o_ref.shape[1]), F32)

    @pl.when(j == j_bottom)
    def _():
        o_ref[half:, :] += h_ref[...]

    g = _dot(xe_ref[...], wg_ref[...])
    u = _dot(xe_ref[HALO:HALO + tm, :], wu_ref[...])
    cw = cw_ref[...]
    gc = (cb_ref[...] + cw[0:1] * g[HALO - 1:HALO - 1 + tm]
          + cw[1:2] * g[HALO:HALO + tm] + cw[2:3] * g[HALO + 1:HALO + 1 + tm])
    act = gc * (1.0 / (1.0 + jnp.exp(-gc))) * u
    o_ref[...] += _dot(act.astype(BF16), wd_ref[...])


def _ffn(xn, xn_meta, h1, wg, wu, wd, cw, cb, nb, tm, tf):
    rows = xn.shape[0]
    nt = rows // tm
    tpb = nt // nb
    hb = tm // HALO
    last_hb = rows // HALO - 1
    n_ff = D_FF // tf
    j_bottom = n_ff // 2
    return pl.pallas_call(
        functools.partial(_ffn_body, tm=tm, tpb=tpb, j_bottom=j_bottom),
        grid=(nt, n_ff),
        in_specs=[
            pl.BlockSpec((tm, D_MODEL), lambda i, j: (i, 0)),
            pl.BlockSpec((HALO, D_MODEL), lambda i, j: (jnp.maximum(i * hb - 1, 0), 0)),
            pl.BlockSpec((HALO, D_MODEL), lambda i, j: (jnp.minimum((i + 1) * hb, last_hb), 0)),
            pl.BlockSpec((HALO, D_MODEL), lambda i, j: ((i // tpb) * (META_ROWS // HALO), 0)),
            pl.BlockSpec((tm // 2, D_MODEL), lambda i, j: (2 * i + jnp.where(j >= j_bottom, 1, 0), 0)),
            pl.BlockSpec((D_MODEL, tf), lambda i, j: (0, j)),
            pl.BlockSpec((D_MODEL, tf), lambda i, j: (0, j)),
            pl.BlockSpec((tf, D_MODEL), lambda i, j: (j, 0)),
            pl.BlockSpec((3, tf), lambda i, j: (0, j)),
            pl.BlockSpec((1, tf), lambda i, j: (0, j)),
        ],
        out_specs=pl.BlockSpec((tm, D_MODEL), lambda i, j: (i, 0)),
        out_shape=jax.ShapeDtypeStruct((rows, D_MODEL), F32),
        scratch_shapes=[pltpu.VMEM((tm + 2 * HALO, D_MODEL), BF16)],
        compiler_params=_cparams("parallel", "arbitrary"),
        name="ffn",
    )(xn, xn, xn, xn_meta, h1, wg, wu, wd, cw, cb)


def _rope_tables(first, n):
    inv = ROPE_THETA ** (-np.arange(0, ROPE_DIM, 2, dtype=np.float64) / ROPE_DIM)
    ang = (first + np.arange(n, dtype=np.float64))[:, None] * inv[None, :]
    cos, sin = np.cos(ang), np.sin(ang)
    z = np.zeros_like(cos)
    tables = (np.concatenate([cos, cos, z, z], axis=-1), np.concatenate([-sin, sin, z, z], axis=-1), cos.T, sin.T)
    return tuple(jnp.asarray(t, F32) for t in tables)


def _softmax_shift(q_gain, k_gain):
    bound = LOG2E * QK_DIM ** 0.5 * jnp.max(jnp.abs(q_gain)) * jnp.max(jnp.abs(k_gain))
    return bound * (1.02 * (1.0 + 2.0 ** -7))


def _na_shift(q_gain, k_gain, rpb, meta_bias):
    qk = LOG2E * HEAD_DIM ** 0.5 * jnp.max(jnp.abs(q_gain)) * jnp.max(jnp.abs(k_gain))
    return 1.02 * qk + LOG2E * jnp.maximum(jnp.max(rpb), jnp.max(meta_bias))


def _pad_lanes(v, width):
    return jnp.pad(v, ((0, 0), (0, width - v.shape[-1])))


def kernel(x, meta_tokens, mix_norm_g, w_in, na_q_g, na_k_g, na_rpb, na_meta_bias, mla_cq_g, mla_ckv_g,
           w_q_up, w_kv_up, mla_q_g, mla_k_g, na_out_g, mla_out_g, w_out, ffn_norm_g, w_gate, w_up,
           conv_w, conv_b, w_down):
    nb, seq, d = x.shape
    assert d == D_MODEL and w_in.shape[0] == 1, "one layer of width 2048"
    rows_grid = seq // GRID_W
    assert seq % max(IN_TM, UP_TM, ATT_TQ, FFN_TM) == 0 and rows_grid >= 4 * NA_QROWS
    assert seq % (2 * ATT_GROUP * UP_TM) == 0, "the latent-attention pipeline runs two key groups per trip"

    w_in_p = _pad_lanes(w_in[0], 4 * IN_TN).astype(BF16)
    w4 = w_in_p.reshape(D_MODEL, 4, IN_TN).transpose(1, 0, 2)
    wqt = w_q_up[0].reshape(Q_RANK, HEADS, QK_DIM).transpose(1, 2, 0).astype(BF16)
    wkv = w_kv_up[0].reshape(KV_RANK, HEADS, 2 * HEAD_DIM)
    wk = wkv[:, :, :HEAD_DIM].reshape(KV_RANK, HEADS // 2, 2 * HEAD_DIM).transpose(1, 0, 2).astype(BF16)
    wvt = wkv[:, :, HEAD_DIM:].transpose(1, 2, 0).astype(BF16)
    wo = w_out[0].astype(BF16)
    wg, wu, wd = w_gate[0].astype(BF16), w_up[0].astype(BF16), w_down[0].astype(BF16)
    row = lambda v: v.reshape(1, -1)
    kg_pad = _pad_lanes(mla_k_g, QK_PAD)
    na_shift = _na_shift(na_q_g, na_k_g, na_rpb, na_meta_bias)
    mb = jnp.pad(na_meta_bias[0] * LOG2E - na_shift, ((0, 0), (0, META_ROWS - N_META)),
                 constant_values=NEG).reshape(HEADS, 1, META_ROWS)

    xr = x.reshape(nb * seq, D_MODEL)
    xm = jnp.pad(meta_tokens.astype(x.dtype), ((0, META_ROWS - N_META), (0, 0)))
    rope_r = _rope_tables(N_META, seq)
    rope_m = _rope_tables(0, META_ROWS)

    inproj = functools.partial(_inproj, g=row(mix_norm_g), w4=w4, qg=row(na_q_g), kg=row(na_k_g),
                               cqg=row(mla_cq_g), ckvg=row(mla_ckv_g))
    qa, ka, va, cq, ckv, kpe = inproj(xr, tm=IN_TM)
    qa_m, ka_m, va_m, cq_m, ckv_m, kpe_m = inproj(xm, tm=META_ROWS)

    up = functools.partial(_mla_up, wqt=wqt, wk=wk, wvt=wvt, qg=mla_q_g, kg=kg_pad,
                           shift=_softmax_shift(mla_q_g, mla_k_g))
    qt, kk, vt = up(cq, ckv, kpe, *rope_r, nb=nb, tm=UP_TM)
    qt_m, kk_m, vt_m = up(cq_m, ckv_m, kpe_m, *rope_m, nb=1, tm=META_ROWS)

    bias = _na_bias(na_rpb[0], na_shift, rows_grid)
    a_n = _na_attn(qa, ka, va, ka_m, va_m, bias, mb, row(na_out_g), nb)
    a_n_m = _na_meta(qa_m, ka_m, va_m, mb, row(na_out_g))

    out_b = _mla_attn(qt, kk, vt, kk_m, vt_m, ATT_TQ, shared_q=False)
    out_b_m = _mla_attn(qt_m, kk, vt, kk_m, vt_m, META_ROWS, shared_q=True)

    op = functools.partial(_outproj, bg=row(mla_out_g), w=wo, fg=row(ffn_norm_g))
    ident = lambda i: (i, 0)
    zero = lambda i: (0, 0)
    h1, xn2 = op(xr, a_n, out_b, tm=OUT_TM, x_map=ident, a_map=ident, rows=nb * seq)
    _, xn2_m = op(xm, a_n_m, out_b_m, tm=META_ROWS, x_map=zero, a_map=zero, rows=nb * META_ROWS)

    out = _ffn(xn2, xn2_m, h1, wg, wu, wd, conv_w[0], row(conv_b), nb, FFN_TM, FFN_TF)
    return out.reshape(nb, seq, D_MODEL)
```
